```python
import jax, jax.numpy as jnp
from jax import lax
import numpy as np

D_MODEL = 4096
BATCH = 32
SEQ = 256
DEPTH = 1
DEC_BATCH = 8
DEC_SEQ = 1024
PAST_LEN = 256

GRID_W = 64
HEAD_DIM = 128
N_Q_HEADS = 32
N_KV_HEADS = 8
GQA_GROUP = N_Q_HEADS // N_KV_HEADS
WINDOW = 128
BLOCK = 128
ROPE_THETA = 10000.0
GLA_HEADS = 4
GLA_DK = D_MODEL // 2
GLA_DV = D_MODEL
GLA_HK = GLA_DK // GLA_HEADS
GLA_HV = GLA_DV // GLA_HEADS
GLA_GATE_RANK = 16
GLA_TAU = 16.0
GLA_CHUNK = 64
D_FF = -(-8 * D_MODEL // (3 * 256)) * 256
N_MOD = 6
EPS = 1e-6
SPLITS = (N_Q_HEADS * HEAD_DIM, N_KV_HEADS * HEAD_DIM, N_KV_HEADS * HEAD_DIM,
          GLA_DK, GLA_DK, GLA_DV, GLA_DV, D_MODEL, D_MODEL, GLA_GATE_RANK, GLA_GATE_RANK)
D_IN = sum(SPLITS)

kernel_name = "hybrid_dit_window_gqa_gla_step"


def rms_norm(x, g):
    xf = x.astype(jnp.float32)
    y = xf * lax.rsqrt(jnp.mean(xf * xf, axis=-1, keepdims=True) + EPS)
    return (y * g.astype(jnp.float32)).astype(x.dtype)


def modulate(x, g, shift, scale):
    return rms_norm(x, g) * (1 + scale[:, None, :]) + shift[:, None, :]


def axial_rope(x, rows):
    half = HEAD_DIM // 2
    row = jnp.repeat(jnp.arange(rows, dtype=jnp.float32), GRID_W)
    col = jnp.tile(jnp.arange(GRID_W, dtype=jnp.float32), rows)
    inv = ROPE_THETA ** (-jnp.arange(0, half, 2, dtype=jnp.float32) / half)

    def rot(xa, pos):
        ang = pos[:, None] * inv[None, :]
        cos = jnp.cos(ang)[None, :, None, :]
        sin = jnp.sin(ang)[None, :, None, :]
        x1, x2 = jnp.split(xa, 2, axis=-1)
        return jnp.concatenate([x1 * cos - x2 * sin, x1 * sin + x2 * cos], axis=-1)

    xf = x.astype(jnp.float32)
    return jnp.concatenate([rot(xf[..., :half], row), rot(xf[..., half:], col)], axis=-1).astype(x.dtype)


def context_attention(q, k, v, sink):
    B, S = q.shape[:2]
    qg = q.reshape(B, S, N_KV_HEADS, GQA_GROUP, HEAD_DIM)
    s = jnp.einsum('bqhgd,bkhd->bhgqk', qg, k).astype(jnp.float32) * (HEAD_DIM ** -0.5)
    sink_b = jnp.broadcast_to(sink.reshape(N_KV_HEADS, GQA_GROUP)[None, :, :, None, None].astype(jnp.float32),
                              s.shape[:-1] + (1,))
    p = jax.nn.softmax(jnp.concatenate([s, sink_b], axis=-1), axis=-1)[..., :-1]
    o = jnp.einsum('bhgqk,bkhd->bqhgd', p.astype(v.dtype), v)
    return o.reshape(B, S, N_Q_HEADS * HEAD_DIM)


def latent_attention(q, k, v, ck, cv, sink):
    B, T = q.shape[:2]
    nb = T // BLOCK
    P = ck.shape[1]
    scale = HEAD_DIM ** -0.5
    qb = q.reshape(B, nb, BLOCK, N_KV_HEADS, GQA_GROUP, HEAD_DIM)

    def band(a):
        ab = a.reshape(B, nb, BLOCK, N_KV_HEADS, HEAD_DIM)
        ap = jnp.pad(ab, ((0, 0), (1, 1), (0, 0), (0, 0), (0, 0)))
        return jnp.concatenate([ap[:, :-2], ap[:, 1:-1], ap[:, 2:]], axis=2)

    kw, vw = band(k), band(v)
    qi = jnp.arange(nb)[:, None, None] * BLOCK + jnp.arange(BLOCK)[None, :, None]
    kj = (jnp.arange(nb)[:, None, None] - 1) * BLOCK + jnp.arange(3 * BLOCK)[None, None, :]
    valid = (jnp.abs(qi - kj) <= WINDOW) & (kj >= 0) & (kj < T)
    s_win = jnp.einsum('bnqhgd,bnkhd->bnhgqk', qb, kw).astype(jnp.float32) * scale
    s_win = jnp.where(valid[None, :, None, None], s_win, jnp.finfo(jnp.float32).min)
    s_ctx = jnp.einsum('bnqhgd,bphd->bnhgqp', qb, ck).astype(jnp.float32) * scale
    sink_b = jnp.broadcast_to(sink.reshape(N_KV_HEADS, GQA_GROUP)[None, None, :, :, None, None].astype(jnp.float32),
                              s_win.shape[:-1] + (1,))
    p = jax.nn.softmax(jnp.concatenate([s_win, s_ctx, sink_b], axis=-1), axis=-1)
    p_win = p[..., :3 * BLOCK].astype(v.dtype)
    p_ctx = p[..., 3 * BLOCK:3 * BLOCK + P].astype(v.dtype)
    o = (jnp.einsum('bnhgqk,bnkhd->bnqhgd', p_win, vw)
         + jnp.einsum('bnhgqp,bphd->bnqhgd', p_ctx, cv))
    return o.reshape(B, T, N_Q_HEADS * HEAD_DIM)


def gla_chunked(q, k, v, log_a, S0):
    B, T, H, dk = q.shape
    dv = v.shape[-1]
    nc = T // GLA_CHUNK

    def to_chunks(a):
        return a.astype(jnp.float32).reshape(B, nc, GLA_CHUNK, H, a.shape[-1]).transpose(1, 0, 3, 2, 4)

    qc, kc, vc, ac = to_chunks(q), to_chunks(k), to_chunks(v), to_chunks(log_a)
    mask = jnp.tril(jnp.ones((GLA_CHUNK, GLA_CHUNK), dtype=bool))

    def step(S, inp):
        qi, ki, vi, ai = inp
        b = jnp.cumsum(ai, axis=2)
        b_last = b[:, :, -1:]
        qe = qi * jnp.exp(b)
        ke = ki * jnp.exp(-b)
        att = jnp.where(mask, jnp.einsum('bhik,bhjk->bhij', qe, ke), 0.0)
        o = jnp.einsum('bhij,bhjv->bhiv', att, vi) + jnp.einsum('bhik,bhkv->bhiv', qe, S)
        S_new = (jnp.exp(b_last[:, :, 0])[..., None] * S
                 + jnp.einsum('bhjk,bhjv->bhkv', ki * jnp.exp(b_last - b), vi))
        return S_new, o

    S_fin, o = lax.scan(step, S0.astype(jnp.float32), (qc, kc, vc, ac))
    o = o.transpose(1, 0, 3, 2, 4).reshape(B, T, H, dv)
    return o, S_fin


def token_mixers(h, w_in, q_norm_g, k_norm_g, attn_sink, w_a2_f, b_a_f, w_a2_b, b_a_b,
                 gla_norm_g, w_out, ctx):
    B, T, _ = h.shape
    proj = h @ w_in
    offs = np.cumsum(SPLITS)[:-1].tolist()
    q, k, v, gq, gk, gv, gr, g_att, g_gla, a_f, a_b = jnp.split(proj, offs, axis=-1)

    q = rms_norm(q.reshape(B, T, N_Q_HEADS, HEAD_DIM), q_norm_g)
    k = rms_norm(k.reshape(B, T, N_KV_HEADS, HEAD_DIM), k_norm_g)
    v = v.reshape(B, T, N_KV_HEADS, HEAD_DIM)
    if ctx is None:
        o_att = context_attention(q, k, v, attn_sink)
        S_f0 = jnp.zeros((B, GLA_HEADS, GLA_HK, GLA_HV), jnp.float32)
        S_b0 = S_f0
    else:
        ck, cv, S_f0, S_b0 = ctx
        rows = T // GRID_W
        o_att = latent_attention(axial_rope(q, rows), axial_rope(k, rows), v, ck, cv, attn_sink)

    qg = gq.reshape(B, T, GLA_HEADS, GLA_HK) * (GLA_HK ** -0.5)
    kg = gk.reshape(B, T, GLA_HEADS, GLA_HK)
    vg = gv.reshape(B, T, GLA_HEADS, GLA_HV)
    la_f = (jax.nn.log_sigmoid((a_f @ w_a2_f + b_a_f).astype(jnp.float32)) / GLA_TAU).reshape(B, T, GLA_HEADS, GLA_HK)
    la_b = (jax.nn.log_sigmoid((a_b @ w_a2_b + b_a_b).astype(jnp.float32)) / GLA_TAU).reshape(B, T, GLA_HEADS, GLA_HK)
    o_f, S_f = gla_chunked(qg, kg, vg, la_f, S_f0)
    o_b_rev, S_b = gla_chunked(jnp.flip(qg, 1), jnp.flip(kg, 1), jnp.flip(vg, 1), jnp.flip(la_b, 1), S_b0)
    o_gla = o_f + jnp.flip(o_b_rev, 1)
    o_gla = rms_norm(o_gla, gla_norm_g.reshape(GLA_HEADS, GLA_HV)).reshape(B, T, GLA_DV).astype(h.dtype)
    o_gla = o_gla * jax.nn.silu(gr)

    merged = jax.nn.sigmoid(g_att) * o_att.astype(h.dtype) + jax.nn.sigmoid(g_gla) * o_gla
    return merged @ w_out, (k, v, S_f, S_b)


def trunk_layer(x, mod, norm1_g, norm2_g, w_in, q_norm_g, k_norm_g, attn_sink, w_a2_f, b_a_f,
                w_a2_b, b_a_b, gla_norm_g, w_out, w_ffn_in, w_ffn_out, ctx):
    shift1, scale1, gate1, shift2, scale2, gate2 = jnp.split(mod, N_MOD, axis=-1)
    h = modulate(x, norm1_g, shift1, scale1)
    mix, ctx_tensors = token_mixers(h, w_in, q_norm_g, k_norm_g, attn_sink, w_a2_f, b_a_f, w_a2_b, b_a_b,
                                    gla_norm_g, w_out, ctx)
    x = x + gate1[:, None, :] * mix
    h = modulate(x, norm2_g, shift2, scale2)
    g, u = jnp.split(h @ w_ffn_in, 2, axis=-1)
    x = x + gate2[:, None, :] * ((jax.nn.silu(g) * u) @ w_ffn_out)
    return x, ctx_tensors


def setup_inputs(seed: int = 0) -> dict:
    key = jax.random.key(seed)
    ks = jax.random.split(key, 26)

    def nrm(k, shape, scale):
        return jax.random.normal(k, shape, jnp.float32) * scale

    return {
        "x_prompt": nrm(ks[0], (BATCH, SEQ, D_MODEL), 1.0),
        "x_sample": nrm(ks[1], (DEC_BATCH, DEC_SEQ, D_MODEL), 1.0),
        "c": nrm(ks[2], (DEC_BATCH, D_MODEL), 1.0),
        "cache_k": nrm(ks[3], (DEC_BATCH, DEPTH, PAST_LEN, N_KV_HEADS, HEAD_DIM), 1.0),
        "cache_v": nrm(ks[4], (DEC_BATCH, DEPTH, PAST_LEN, N_KV_HEADS, HEAD_DIM), 1.0),
        "state_gla_fwd": nrm(ks[5], (DEC_BATCH, DEPTH, GLA_HEADS, GLA_HK, GLA_HV), 1.0),
        "state_gla_bwd": nrm(ks[6], (DEC_BATCH, DEPTH, GLA_HEADS, GLA_HK, GLA_HV), 1.0),
        "c_ctx": nrm(ks[7], (D_MODEL,), 1.0),
        "w_ada": nrm(ks[8], (DEPTH, D_MODEL, N_MOD * D_MODEL), D_MODEL ** -0.5),
        "b_ada": nrm(ks[9], (DEPTH, N_MOD * D_MODEL), 0.02),
        "norm1_g": 1.0 + nrm(ks[10], (DEPTH, D_MODEL), 0.02),
        "norm2_g": 1.0 + nrm(ks[11], (DEPTH, D_MODEL), 0.02),
        "w_in": nrm(ks[12], (DEPTH, D_MODEL, D_IN), D_MODEL ** -0.5),
        "q_norm_g": 1.0 + nrm(ks[13], (DEPTH, HEAD_DIM), 0.02),
        "k_norm_g": 1.0 + nrm(ks[14], (DEPTH, HEAD_DIM), 0.02),
        "attn_sink": nrm(ks[15], (DEPTH, N_Q_HEADS), 0.5),
        "w_a2_fwd": nrm(ks[16], (DEPTH, GLA_GATE_RANK, GLA_DK), GLA_GATE_RANK ** -0.5),
        "b_a_fwd": 1.0 + nrm(ks[17], (DEPTH, GLA_DK), 0.5),
        "w_a2_bwd": nrm(ks[18], (DEPTH, GLA_GATE_RANK, GLA_DK), GLA_GATE_RANK ** -0.5),
        "b_a_bwd": 1.0 + nrm(ks[19], (DEPTH, GLA_DK), 0.5),
        "gla_norm_g": 1.0 + nrm(ks[20], (DEPTH, GLA_DV), 0.02),
        "w_out": nrm(ks[21], (DEPTH, D_MODEL, D_MODEL), D_MODEL ** -0.5),
        "w_ffn_in": nrm(ks[22], (DEPTH, D_MODEL, 2 * D_FF), D_MODEL ** -0.5),
        "w_ffn_out": nrm(ks[23], (DEPTH, D_FF, D_MODEL), D_FF ** -0.5),
    }


def reference(x_prompt, x_sample, c, cache_k, cache_v, state_gla_fwd, state_gla_bwd, c_ctx,
              w_ada, b_ada, norm1_g, norm2_g, w_in, q_norm_g, k_norm_g, attn_sink,
              w_a2_fwd, b_a_fwd, w_a2_bwd, b_a_bwd, gla_norm_g, w_out, w_ffn_in, w_ffn_out):
    xp = x_prompt
    xs = x_sample
    new_k, new_v, new_sf, new_sb = [], [], [], []
    for l in range(DEPTH):
        layer_w = (norm1_g[l], norm2_g[l], w_in[l], q_norm_g[l], k_norm_g[l], attn_sink[l],
                   w_a2_fwd[l], b_a_fwd[l], w_a2_bwd[l], b_a_bwd[l], gla_norm_g[l], w_out[l],
                   w_ffn_in[l], w_ffn_out[l])
        mod_ctx = jax.nn.silu(c_ctx) @ w_ada[l] + b_ada[l]
        mod_ctx = jnp.broadcast_to(mod_ctx[None, :], (xp.shape[0], N_MOD * D_MODEL))
        xp, (k_l, v_l, sf_l, sb_l) = trunk_layer(xp, mod_ctx, *layer_w, None)
        new_k.append(k_l)
        new_v.append(v_l)
        new_sf.append(sf_l)
        new_sb.append(sb_l)
        mod_lat = jax.nn.silu(c) @ w_ada[l] + b_ada[l]
        xs, _ = trunk_layer(xs, mod_lat, *layer_w,
                            (cache_k[:, l], cache_v[:, l], state_gla_fwd[:, l], state_gla_bwd[:, l]))
    k_out = jnp.stack(new_k, axis=1)
    v_out = jnp.stack(new_v, axis=1)
    sf_out = jnp.stack(new_sf, axis=1).astype(x_prompt.dtype)
    sb_out = jnp.stack(new_sb, axis=1).astype(x_prompt.dtype)
    return (xp, xs, k_out, v_out, sf_out, sb_out)
```

```python
import functools

import jax
import jax.numpy as jnp
from jax import lax
from jax.experimental import pallas as pl
from jax.experimental.pallas import tpu as pltpu

F32 = jnp.float32
BF16 = jnp.bfloat16

D_MODEL = 4096
BATCH = 32
SEQ = 256
DEC_BATCH = 8
DEC_SEQ = 1024
PAST_LEN = 256
GRID_W = 64
HEAD_DIM = 128
N_Q_HEADS = 32
N_KV_HEADS = 8
GQA_GROUP = N_Q_HEADS // N_KV_HEADS
WINDOW = 128
BLOCK = 128
ROPE_THETA = 10000.0
GLA_HEADS = 4
GLA_DK = D_MODEL // 2
GLA_DV = D_MODEL
GLA_HK = GLA_DK // GLA_HEADS
GLA_HV = GLA_DV // GLA_HEADS
GLA_GATE_RANK = 16
GLA_TAU = 16.0
GLA_CHUNK = 64
D_FF = 11008
N_MOD = 6
EPS = 1e-6

OFF_Q = 0
OFF_K = 4096
OFF_V = 5120
OFF_GQ = 6144
OFF_GK = 8192
OFF_GV = 10240
OFF_GR = 14336
OFF_GATT = 18432
OFF_GGLA = 22528
D_WIDE = 26624
LANES = 128
MOD_ROWS = 16

MIB = 1024 * 1024
NT_DIMS = (((1,), (1,)), ((), ()))
TN_DIMS = (((0,), (0,)), ((), ()))


def _cp(sem, vmem_mib):
    return pltpu.CompilerParams(dimension_semantics=sem, vmem_limit_bytes=vmem_mib * MIB)


def _rms(x, g):
    ms = jnp.mean(x * x, axis=-1, keepdims=True)
    return x * lax.rsqrt(ms + EPS) * g


def _sigmoid(x):
    return 1.0 / (1.0 + jnp.exp(-x))


def _dot(a, b):
    return jnp.dot(a, b, preferred_element_type=F32)


ADA_TN = 512


def _ada_kernel(c_ref, w_ref, b_ref, o_ref):
    c = c_ref[...]
    s = (c * _sigmoid(c)).astype(BF16)
    o_ref[...] = _dot(s, w_ref[...].astype(BF16)) + b_ref[...]


def _ada(cc, w_ada, b_ada):
    n = w_ada.shape[1]
    return pl.pallas_call(
        _ada_kernel,
        grid=(n // ADA_TN,),
        in_specs=[
            pl.BlockSpec((MOD_ROWS, D_MODEL), lambda j: (0, 0)),
            pl.BlockSpec((D_MODEL, ADA_TN), lambda j: (0, j)),
            pl.BlockSpec((1, ADA_TN), lambda j: (0, j)),
        ],
        out_specs=pl.BlockSpec((MOD_ROWS, ADA_TN), lambda j: (0, j)),
        out_shape=jax.ShapeDtypeStruct((MOD_ROWS, n), F32),
        compiler_params=_cp(("arbitrary",), 40),
        name="ada_ln",
    )(cc, w_ada, b_ada)


PROJ_TM = 512
PROJ_TN = 1024


def _proj_kernel(x_ref, g_ref, sh_ref, sc_ref, w_ref, wa_ref, o_ref, a_ref, h_scr):
    @pl.when(pl.program_id(1) == 0)
    def _():
        h = _rms(x_ref[...], g_ref[...]) * (1.0 + sc_ref[...]) + sh_ref[...]
        hb = h.astype(BF16)
        h_scr[...] = hb
        a_ref[...] = _dot(hb, wa_ref[...])

    o_ref[...] = _dot(h_scr[...], w_ref[...])


def _mod_spec(mod_off, bpb):
    return pl.BlockSpec((None, 1, D_MODEL), lambda i, j: (mod_off + i // bpb, 0, 0))


def _proj(x, g1, shift, scale, w, wa, mod_off, rows_per_mod):
    m = x.shape[0]
    bpb = rows_per_mod // PROJ_TM
    return pl.pallas_call(
        _proj_kernel,
        grid=(m // PROJ_TM, D_WIDE // PROJ_TN),
        in_specs=[
            pl.BlockSpec((PROJ_TM, D_MODEL), lambda i, j: (i, 0)),
            pl.BlockSpec((1, D_MODEL), lambda i, j: (0, 0)),
            _mod_spec(mod_off, bpb),
            _mod_spec(mod_off, bpb),
            pl.BlockSpec((D_MODEL, PROJ_TN), lambda i, j: (0, j)),
            pl.BlockSpec((D_MODEL, LANES), lambda i, j: (0, 0)),
        ],
        out_specs=[
            pl.BlockSpec((PROJ_TM, PROJ_TN), lambda i, j: (i, j)),
            pl.BlockSpec((PROJ_TM, LANES), lambda i, j: (i, 0)),
        ],
        out_shape=[
            jax.ShapeDtypeStruct((m, D_WIDE), F32),
            jax.ShapeDtypeStruct((m, LANES), F32),
        ],
        scratch_shapes=[pltpu.VMEM((PROJ_TM, D_MODEL), BF16)],
        compiler_params=_cp(("arbitrary", "arbitrary"), 52),
        name="norm_mod_in_proj",
    )(x, g1, shift, scale, w, wa)


def _sink_col(sink_ref, h, rows_per_head):
    rid = lax.broadcasted_iota(jnp.int32, (GQA_GROUP * rows_per_head, 1), 0) // rows_per_head
    col = jnp.full(rid.shape, sink_ref[h * GQA_GROUP], F32)
    for g in range(1, GQA_GROUP):
        col = jnp.where(rid == g, sink_ref[h * GQA_GROUP + g], col)
    return col


def _ctx_attn_kernel(sink_ref, q_ref, k_ref, v_ref, qg_ref, kg_ref, o_ref, nk_ref, nv_ref):
    scale = HEAD_DIM ** -0.5
    nv_ref[...] = v_ref[...]
    for h in range(N_KV_HEADS):
        hs = slice(h * HEAD_DIM, (h + 1) * HEAD_DIM)
        kn = _rms(k_ref[:, hs], kg_ref[...])
        nk_ref[:, hs] = kn
        kb = kn.astype(BF16)
        vb = v_ref[:, hs].astype(BF16)
        qs = []
        for g in range(GQA_GROUP):
            c0 = (h * GQA_GROUP + g) * HEAD_DIM
            qs.append(_rms(q_ref[:, c0:c0 + HEAD_DIM], qg_ref[...]).astype(BF16))
        q4 = jnp.concatenate(qs, axis=0)
        s = lax.dot_general(q4, kb, NT_DIMS, preferred_element_type=F32) * scale
        sk = _sink_col(sink_ref, h, SEQ)
        m = jnp.maximum(jnp.max(s, axis=1, keepdims=True), sk)
        p = jnp.exp(s - m)
        den = jnp.sum(p, axis=1, keepdims=True) + jnp.exp(sk - m)
        o = _dot(p.astype(BF16), vb) / den
        for g in range(GQA_GROUP):
            c0 = (h * GQA_GROUP + g) * HEAD_DIM
            o_ref[:, c0:c0 + HEAD_DIM] = o[g * SEQ:(g + 1) * SEQ, :]


def _ctx_attn(proj, sink, qg, kg):
    m = proj.shape[0]
    kvw = N_KV_HEADS * HEAD_DIM
    return pl.pallas_call(
        _ctx_attn_kernel,
        grid=(m // SEQ,),
        in_specs=[
            pl.BlockSpec(memory_space=pltpu.SMEM),
            pl.BlockSpec((SEQ, D_MODEL), lambda b: (b, OFF_Q // D_MODEL)),
            pl.BlockSpec((SEQ, kvw), lambda b: (b, OFF_K // kvw)),
            pl.BlockSpec((SEQ, kvw), lambda b: (b, OFF_V // kvw)),
            pl.BlockSpec((1, HEAD_DIM), lambda b: (0, 0)),
            pl.BlockSpec((1, HEAD_DIM), lambda b: (0, 0)),
        ],
        out_specs=[
            pl.BlockSpec((SEQ, D_MODEL), lambda b: (b, 0)),
            pl.BlockSpec((SEQ, kvw), lambda b: (b, 0)),
            pl.BlockSpec((SEQ, kvw), lambda b: (b, 0)),
        ],
        out_shape=[
            jax.ShapeDtypeStruct((m, D_MODEL), F32),
            jax.ShapeDtypeStruct((m, kvw), F32),
            jax.ShapeDtypeStruct((m, kvw), F32),
        ],
        compiler_params=_cp(("arbitrary",), 40),
        name="ctx_attention",
    )(sink, proj, proj, proj, qg, kg)


def _rope(x, c_ref, sa_ref, sb_ref):
    up = pltpu.roll(x, HEAD_DIM - 32, axis=1)
    dn = pltpu.roll(x, 32, axis=1)
    return x * c_ref[...] + up * sa_ref[...] + dn * sb_ref[...]


def _lat_attn_kernel(sink_ref, q_ref, k_ref, v_ref, ck_ref, cv_ref, qg_ref, kg_ref,
                     c_ref, sa_ref, sb_ref, o_ref, q_scr, k_scr, v_scr):
    t = DEC_SEQ
    scale = HEAD_DIM ** -0.5
    h = pl.program_id(1)
    zpad = jnp.zeros((BLOCK, HEAD_DIM), BF16)
    k_scr[0:BLOCK, :] = zpad
    k_scr[t + BLOCK:t + 2 * BLOCK, :] = zpad
    v_scr[0:BLOCK, :] = zpad
    v_scr[t + BLOCK:t + 2 * BLOCK, :] = zpad
    kr = _rope(_rms(k_ref[...], kg_ref[...]), c_ref, sa_ref, sb_ref)
    k_scr[BLOCK:t + BLOCK, :] = kr.astype(BF16)
    v_scr[BLOCK:t + BLOCK, :] = v_ref[...].astype(BF16)
    for g in range(GQA_GROUP):
        qn = _rms(q_ref[:, g * HEAD_DIM:(g + 1) * HEAD_DIM], qg_ref[...])
        q_scr[g] = _rope(qn, c_ref, sa_ref, sb_ref).astype(BF16)
    ckb = ck_ref[...].astype(BF16)
    cvb = cv_ref[...].astype(BF16)
    sk = _sink_col(sink_ref, h, BLOCK)
    rows = GQA_GROUP * BLOCK
    qi = lax.broadcasted_iota(jnp.int32, (rows, 3 * BLOCK), 0) % BLOCK
    kj = lax.broadcasted_iota(jnp.int32, (rows, 3 * BLOCK), 1)
    band = (kj >= qi) & (kj <= qi + 2 * WINDOW)
    neg = jnp.finfo(F32).min

    def body(n, carry):
        r0 = pl.multiple_of(n * BLOCK, BLOCK)
        q4 = jnp.concatenate([q_scr[g, pl.ds(r0, BLOCK), :] for g in range(GQA_GROUP)], axis=0)
        kw = k_scr[pl.ds(r0, 3 * BLOCK), :]
        vw = v_scr[pl.ds(r0, 3 * BLOCK), :]
        kjg = kj + (n - 1) * BLOCK
        valid = band & (kjg >= 0) & (kjg < t)
        s_win = lax.dot_general(q4, kw, NT_DIMS, preferred_element_type=F32) * scale
        s_win = jnp.where(valid, s_win, neg)
        s_ctx = lax.dot_general(q4, ckb, NT_DIMS, preferred_element_type=F32) * scale
        m = jnp.maximum(jnp.max(s_win, axis=1, keepdims=True),
                        jnp.max(s_ctx, axis=1, keepdims=True))
        m = jnp.maximum(m, sk)
        p_win = jnp.exp(s_win - m)
        p_ctx = jnp.exp(s_ctx - m)
        den = (jnp.sum(p_win, axis=1, keepdims=True) + jnp.sum(p_ctx, axis=1, keepdims=True)
               + jnp.exp(sk - m))
        o = (_dot(p_win.astype(BF16), vw) + _dot(p_ctx.astype(BF16), cvb)) / den
        for g in range(GQA_GROUP):
            o_ref[pl.ds(r0, BLOCK), g * HEAD_DIM:(g + 1) * HEAD_DIM] = o[g * BLOCK:(g + 1) * BLOCK, :]
        return carry

    lax.fori_loop(0, t // BLOCK, body, 0)


def _lat_attn(proj, ck, cv, sink, qg, kg, rope_c, rope_sa, rope_sb):
    m = proj.shape[0]
    t = DEC_SEQ
    gw = GQA_GROUP * HEAD_DIM
    tab = pl.BlockSpec((t, HEAD_DIM), lambda b, h: (0, 0))
    vec = pl.BlockSpec((1, HEAD_DIM), lambda b, h: (0, 0))
    cache = pl.BlockSpec((None, PAST_LEN, HEAD_DIM), lambda b, h: (b, 0, h))
    return pl.pallas_call(
        _lat_attn_kernel,
        grid=(m // t, N_KV_HEADS),
        in_specs=[
            pl.BlockSpec(memory_space=pltpu.SMEM),
            pl.BlockSpec((t, gw), lambda b, h: (b, OFF_Q // gw + h)),
            pl.BlockSpec((t, HEAD_DIM), lambda b, h: (b, OFF_K // HEAD_DIM + h)),
            pl.BlockSpec((t, HEAD_DIM), lambda b, h: (b, OFF_V // HEAD_DIM + h)),
            cache, cache, vec, vec, tab, tab, tab,
        ],
        out_specs=pl.BlockSpec((t, gw), lambda b, h: (b, h)),
        out_shape=jax.ShapeDtypeStruct((m, D_MODEL), F32),
        scratch_shapes=[
            pltpu.VMEM((GQA_GROUP, t, HEAD_DIM), BF16),
            pltpu.VMEM((t + 2 * BLOCK, HEAD_DIM), BF16),
            pltpu.VMEM((t + 2 * BLOCK, HEAD_DIM), BF16),
        ],
        compiler_params=_cp(("arbitrary", "arbitrary"), 40),
        name="latent_attention",
    )(sink, proj, proj, proj, ck, cv, qg, kg, rope_c, rope_sa, rope_sb)


def _log_sigmoid(x):
    return jnp.minimum(x, 0.0) - jnp.log1p(jnp.exp(-jnp.abs(x)))


def _split3(x):
    hi = x.astype(BF16)
    r = x - hi.astype(F32)
    mid = r.astype(BF16)
    lo = (r - mid.astype(F32)).astype(BF16)
    return hi, mid, lo


def _gla_kernel(*refs, t, has_state):
    if has_state:
        (q_ref, k_ref, v_ref, a_ref, waf_ref, wab_ref, baf_ref, bab_ref, gn_ref,
         s0f_ref, s0b_ref, og_ref, stf, stb, laf, lab, ob) = refs
    else:
        (q_ref, k_ref, v_ref, a_ref, waf_ref, wab_ref, baf_ref, bab_ref, gn_ref,
         og_ref, sf_out, sb_out, stf, stb, laf, lab, ob) = refs
    c = GLA_CHUNK
    nc = t // c
    ab = a_ref[...].astype(BF16)
    laf[...] = _log_sigmoid(_dot(ab, waf_ref[...]) + baf_ref[...]) / GLA_TAU
    lab[...] = _log_sigmoid(_dot(ab, wab_ref[...]) + bab_ref[...]) / GLA_TAU
    if has_state:
        stf[...] = s0f_ref[...].T
        stb[...] = s0b_ref[...].T
    else:
        stf[...] = jnp.zeros(stf.shape, F32)
        stb[...] = jnp.zeros(stb.shape, F32)

    row = lax.broadcasted_iota(jnp.int32, (c, c), 0)
    col = lax.broadcasted_iota(jnp.int32, (c, c), 1)
    lower = col <= row
    upper = col >= row
    tri_lo = jnp.where(lower, 1.0, 0.0).astype(BF16)
    tri_up = jnp.where(upper, 1.0, 0.0).astype(BF16)

    def chunk(ci, fwd):
        r0 = pl.multiple_of(ci * c, c)
        la_ref, st_ref, dst = (laf, stf, og_ref) if fwd else (lab, stb, ob)
        tri, mask = (tri_lo, lower) if fwd else (tri_up, upper)
        q = q_ref[pl.ds(r0, c), :] * (GLA_HK ** -0.5)
        k = k_ref[pl.ds(r0, c), :]
        v = v_ref[pl.ds(r0, c), :].astype(BF16)
        hi, mid, lo = _split3(la_ref[pl.ds(r0, c), :])
        b = _dot(tri, hi) + _dot(tri, mid) + _dot(tri, lo)
        btot = b[c - 1:c, :] if fwd else b[0:1, :]
        qe = (q * jnp.exp(b)).astype(BF16)
        ke = (k * jnp.exp(-b)).astype(BF16)
        kl = (k * jnp.exp(btot - b)).astype(BF16)
        att = lax.dot_general(qe, ke, NT_DIMS, preferred_element_type=F32)
        att = jnp.where(mask, att, 0.0).astype(BF16)
        st = st_ref[...]
        o = _dot(att, v) + lax.dot_general(qe, st.astype(BF16), NT_DIMS, preferred_element_type=F32)
        dst[pl.ds(r0, c), :] = o
        st_ref[...] = st * jnp.exp(btot) + lax.dot_general(v, kl, TN_DIMS, preferred_element_type=F32)

    def body(i, carry):
        chunk(i, True)
        chunk(nc - 1 - i, False)
        return carry

    lax.fori_loop(0, nc, body, 0)
    og_ref[...] = _rms(og_ref[...] + ob[...], gn_ref[...])
    if not has_state:
        sf_out[...] = stf[...].T
        sb_out[...] = stb[...].T


def _gla(proj, aproj, waf, wab, baf, bab, gn, t, s0f=None, s0b=None):
    m = proj.shape[0]
    nb = m // t
    has_state = s0f is not None
    st_spec = pl.BlockSpec((None, GLA_HK, GLA_HV), lambda b, h: (b * GLA_HEADS + h, 0, 0))
    in_specs = [
        pl.BlockSpec((t, GLA_HK), lambda b, h: (b, OFF_GQ // GLA_HK + h)),
        pl.BlockSpec((t, GLA_HK), lambda b, h: (b, OFF_GK // GLA_HK + h)),
        pl.BlockSpec((t, GLA_HV), lambda b, h: (b, OFF_GV // GLA_HV + h)),
        pl.BlockSpec((t, LANES), lambda b, h: (b, 0)),
        pl.BlockSpec((LANES, GLA_HK), lambda b, h: (0, h)),
        pl.BlockSpec((LANES, GLA_HK), lambda b, h: (0, h)),
        pl.BlockSpec((1, GLA_HK), lambda b, h: (0, h)),
        pl.BlockSpec((1, GLA_HK), lambda b, h: (0, h)),
        pl.BlockSpec((1, GLA_HV), lambda b, h: (0, h)),
    ]
    args = [proj, proj, proj, aproj, waf, wab, baf, bab, gn]
    og_spec = pl.BlockSpec((t, GLA_HV), lambda b, h: (b, h))
    og_shape = jax.ShapeDtypeStruct((m, GLA_DV), F32)
    if has_state:
        in_specs += [st_spec, st_spec]
        args += [s0f, s0b]
        out_specs, out_shape = og_spec, og_shape
    else:
        st_shape = jax.ShapeDtypeStruct((nb * GLA_HEADS, GLA_HK, GLA_HV), F32)
        out_specs, out_shape = [og_spec, st_spec, st_spec], [og_shape, st_shape, st_shape]
    return pl.pallas_call(
        functools.partial(_gla_kernel, t=t, has_state=has_state),
        grid=(nb, GLA_HEADS),
        in_specs=in_specs,
        out_specs=out_specs,
        out_shape=out_shape,
        scratch_shapes=[
            pltpu.VMEM((GLA_HV, GLA_HK), F32),
            pltpu.VMEM((GLA_HV, GLA_HK), F32),
            pltpu.VMEM((t, GLA_HK), F32),
            pltpu.VMEM((t, GLA_HK), F32),
            pltpu.VMEM((t, GLA_HV), F32),
        ],
        compiler_params=_cp(("arbitrary", "arbitrary"), 56),
        name="gla_state" if has_state else "gla_zero",
    )(*args)


MERGE_TM = 256
MERGE_TN = 2048


def _merge_kernel(oa_ref, og_ref, gr_ref, ga_ref, gg_ref, o_ref):
    gr = gr_ref[...]
    o_gla = og_ref[...] * (gr * _sigmoid(gr))
    merged = _sigmoid(ga_ref[...]) * oa_ref[...] + _sigmoid(gg_ref[...]) * o_gla
    o_ref[...] = merged.astype(BF16)


def _merge(o_att, og, proj):
    m = o_att.shape[0]
    blk = lambda off: pl.BlockSpec((MERGE_TM, MERGE_TN), lambda i, j: (i, off // MERGE_TN + j))
    return pl.pallas_call(
        _merge_kernel,
        grid=(m // MERGE_TM, D_MODEL // MERGE_TN),
        in_specs=[blk(0), blk(0), blk(OFF_GR), blk(OFF_GATT), blk(OFF_GGLA)],
        out_specs=blk(0),
        out_shape=jax.ShapeDtypeStruct((m, D_MODEL), BF16),
        compiler_params=_cp(("arbitrary", "arbitrary"), 40),
        name="branch_merge",
    )(o_att, og, proj, proj, proj)


OUT_TM = 512
OUT_TN = 1024


def _out_kernel(mg_ref, w_ref, x_ref, gt_ref, g2_ref, sh_ref, sc_ref, x1_ref, h2_ref, x1_scr):
    j = pl.program_id(1)
    nj = D_MODEL // OUT_TN
    x1 = x_ref[...] + gt_ref[...] * _dot(mg_ref[...], w_ref[...])
    x1_ref[...] = x1
    x1_scr[j] = x1

    @pl.when(j == nj - 1)
    def _():
        ssq = jnp.zeros((OUT_TM, 1), F32)
        for jj in range(nj):
            xs = x1_scr[jj]
            ssq = ssq + jnp.sum(xs * xs, axis=-1, keepdims=True)
        inv = lax.rsqrt(ssq / D_MODEL + EPS)
        for jj in range(nj):
            cs = slice(jj * OUT_TN, (jj + 1) * OUT_TN)
            y = x1_scr[jj] * inv * g2_ref[:, cs]
            h2_ref[:, cs] = (y * (1.0 + sc_ref[:, cs]) + sh_ref[:, cs]).astype(BF16)


def _out_proj(merged, w_out, x, gate1, g2, shift2, scale2, mod_off, rows_per_mod):
    m = x.shape[0]
    bpb = rows_per_mod // OUT_TM
    gate_spec = pl.BlockSpec((None, 1, OUT_TN), lambda i, j: (mod_off + i // bpb, 0, j))
    return pl.pallas_call(
        _out_kernel,
        grid=(m // OUT_TM, D_MODEL // OUT_TN),
        in_specs=[
            pl.BlockSpec((OUT_TM, D_MODEL), lambda i, j: (i, 0)),
            pl.BlockSpec((D_MODEL, OUT_TN), lambda i, j: (0, j)),
            pl.BlockSpec((OUT_TM, OUT_TN), lambda i, j: (i, j)),
            gate_spec,
            pl.BlockSpec((1, D_MODEL), lambda i, j: (0, 0)),
            _mod_spec(mod_off, bpb),
            _mod_spec(mod_off, bpb),
        ],
        out_specs=[
            pl.BlockSpec((OUT_TM, OUT_TN), lambda i, j: (i, j)),
            pl.BlockSpec((OUT_TM, D_MODEL), lambda i, j: (i, 0)),
        ],
        out_shape=[
            jax.ShapeDtypeStruct((m, D_MODEL), F32),
            jax.ShapeDtypeStruct((m, D_MODEL), BF16),
        ],
        scratch_shapes=[pltpu.VMEM((D_MODEL // OUT_TN, OUT_TM, OUT_TN), F32)],
        compiler_params=_cp(("arbitrary", "arbitrary"), 56),
        name="out_proj_residual_norm",
    )(merged, w_out, x, gate1, g2, shift2, scale2)


FFI_TM = 1024
FFI_TN = 256


def _ffn_in_kernel(h_ref, wg_ref, wu_ref, o_ref):
    h = h_ref[...]
    g = _dot(h, wg_ref[...])
    u = _dot(h, wu_ref[...])
    o_ref[...] = (g * _sigmoid(g) * u).astype(BF16)


def _ffn_in(h2, wg, wu):
    m = h2.shape[0]
    return pl.pallas_call(
        _ffn_in_kernel,
        grid=(m // FFI_TM, D_FF // FFI_TN),
        in_specs=[
            pl.BlockSpec((FFI_TM, D_MODEL), lambda i, j: (i, 0)),
            pl.BlockSpec((D_MODEL, FFI_TN), lambda i, j: (0, j)),
            pl.BlockSpec((D_MODEL, FFI_TN), lambda i, j: (0, j)),
        ],
        out_specs=pl.BlockSpec((FFI_TM, FFI_TN), lambda i, j: (i, j)),
        out_shape=jax.ShapeDtypeStruct((m, D_FF), BF16),
        compiler_params=_cp(("arbitrary", "arbitrary"), 48),
        name="ffn_in_swiglu",
    )(h2, wg, wu)


FFO_TM = 512
FFO_TN = 512


def _ffn_out_kernel(a_ref, w_ref, x_ref, gt_ref, o_ref):
    o_ref[...] = x_ref[...] + gt_ref[...] * _dot(a_ref[...], w_ref[...])


def _ffn_out(act, w, x1, gate2, mod_off, rows_per_mod):
    m = x1.shape[0]
    bpb = rows_per_mod // FFO_TM
    return pl.pallas_call(
        _ffn_out_kernel,
        grid=(m // FFO_TM, D_MODEL // FFO_TN),
        in_specs=[
            pl.BlockSpec((FFO_TM, D_FF), lambda i, j: (i, 0)),
            pl.BlockSpec((D_FF, FFO_TN), lambda i, j: (0, j)),
            pl.BlockSpec((FFO_TM, FFO_TN), lambda i, j: (i, j)),
            pl.BlockSpec((None, 1, FFO_TN), lambda i, j: (mod_off + i // bpb, 0, j)),
        ],
        out_specs=pl.BlockSpec((FFO_TM, FFO_TN), lambda i, j: (i, j)),
        out_shape=jax.ShapeDtypeStruct((m, D_MODEL), F32),
        compiler_params=_cp(("arbitrary", "arbitrary"), 56),
        name="ffn_out_residual",
    )(act, w, x1, gate2)


def _rope_tables(t):
    rows = t // GRID_W
    half = HEAD_DIM // 2
    row = jnp.repeat(jnp.arange(rows, dtype=F32), GRID_W)
    col = jnp.tile(jnp.arange(GRID_W, dtype=F32), rows)
    inv = ROPE_THETA ** (-jnp.arange(0, half, 2, dtype=F32) / half)
    ar = row[:, None] * inv[None, :]
    ac = col[:, None] * inv[None, :]
    cr, sr, cc, sc = jnp.cos(ar), jnp.sin(ar), jnp.cos(ac), jnp.sin(ac)
    z = jnp.zeros_like(sr)
    tab_c = jnp.concatenate([cr, cr, cc, cc], axis=1)
    tab_sa = jnp.concatenate([-sr, z, -sc, z], axis=1)
    tab_sb = jnp.concatenate([z, sr, z, sc], axis=1)
    return tab_c, tab_sa, tab_sb


def _trunk(x, mod, mod_off, rows_per_mod, wts, ctx):
    (g1, g2, w_in, wa, qg, kg, sink, waf, wab, baf, bab, gn, w_out, wg, wu, w_fo) = wts
    shift1, scale1, gate1, shift2, scale2, gate2 = mod
    proj, aproj = _proj(x, g1, shift1, scale1, w_in, wa, mod_off, rows_per_mod)
    if ctx is None:
        o_att, new_k, new_v = _ctx_attn(proj, sink, qg, kg)
        og, s_f, s_b = _gla(proj, aproj, waf, wab, baf, bab, gn, SEQ)
        extra = (new_k, new_v, s_f, s_b)
    else:
        ck, cv, s0f, s0b, rope = ctx
        o_att = _lat_attn(proj, ck, cv, sink, qg, kg, *rope)
        og = _gla(proj, aproj, waf, wab, baf, bab, gn, DEC_SEQ, s0f, s0b)
        extra = None
    merged = _merge(o_att, og, proj)
    x1, h2 = _out_proj(merged, w_out, x, gate1, g2, shift2, scale2, mod_off, rows_per_mod)
    act = _ffn_in(h2, wg, wu)
    y = _ffn_out(act, w_fo, x1, gate2, mod_off, rows_per_mod)
    return y, extra


def kernel(x_prompt, x_sample, c, cache_k, cache_v, state_gla_fwd, state_gla_bwd, c_ctx, w_ada, b_ada, norm1_g, norm2_g, w_in, q_norm_g, k_norm_g, attn_sink, w_a2_fwd, b_a_fwd, w_a2_bwd, b_a_bwd, gla_norm_g, w_out, w_ffn_in, w_ffn_out):
    assert w_ada.shape[0] == 1, "single trunk layer"
    cc = jnp.zeros((MOD_ROWS, D_MODEL), F32).at[0].set(c_ctx).at[1:1 + DEC_BATCH].set(c)
    mod_all = _ada(cc, w_ada[0], b_ada[0][None, :])
    mod = tuple(mod_all[:, i * D_MODEL:(i + 1) * D_MODEL].reshape(MOD_ROWS, 1, D_MODEL)
                for i in range(N_MOD))

    r = GLA_GATE_RANK
    wa = jnp.zeros((D_MODEL, LANES), BF16).at[:, :2 * r].set(w_in[0][:, D_WIDE:].astype(BF16))
    waf = jnp.zeros((LANES, GLA_DK), BF16).at[:r].set(w_a2_fwd[0].astype(BF16))
    wab = jnp.zeros((LANES, GLA_DK), BF16).at[r:2 * r].set(w_a2_bwd[0].astype(BF16))
    wts = (
        norm1_g[0][None, :], norm2_g[0][None, :],
        w_in[0][:, :D_WIDE].astype(BF16), wa,
        q_norm_g[0][None, :], k_norm_g[0][None, :], attn_sink[0],
        waf, wab, b_a_fwd[0][None, :], b_a_bwd[0][None, :], gla_norm_g[0][None, :],
        w_out[0].astype(BF16),
        w_ffn_in[0][:, :D_FF].astype(BF16), w_ffn_in[0][:, D_FF:].astype(BF16),
        w_ffn_out[0].astype(BF16),
    )

    xp = x_prompt.reshape(BATCH * SEQ, D_MODEL)
    xs = x_sample.reshape(DEC_BATCH * DEC_SEQ, D_MODEL)
    yp, (new_k, new_v, s_f, s_b) = _trunk(xp, mod, 0, BATCH * SEQ, wts, None)
    kvw = N_KV_HEADS * HEAD_DIM
    ctx = (
        cache_k[:, 0].reshape(DEC_BATCH, PAST_LEN, kvw),
        cache_v[:, 0].reshape(DEC_BATCH, PAST_LEN, kvw),
        state_gla_fwd[:, 0].reshape(DEC_BATCH * GLA_HEADS, GLA_HK, GLA_HV),
        state_gla_bwd[:, 0].reshape(DEC_BATCH * GLA_HEADS, GLA_HK, GLA_HV),
        _rope_tables(DEC_SEQ),
    )
    ys, _ = _trunk(xs, mod, 1, DEC_SEQ, wts, ctx)
    return (
        yp.reshape(BATCH, SEQ, D_MODEL),
        ys.reshape(DEC_BATCH, DEC_SEQ, D_MODEL),
        new_k.reshape(BATCH, 1, SEQ, N_KV_HEADS, HEAD_DIM),
        new_v.reshape(BATCH, 1, SEQ, N_KV_HEADS, HEAD_DIM),
        s_f.reshape(BATCH, 1, GLA_HEADS, GLA_HK, GLA_HV),
        s_b.reshape(BATCH, 1, GLA_HEADS, GLA_HK, GLA_HV),
    )
```

```python
import functools

import jax
import jax.numpy as jnp
from jax import lax
from jax.experimental import pallas as pl
from jax.experimental.pallas import tpu as pltpu

F32 = jnp.float32
BF16 = jnp.bfloat16

D_MODEL = 4096
BATCH = 32
SEQ = 256
DEC_BATCH = 8
DEC_SEQ = 1024
PAST_LEN = 256
GRID_W = 64
HEAD_DIM = 128
N_Q_HEADS = 32
N_KV_HEADS = 8
GQA_GROUP = N_Q_HEADS // N_KV_HEADS
WINDOW = 128
BLOCK = 128
ROPE_THETA = 10000.0
GLA_HEADS = 4
GLA_DK = D_MODEL // 2
GLA_DV = D_MODEL
GLA_HK = GLA_DK // GLA_HEADS
GLA_HV = GLA_DV // GLA_HEADS
GLA_GATE_RANK = 16
GLA_TAU = 16.0
GLA_CHUNK = 64
GLA_SUPER = 256
D_FF = 11008
N_MOD = 6
EPS = 1e-6

OFF_Q = 0
OFF_K = 4096
OFF_V = 5120
OFF_GQ = 6144
OFF_GK = 8192
OFF_GV = 10240
OFF_GR = 14336
OFF_GATT = 18432
OFF_GGLA = 22528
D_WIDE = 26624
LANES = 128
MOD_ROWS = 16

MIB = 1024 * 1024
NT_DIMS = (((1,), (1,)), ((), ()))
TN_DIMS = (((0,), (0,)), ((), ()))


def _cp(sem, vmem_mib):
    return pltpu.CompilerParams(dimension_semantics=sem, vmem_limit_bytes=vmem_mib * MIB)


def _rms(x, g):
    ms = jnp.mean(x * x, axis=-1, keepdims=True)
    return x * lax.rsqrt(ms + EPS) * g


def _sigmoid(x):
    return 1.0 / (1.0 + jnp.exp(-x))


def _dot(a, b):
    return jnp.dot(a, b, preferred_element_type=F32)


ADA_TN = 512


def _ada_kernel(c_ref, w_ref, b_ref, o_ref):
    c = c_ref[...]
    s = (c * _sigmoid(c)).astype(BF16)
    o_ref[...] = _dot(s, w_ref[...].astype(BF16)) + b_ref[...]


def _ada(cc, w_ada, b_ada):
    n = w_ada.shape[1]
    return pl.pallas_call(
        _ada_kernel,
        grid=(n // ADA_TN,),
        in_specs=[
            pl.BlockSpec((MOD_ROWS, D_MODEL), lambda j: (0, 0)),
            pl.BlockSpec((D_MODEL, ADA_TN), lambda j: (0, j)),
            pl.BlockSpec((1, ADA_TN), lambda j: (0, j)),
        ],
        out_specs=pl.BlockSpec((MOD_ROWS, ADA_TN), lambda j: (0, j)),
        out_shape=jax.ShapeDtypeStruct((MOD_ROWS, n), F32),
        compiler_params=_cp(("arbitrary",), 40),
        name="ada_ln",
    )(cc, w_ada, b_ada)


PROJ_TM = 512
PROJ_TN = 1024


def _proj_kernel(x_ref, g_ref, sh_ref, sc_ref, w_ref, wa_ref, o_ref, a_ref, h_scr):
    @pl.when(pl.program_id(1) == 0)
    def _():
        h = _rms(x_ref[...], g_ref[...]) * (1.0 + sc_ref[...]) + sh_ref[...]
        hb = h.astype(BF16)
        h_scr[...] = hb
        a_ref[...] = _dot(hb, wa_ref[...])

    o_ref[...] = _dot(h_scr[...], w_ref[...])


def _mod_spec(mod_off, bpb):
    return pl.BlockSpec((None, 1, D_MODEL), lambda i, j: (mod_off + i // bpb, 0, 0))


def _proj(x, g1, shift, scale, w, wa, mod_off, rows_per_mod):
    m = x.shape[0]
    bpb = rows_per_mod // PROJ_TM
    return pl.pallas_call(
        _proj_kernel,
        grid=(m // PROJ_TM, D_WIDE // PROJ_TN),
        in_specs=[
            pl.BlockSpec((PROJ_TM, D_MODEL), lambda i, j: (i, 0)),
            pl.BlockSpec((1, D_MODEL), lambda i, j: (0, 0)),
            _mod_spec(mod_off, bpb),
            _mod_spec(mod_off, bpb),
            pl.BlockSpec((D_MODEL, PROJ_TN), lambda i, j: (0, j)),
            pl.BlockSpec((D_MODEL, LANES), lambda i, j: (0, 0)),
        ],
        out_specs=[
            pl.BlockSpec((PROJ_TM, PROJ_TN), lambda i, j: (i, j)),
            pl.BlockSpec((PROJ_TM, LANES), lambda i, j: (i, 0)),
        ],
        out_shape=[
            jax.ShapeDtypeStruct((m, D_WIDE), F32),
            jax.ShapeDtypeStruct((m, LANES), F32),
        ],
        scratch_shapes=[pltpu.VMEM((PROJ_TM, D_MODEL), BF16)],
        compiler_params=_cp(("arbitrary", "arbitrary"), 52),
        name="norm_mod_in_proj",
    )(x, g1, shift, scale, w, wa)


def _sink_col(sink_ref, h, rows_per_head):
    rid = lax.broadcasted_iota(jnp.int32, (GQA_GROUP * rows_per_head, 1), 0) // rows_per_head
    col = jnp.full(rid.shape, sink_ref[h * GQA_GROUP], F32)
    for g in range(1, GQA_GROUP):
        col = jnp.where(rid == g, sink_ref[h * GQA_GROUP + g], col)
    return col


def _ctx_attn_kernel(sink_ref, q_ref, k_ref, v_ref, qg_ref, kg_ref, o_ref, nk_ref, nv_ref):
    scale = HEAD_DIM ** -0.5
    nv_ref[...] = v_ref[...]
    for h in range(N_KV_HEADS):
        hs = slice(h * HEAD_DIM, (h + 1) * HEAD_DIM)
        kn = _rms(k_ref[:, hs], kg_ref[...])
        nk_ref[:, hs] = kn
        kb = kn.astype(BF16)
        vb = v_ref[:, hs].astype(BF16)
        qs = []
        for g in range(GQA_GROUP):
            c0 = (h * GQA_GROUP + g) * HEAD_DIM
            qs.append(_rms(q_ref[:, c0:c0 + HEAD_DIM], qg_ref[...]).astype(BF16))
        q4 = jnp.concatenate(qs, axis=0)
        s = lax.dot_general(q4, kb, NT_DIMS, preferred_element_type=F32) * scale
        sk = _sink_col(sink_ref, h, SEQ)
        m = jnp.maximum(jnp.max(s, axis=1, keepdims=True), sk)
        p = jnp.exp(s - m)
        den = jnp.sum(p, axis=1, keepdims=True) + jnp.exp(sk - m)
        o = _dot(p.astype(BF16), vb) / den
        for g in range(GQA_GROUP):
            c0 = (h * GQA_GROUP + g) * HEAD_DIM
            o_ref[:, c0:c0 + HEAD_DIM] = o[g * SEQ:(g + 1) * SEQ, :]


def _ctx_attn(proj, sink, qg, kg):
    m = proj.shape[0]
    kvw = N_KV_HEADS * HEAD_DIM
    return pl.pallas_call(
        _ctx_attn_kernel,
        grid=(m // SEQ,),
        in_specs=[
            pl.BlockSpec(memory_space=pltpu.SMEM),
            pl.BlockSpec((SEQ, D_MODEL), lambda b: (b, OFF_Q // D_MODEL)),
            pl.BlockSpec((SEQ, kvw), lambda b: (b, OFF_K // kvw)),
            pl.BlockSpec((SEQ, kvw), lambda b: (b, OFF_V // kvw)),
            pl.BlockSpec((1, HEAD_DIM), lambda b: (0, 0)),
            pl.BlockSpec((1, HEAD_DIM), lambda b: (0, 0)),
        ],
        out_specs=[
            pl.BlockSpec((SEQ, D_MODEL), lambda b: (b, 0)),
            pl.BlockSpec((SEQ, kvw), lambda b: (b, 0)),
            pl.BlockSpec((SEQ, kvw), lambda b: (b, 0)),
        ],
        out_shape=[
            jax.ShapeDtypeStruct((m, D_MODEL), F32),
            jax.ShapeDtypeStruct((m, kvw), F32),
            jax.ShapeDtypeStruct((m, kvw), F32),
        ],
        compiler_params=_cp(("arbitrary",), 40),
        name="ctx_attention",
    )(sink, proj, proj, proj, qg, kg)


def _rope(x, c_ref, sa_ref, sb_ref):
    up = pltpu.roll(x, HEAD_DIM - 32, axis=1)
    dn = pltpu.roll(x, 32, axis=1)
    return x * c_ref[...] + up * sa_ref[...] + dn * sb_ref[...]


def _lat_attn_kernel(sink_ref, q_ref, k_ref, v_ref, ck_ref, cv_ref, qg_ref, kg_ref,
                     c_ref, sa_ref, sb_ref, o_ref, q_scr, k_scr, v_scr):
    t = DEC_SEQ
    scale = HEAD_DIM ** -0.5
    h = pl.program_id(1)
    zpad = jnp.zeros((BLOCK, HEAD_DIM), BF16)
    k_scr[0:BLOCK, :] = zpad
    k_scr[t + BLOCK:t + 2 * BLOCK, :] = zpad
    v_scr[0:BLOCK, :] = zpad
    v_scr[t + BLOCK:t + 2 * BLOCK, :] = zpad
    kr = _rope(_rms(k_ref[...], kg_ref[...]), c_ref, sa_ref, sb_ref)
    k_scr[BLOCK:t + BLOCK, :] = kr.astype(BF16)
    v_scr[BLOCK:t + BLOCK, :] = v_ref[...].astype(BF16)
    for g in range(GQA_GROUP):
        qn = _rms(q_ref[:, g * HEAD_DIM:(g + 1) * HEAD_DIM], qg_ref[...])
        q_scr[g] = _rope(qn, c_ref, sa_ref, sb_ref).astype(BF16)
    ckb = ck_ref[...].astype(BF16)
    cvb = cv_ref[...].astype(BF16)
    sk = _sink_col(sink_ref, h, BLOCK)
    rows = GQA_GROUP * BLOCK
    qi = lax.broadcasted_iota(jnp.int32, (rows, 3 * BLOCK), 0) % BLOCK
    kj = lax.broadcasted_iota(jnp.int32, (rows, 3 * BLOCK), 1)
    band = (kj >= qi) & (kj <= qi + 2 * WINDOW)
    neg = jnp.finfo(F32).min

    def body(n, carry):
        r0 = pl.multiple_of(n * BLOCK, BLOCK)
        q4 = jnp.concatenate([q_scr[g, pl.ds(r0, BLOCK), :] for g in range(GQA_GROUP)], axis=0)
        kw = k_scr[pl.ds(r0, 3 * BLOCK), :]
        vw = v_scr[pl.ds(r0, 3 * BLOCK), :]
        kjg = kj + (n - 1) * BLOCK
        valid = band & (kjg >= 0) & (kjg < t)
        s_win = lax.dot_general(q4, kw, NT_DIMS, preferred_element_type=F32) * scale
        s_win = jnp.where(valid, s_win, neg)
        s_ctx = lax.dot_general(q4, ckb, NT_DIMS, preferred_element_type=F32) * scale
        m = jnp.maximum(jnp.max(s_win, axis=1, keepdims=True),
                        jnp.max(s_ctx, axis=1, keepdims=True))
        m = jnp.maximum(m, sk)
        p_win = jnp.exp(s_win - m)
        p_ctx = jnp.exp(s_ctx - m)
        den = (jnp.sum(p_win, axis=1, keepdims=True) + jnp.sum(p_ctx, axis=1, keepdims=True)
               + jnp.exp(sk - m))
        o = (_dot(p_win.astype(BF16), vw) + _dot(p_ctx.astype(BF16), cvb)) / den
        for g in range(GQA_GROUP):
            o_ref[pl.ds(r0, BLOCK), g * HEAD_DIM:(g + 1) * HEAD_DIM] = o[g * BLOCK:(g + 1) * BLOCK, :]
        return carry

    lax.fori_loop(0, t // BLOCK, body, 0, unroll=2)


def _lat_attn(proj, ck, cv, sink, qg, kg, rope_c, rope_sa, rope_sb):
    m = proj.shape[0]
    t = DEC_SEQ
    gw = GQA_GROUP * HEAD_DIM
    tab = pl.BlockSpec((t, HEAD_DIM), lambda b, h: (0, 0))
    vec = pl.BlockSpec((1, HEAD_DIM), lambda b, h: (0, 0))
    cache = pl.BlockSpec((None, PAST_LEN, HEAD_DIM), lambda b, h: (b, 0, h))
    return pl.pallas_call(
        _lat_attn_kernel,
        grid=(m // t, N_KV_HEADS),
        in_specs=[
            pl.BlockSpec(memory_space=pltpu.SMEM),
            pl.BlockSpec((t, gw), lambda b, h: (b, OFF_Q // gw + h)),
            pl.BlockSpec((t, HEAD_DIM), lambda b, h: (b, OFF_K // HEAD_DIM + h)),
            pl.BlockSpec((t, HEAD_DIM), lambda b, h: (b, OFF_V // HEAD_DIM + h)),
            cache, cache, vec, vec, tab, tab, tab,
        ],
        out_specs=pl.BlockSpec((t, gw), lambda b, h: (b, h)),
        out_shape=jax.ShapeDtypeStruct((m, D_MODEL), F32),
        scratch_shapes=[
            pltpu.VMEM((GQA_GROUP, t, HEAD_DIM), BF16),
            pltpu.VMEM((t + 2 * BLOCK, HEAD_DIM), BF16),
            pltpu.VMEM((t + 2 * BLOCK, HEAD_DIM), BF16),
        ],
        compiler_params=_cp(("arbitrary", "arbitrary"), 40),
        name="latent_attention",
    )(sink, proj, proj, proj, ck, cv, qg, kg, rope_c, rope_sa, rope_sb)


def _log_sigmoid(x):
    return jnp.minimum(x, 0.0) - jnp.log(1.0 + jnp.exp(-jnp.abs(x)))


def _split3(x):
    hi = x.astype(BF16)
    r = x - hi.astype(F32)
    mid = r.astype(BF16)
    lo = (r - mid.astype(F32)).astype(BF16)
    return hi, mid, lo


def _gla_kernel(*refs, t, has_state):
    if has_state:
        (q_ref, k_ref, v_ref, a_ref, waf_ref, wab_ref, baf_ref, bab_ref, gn_ref,
         s0f_ref, s0b_ref, og_ref, *scr) = refs
    else:
        (q_ref, k_ref, v_ref, a_ref, waf_ref, wab_ref, baf_ref, bab_ref, gn_ref,
         og_ref, sf_out, sb_out, *scr) = refs
    stf, stb, vb, ob = scr[:4]
    qsf, kuf, decf, *tmpf = scr[4:13]
    qsb, kub, decb, *tmpb = scr[13:]
    c = GLA_CHUNK
    sc = GLA_SUPER
    nsub = sc // c
    nsc = t // sc
    vb[...] = v_ref[...].astype(BF16)

    row = lax.broadcasted_iota(jnp.int32, (sc, sc), 0)
    col = lax.broadcasted_iota(jnp.int32, (sc, sc), 1)
    rc, cc = row // c, col // c

    def intra(si, fwd):
        r0 = pl.multiple_of(si * sc, sc)
        w_ref, b_ref, qs_scr, ku_scr, dec_scr, tmp, dst = (
            (waf_ref, baf_ref, qsf, kuf, decf, tmpf, og_ref) if fwd
            else (wab_ref, bab_ref, qsb, kub, decb, tmpb, ob))
        qd, kd, ke, q2, q3, amat = tmp
        tri = jnp.where((col <= row) if fwd else (col >= row), 1.0, 0.0).astype(BF16)
        dist = (rc - cc) if fwd else (cc - rc)
        x = _dot(a_ref[pl.ds(r0, sc), :].astype(BF16), w_ref[...]) + b_ref[...]
        hi, mid, lo = _split3(_log_sigmoid(x) / GLA_TAU)
        cum = _dot(tri, hi) + _dot(tri, mid) + _dot(tri, lo)
        zero = jnp.zeros((1, GLA_HK), F32)

        def at_start(j):
            if fwd:
                return cum[j * c - 1:j * c, :] if j > 0 else zero
            return cum[(j + 1) * c:(j + 1) * c + 1, :] if j < nsub - 1 else zero

        def at_end(j):
            return cum[(j + 1) * c - 1:(j + 1) * c, :] if fwd else cum[j * c:j * c + 1, :]

        total = at_end(nsub - 1) if fwd else at_end(0)
        dec_scr[pl.ds(pl.multiple_of(si * 8, 8), 8), :] = jnp.broadcast_to(jnp.exp(total), (8, GLA_HK))
        for j in range(nsub):
            rs = slice(j * c, (j + 1) * c)
            rows = pl.ds(pl.multiple_of(r0 + j * c, c), c)
            cj = cum[rs, :]
            cs, ce = at_start(j), at_end(j)
            q = q_ref[rows, :] * (GLA_HK ** -0.5)
            k = k_ref[rows, :]
            qdj = q * jnp.exp(cj - cs)
            kej = k * jnp.exp(ce - cj)
            qd[rs, :] = qdj.astype(BF16)
            kd[rs, :] = (k * jnp.exp(cs - cj)).astype(BF16)
            ke[rs, :] = kej.astype(BF16)
            qs_scr[rows, :] = (qdj * jnp.exp(cs)).astype(BF16)
            ku_scr[rows, :] = (kej * jnp.exp(total - ce)).astype(BF16)
            p2 = j - 2 if fwd else j + 2
            if 0 <= p2 < nsub:
                l2 = (j - 2) if fwd else j
                q2[l2 * c:(l2 + 1) * c, :] = (qdj * jnp.exp(cs - at_end(p2))).astype(BF16)
            p3 = j - 3 if fwd else j + 3
            if 0 <= p3 < nsub:
                q3[...] = (qdj * jnp.exp(cs - at_end(p3))).astype(BF16)
        nt = lambda a, b: lax.dot_general(a, b, NT_DIMS, preferred_element_type=F32)
        tril = (col <= row) if fwd else (col >= row)
        amat[...] = (jnp.where((dist == 0) & tril, nt(qd[...], kd[...]), 0.0)
                     + jnp.where(dist == 1, nt(qd[...], ke[...]), 0.0))
        r2 = slice(2 * c, sc) if fwd else slice(0, 2 * c)
        amat[r2, :] += jnp.where(dist[r2, :] == 2, nt(q2[...], ke[...]), 0.0)
        r3 = slice(3 * c, sc) if fwd else slice(0, c)
        amat[r3, :] += jnp.where(dist[r3, :] == 3, nt(q3[...], ke[...]), 0.0)
        dst[pl.ds(r0, sc), :] = _dot(amat[...].astype(BF16), vb[pl.ds(r0, sc), :])

    def intra_body(i, carry):
        intra(i, True)
        intra(nsc - 1 - i, False)
        return carry

    lax.fori_loop(0, nsc, intra_body, 0)

    if has_state:
        for i in range(nsc):
            for fwd in (True, False):
                si = i if fwd else nsc - 1 - i
                rows = slice(si * sc, (si + 1) * sc)
                st_ref, s0_ref, qs_scr, ku_scr, dec_scr, dst = (
                    (stf, s0f_ref, qsf, kuf, decf, og_ref) if fwd
                    else (stb, s0b_ref, qsb, kub, decb, ob))
                st = s0_ref[...] if i == 0 else st_ref[...]
                dst[rows, :] += _dot(qs_scr[rows, :], st.astype(BF16))
                if i < nsc - 1:
                    dec = jnp.broadcast_to(dec_scr[si * 8:si * 8 + 1, :], (LANES, GLA_HK)).T
                    dec = jnp.concatenate([dec] * (GLA_HV // LANES), axis=1)
                    st_ref[...] = st * dec + lax.dot_general(
                        ku_scr[rows, :], vb[rows, :], TN_DIMS, preferred_element_type=F32)
    else:
        sf_out[...] = lax.dot_general(kuf[...], vb[...], TN_DIMS, preferred_element_type=F32)
        sb_out[...] = lax.dot_general(kub[...], vb[...], TN_DIMS, preferred_element_type=F32)
    og_ref[...] = _rms(og_ref[...] + ob[...], gn_ref[...])


def _gla(proj, aproj, waf, wab, baf, bab, gn, t, s0f=None, s0b=None):
    m = proj.shape[0]
    nb = m // t
    has_state = s0f is not None
    assert t % GLA_SUPER == 0 and (has_state or t == GLA_SUPER)
    sc, c = GLA_SUPER, GLA_CHUNK
    st_shape_vmem = (GLA_HK, GLA_HV) if has_state else (8, LANES)
    per_direction = [
        pltpu.VMEM((t, GLA_HK), BF16),
        pltpu.VMEM((t, GLA_HK), BF16),
        pltpu.VMEM((8 * (t // sc), GLA_HK), F32),
        pltpu.VMEM((sc, GLA_HK), BF16),
        pltpu.VMEM((sc, GLA_HK), BF16),
        pltpu.VMEM((sc, GLA_HK), BF16),
        pltpu.VMEM((sc - 2 * c, GLA_HK), BF16),
        pltpu.VMEM((sc - 3 * c, GLA_HK), BF16),
        pltpu.VMEM((sc, sc), F32),
    ]
    st_spec = pl.BlockSpec((None, GLA_HK, GLA_HV), lambda b, h: (b * GLA_HEADS + h, 0, 0))
    in_specs = [
        pl.BlockSpec((t, GLA_HK), lambda b, h: (b, OFF_GQ // GLA_HK + h)),
        pl.BlockSpec((t, GLA_HK), lambda b, h: (b, OFF_GK // GLA_HK + h)),
        pl.BlockSpec((t, GLA_HV), lambda b, h: (b, OFF_GV // GLA_HV + h)),
        pl.BlockSpec((t, LANES), lambda b, h: (b, 0)),
        pl.BlockSpec((LANES, GLA_HK), lambda b, h: (0, h)),
        pl.BlockSpec((LANES, GLA_HK), lambda b, h: (0, h)),
        pl.BlockSpec((1, GLA_HK), lambda b, h: (0, h)),
        pl.BlockSpec((1, GLA_HK), lambda b, h: (0, h)),
        pl.BlockSpec((1, GLA_HV), lambda b, h: (0, h)),
    ]
    args = [proj, proj, proj, aproj, waf, wab, baf, bab, gn]
    og_spec = pl.BlockSpec((t, GLA_HV), lambda b, h: (b, h))
    og_shape = jax.ShapeDtypeStruct((m, GLA_DV), F32)
    if has_state:
        in_specs += [st_spec, st_spec]
        args += [s0f, s0b]
        out_specs, out_shape = og_spec, og_shape
    else:
        st_shape = jax.ShapeDtypeStruct((nb * GLA_HEADS, GLA_HK, GLA_HV), F32)
        out_specs, out_shape = [og_spec, st_spec, st_spec], [og_shape, st_shape, st_shape]
    return pl.pallas_call(
        functools.partial(_gla_kernel, t=t, has_state=has_state),
        grid=(nb, GLA_HEADS),
        in_specs=in_specs,
        out_specs=out_specs,
        out_shape=out_shape,
        scratch_shapes=[
            pltpu.VMEM(st_shape_vmem, F32),
            pltpu.VMEM(st_shape_vmem, F32),
            pltpu.VMEM((t, GLA_HV), BF16),
            pltpu.VMEM((t, GLA_HV), F32),
        ] + 2 * per_direction,
        compiler_params=_cp(("arbitrary", "arbitrary"), 56),
        name="gla_state" if has_state else "gla_zero",
    )(*args)


MERGE_TM = 256
MERGE_TN = 2048


def _merge_kernel(oa_ref, og_ref, gr_ref, ga_ref, gg_ref, o_ref):
    gr = gr_ref[...]
    o_gla = og_ref[...] * (gr * _sigmoid(gr))
    merged = _sigmoid(ga_ref[...]) * oa_ref[...] + _sigmoid(gg_ref[...]) * o_gla
    o_ref[...] = merged.astype(BF16)


def _merge(o_att, og, proj):
    m = o_att.shape[0]
    blk = lambda off: pl.BlockSpec((MERGE_TM, MERGE_TN), lambda i, j: (i, off // MERGE_TN + j))
    return pl.pallas_call(
        _merge_kernel,
        grid=(m // MERGE_TM, D_MODEL // MERGE_TN),
        in_specs=[blk(0), blk(0), blk(OFF_GR), blk(OFF_GATT), blk(OFF_GGLA)],
        out_specs=blk(0),
        out_shape=jax.ShapeDtypeStruct((m, D_MODEL), BF16),
        compiler_params=_cp(("arbitrary", "arbitrary"), 40),
        name="branch_merge",
    )(o_att, og, proj, proj, proj)


OUT_TM = 512
OUT_TN = 1024


def _out_kernel(mg_ref, w_ref, x_ref, gt_ref, g2_ref, sh_ref, sc_ref, x1_ref, h2_ref, x1_scr):
    j = pl.program_id(1)
    nj = D_MODEL // OUT_TN
    x1 = x_ref[...] + gt_ref[...] * _dot(mg_ref[...], w_ref[...])
    x1_ref[...] = x1
    x1_scr[j] = x1

    @pl.when(j == nj - 1)
    def _():
        ssq = jnp.zeros((OUT_TM, 1), F32)
        for jj in range(nj):
            xs = x1_scr[jj]
            ssq = ssq + jnp.sum(xs * xs, axis=-1, keepdims=True)
        inv = lax.rsqrt(ssq / D_MODEL + EPS)
        for jj in range(nj):
            cs = slice(jj * OUT_TN, (jj + 1) * OUT_TN)
            y = x1_scr[jj] * inv * g2_ref[:, cs]
            h2_ref[:, cs] = (y * (1.0 + sc_ref[:, cs]) + sh_ref[:, cs]).astype(BF16)


def _out_proj(merged, w_out, x, gate1, g2, shift2, scale2, mod_off, rows_per_mod):
    m = x.shape[0]
    bpb = rows_per_mod // OUT_TM
    gate_spec = pl.BlockSpec((None, 1, OUT_TN), lambda i, j: (mod_off + i // bpb, 0, j))
    return pl.pallas_call(
        _out_kernel,
        grid=(m // OUT_TM, D_MODEL // OUT_TN),
        in_specs=[
            pl.BlockSpec((OUT_TM, D_MODEL), lambda i, j: (i, 0)),
            pl.BlockSpec((D_MODEL, OUT_TN), lambda i, j: (0, j)),
            pl.BlockSpec((OUT_TM, OUT_TN), lambda i, j: (i, j)),
            gate_spec,
            pl.BlockSpec((1, D_MODEL), lambda i, j: (0, 0)),
            _mod_spec(mod_off, bpb),
            _mod_spec(mod_off, bpb),
        ],
        out_specs=[
            pl.BlockSpec((OUT_TM, OUT_TN), lambda i, j: (i, j)),
            pl.BlockSpec((OUT_TM, D_MODEL), lambda i, j: (i, 0)),
        ],
        out_shape=[
            jax.ShapeDtypeStruct((m, D_MODEL), F32),
            jax.ShapeDtypeStruct((m, D_MODEL), BF16),
        ],
        scratch_shapes=[pltpu.VMEM((D_MODEL // OUT_TN, OUT_TM, OUT_TN), F32)],
        compiler_params=_cp(("arbitrary", "arbitrary"), 56),
        name="out_proj_residual_norm",
    )(merged, w_out, x, gate1, g2, shift2, scale2)


FFI_TM = 1024
FFI_TN = 256


def _ffn_in_kernel(h_ref, wg_ref, wu_ref, o_ref):
    h = h_ref[...]
    g = _dot(h, wg_ref[...])
    u = _dot(h, wu_ref[...])
    o_ref[...] = (g * _sigmoid(g) * u).astype(BF16)


def _ffn_in(h2, w):
    m = h2.shape[0]
    return pl.pallas_call(
        _ffn_in_kernel,
        grid=(m // FFI_TM, D_FF // FFI_TN),
        in_specs=[
            pl.BlockSpec((FFI_TM, D_MODEL), lambda i, j: (i, 0)),
            pl.BlockSpec((D_MODEL, FFI_TN), lambda i, j: (0, j)),
            pl.BlockSpec((D_MODEL, FFI_TN), lambda i, j: (0, D_FF // FFI_TN + j)),
        ],
        out_specs=pl.BlockSpec((FFI_TM, FFI_TN), lambda i, j: (i, j)),
        out_shape=jax.ShapeDtypeStruct((m, D_FF), BF16),
        compiler_params=_cp(("arbitrary", "arbitrary"), 48),
        name="ffn_in_swiglu",
    )(h2, w, w)


FFO_TM = 512
FFO_TN = 512


def _ffn_out_kernel(a_ref, w_ref, x_ref, gt_ref, o_ref):
    o_ref[...] = x_ref[...] + gt_ref[...] * _dot(a_ref[...], w_ref[...])


def _ffn_out(act, w, x1, gate2, mod_off, rows_per_mod):
    m = x1.shape[0]
    bpb = rows_per_mod // FFO_TM
    return pl.pallas_call(
        _ffn_out_kernel,
        grid=(m // FFO_TM, D_MODEL // FFO_TN),
        in_specs=[
            pl.BlockSpec((FFO_TM, D_FF), lambda i, j: (i, 0)),
            pl.BlockSpec((D_FF, FFO_TN), lambda i, j: (0, j)),
            pl.BlockSpec((FFO_TM, FFO_TN), lambda i, j: (i, j)),
            pl.BlockSpec((None, 1, FFO_TN), lambda i, j: (mod_off + i // bpb, 0, j)),
        ],
        out_specs=pl.BlockSpec((FFO_TM, FFO_TN), lambda i, j: (i, j)),
        out_shape=jax.ShapeDtypeStruct((m, D_MODEL), F32),
        compiler_params=_cp(("arbitrary", "arbitrary"), 56),
        name="ffn_out_residual",
    )(act, w, x1, gate2)


def _rope_tables(t):
    rows = t // GRID_W
    half = HEAD_DIM // 2
    row = jnp.repeat(jnp.arange(rows, dtype=F32), GRID_W)
    col = jnp.tile(jnp.arange(GRID_W, dtype=F32), rows)
    inv = ROPE_THETA ** (-jnp.arange(0, half, 2, dtype=F32) / half)
    ar = row[:, None] * inv[None, :]
    ac = col[:, None] * inv[None, :]
    cr, sr, cc, sc = jnp.cos(ar), jnp.sin(ar), jnp.cos(ac), jnp.sin(ac)
    z = jnp.zeros_like(sr)
    tab_c = jnp.concatenate([cr, cr, cc, cc], axis=1)
    tab_sa = jnp.concatenate([-sr, z, -sc, z], axis=1)
    tab_sb = jnp.concatenate([z, sr, z, sc], axis=1)
    return tab_c, tab_sa, tab_sb


def _trunk(x, mod, mod_off, rows_per_mod, wts, ctx):
    (g1, g2, w_in, wa, qg, kg, sink, waf, wab, baf, bab, gn, w_out, w_fi, w_fo) = wts
    shift1, scale1, gate1, shift2, scale2, gate2 = mod
    proj, aproj = _proj(x, g1, shift1, scale1, w_in, wa, mod_off, rows_per_mod)
    if ctx is None:
        o_att, new_k, new_v = _ctx_attn(proj, sink, qg, kg)
        og, s_f, s_b = _gla(proj, aproj, waf, wab, baf, bab, gn, SEQ)
        extra = (new_k, new_v, s_f, s_b)
    else:
        ck, cv, s0f, s0b, rope = ctx
        o_att = _lat_attn(proj, ck, cv, sink, qg, kg, *rope)
        og = _gla(proj, aproj, waf, wab, baf, bab, gn, DEC_SEQ, s0f, s0b)
        extra = None
    merged = _merge(o_att, og, proj)
    x1, h2 = _out_proj(merged, w_out, x, gate1, g2, shift2, scale2, mod_off, rows_per_mod)
    act = _ffn_in(h2, w_fi)
    y = _ffn_out(act, w_fo, x1, gate2, mod_off, rows_per_mod)
    return y, extra


def kernel(x_prompt, x_sample, c, cache_k, cache_v, state_gla_fwd, state_gla_bwd, c_ctx, w_ada, b_ada, norm1_g, norm2_g, w_in, q_norm_g, k_norm_g, attn_sink, w_a2_fwd, b_a_fwd, w_a2_bwd, b_a_bwd, gla_norm_g, w_out, w_ffn_in, w_ffn_out):
    assert w_ada.shape[0] == 1, "single trunk layer"
    cc = jnp.zeros((MOD_ROWS, D_MODEL), F32).at[0].set(c_ctx).at[1:1 + DEC_BATCH].set(c)
    mod_all = _ada(cc, w_ada[0], b_ada[0][None, :])
    mod = tuple(mod_all[:, i * D_MODEL:(i + 1) * D_MODEL].reshape(MOD_ROWS, 1, D_MODEL)
                for i in range(N_MOD))

    r = GLA_GATE_RANK
    wa = jnp.zeros((D_MODEL, LANES), BF16).at[:, :2 * r].set(w_in[0][:, D_WIDE:].astype(BF16))
    waf = jnp.zeros((LANES, GLA_DK), BF16).at[:r].set(w_a2_fwd[0].astype(BF16))
    wab = jnp.zeros((LANES, GLA_DK), BF16).at[r:2 * r].set(w_a2_bwd[0].astype(BF16))
    wts = (
        norm1_g[0][None, :], norm2_g[0][None, :],
        w_in[0].astype(BF16), wa,
        q_norm_g[0][None, :], k_norm_g[0][None, :], attn_sink[0],
        waf, wab, b_a_fwd[0][None, :], b_a_bwd[0][None, :], gla_norm_g[0][None, :],
        w_out[0].astype(BF16),
        w_ffn_in[0].astype(BF16),
        w_ffn_out[0].astype(BF16),
    )

    xp = x_prompt.reshape(BATCH * SEQ, D_MODEL)
    xs = x_sample.reshape(DEC_BATCH * DEC_SEQ, D_MODEL)
    yp, (new_k, new_v, s_f, s_b) = _trunk(xp, mod, 0, BATCH * SEQ, wts, None)
    kvw = N_KV_HEADS * HEAD_DIM
    ctx = (
        cache_k[:, 0].reshape(DEC_BATCH, PAST_LEN, kvw),
        cache_v[:, 0].reshape(DEC_BATCH, PAST_LEN, kvw),
        state_gla_fwd[:, 0].reshape(DEC_BATCH * GLA_HEADS, GLA_HK, GLA_HV),
        state_gla_bwd[:, 0].reshape(DEC_BATCH * GLA_HEADS, GLA_HK, GLA_HV),
        _rope_tables(DEC_SEQ),
    )
    ys, _ = _trunk(xs, mod, 1, DEC_SEQ, wts, ctx)
    return (
        yp.reshape(BATCH, SEQ, D_MODEL),
        ys.reshape(DEC_BATCH, DEC_SEQ, D_MODEL),
        new_k.reshape(BATCH, 1, SEQ, N_KV_HEADS, HEAD_DIM),
        new_v.reshape(BATCH, 1, SEQ, N_KV_HEADS, HEAD_DIM),
        s_f.reshape(BATCH, 1, GLA_HEADS, GLA_HK, GLA_HV),
        s_b.reshape(BATCH, 1, GLA_HEADS, GLA_HK, GLA_HV),
    )
```

```python
import functools

import jax
import jax.numpy as jnp
from jax import lax
from jax.experimental import pallas as pl
from jax.experimental.pallas import tpu as pltpu

F32 = jnp.float32
BF16 = jnp.bfloat16

D_MODEL = 4096
BATCH = 32
SEQ = 256
DEC_BATCH = 8
DEC_SEQ = 1024
PAST_LEN = 256
GRID_W = 64
HEAD_DIM = 128
N_Q_HEADS = 32
N_KV_HEADS = 8
GQA_GROUP = N_Q_HEADS // N_KV_HEADS
WINDOW = 128
BLOCK = 128
ROPE_THETA = 10000.0
GLA_HEADS = 4
GLA_DK = D_MODEL // 2
GLA_DV = D_MODEL
GLA_HK = GLA_DK // GLA_HEADS
GLA_HV = GLA_DV // GLA_HEADS
GLA_GATE_RANK = 16
GLA_TAU = 16.0
GLA_CHUNK = 64
GLA_SUPER = 256
D_FF = 11008
N_MOD = 6
EPS = 1e-6

OFF_Q = 0
OFF_K = 4096
OFF_V = 5120
OFF_GQ = 6144
OFF_GK = 8192
OFF_GV = 10240
OFF_GR = 14336
OFF_GATT = 18432
OFF_GGLA = 22528
D_WIDE = 26624
LANES = 128
MOD_ROWS = 16

MIB = 1024 * 1024
NT_DIMS = (((1,), (1,)), ((), ()))
TN_DIMS = (((0,), (0,)), ((), ()))


def _cp(sem, vmem_mib):
    return pltpu.CompilerParams(dimension_semantics=sem, vmem_limit_bytes=vmem_mib * MIB)


def _rms(x, g):
    ms = jnp.mean(x * x, axis=-1, keepdims=True)
    return x * lax.rsqrt(ms + EPS) * g


def _sigmoid(x):
    return 1.0 / (1.0 + jnp.exp(-x))


def _dot(a, b):
    return jnp.dot(a, b, preferred_element_type=F32)


ADA_TN = 512


def _ada_kernel(c_ref, w_ref, b_ref, o_ref):
    c = c_ref[...]
    s = (c * _sigmoid(c)).astype(BF16)
    o_ref[...] = _dot(s, w_ref[...].astype(BF16)) + b_ref[...]


def _ada(cc, w_ada, b_ada):
    n = w_ada.shape[1]
    return pl.pallas_call(
        _ada_kernel,
        grid=(n // ADA_TN,),
        in_specs=[
            pl.BlockSpec((MOD_ROWS, D_MODEL), lambda j: (0, 0)),
            pl.BlockSpec((D_MODEL, ADA_TN), lambda j: (0, j)),
            pl.BlockSpec((1, ADA_TN), lambda j: (0, j)),
        ],
        out_specs=pl.BlockSpec((MOD_ROWS, ADA_TN), lambda j: (0, j)),
        out_shape=jax.ShapeDtypeStruct((MOD_ROWS, n), F32),
        compiler_params=_cp(("arbitrary",), 40),
        name="ada_ln",
    )(cc, w_ada, b_ada)


NORM_TM = 512
PROJ_TM = 1024
PROJ_TN = 512


def _mod_spec(mod_off, bpb):
    return pl.BlockSpec((None, 1, D_MODEL), lambda i, *_: (mod_off + i // bpb, 0, 0))


def _norm_kernel(x_ref, g_ref, sh_ref, sc_ref, wa_ref, h_ref, a_ref):
    h = _rms(x_ref[...], g_ref[...]) * (1.0 + sc_ref[...]) + sh_ref[...]
    hb = h.astype(BF16)
    h_ref[...] = hb
    a_ref[...] = _dot(hb, wa_ref[...])


def _norm_mod(x, g1, shift, scale, wa, mod_off, rows_per_mod):
    m = x.shape[0]
    bpb = rows_per_mod // NORM_TM
    return pl.pallas_call(
        _norm_kernel,
        grid=(m // NORM_TM,),
        in_specs=[
            pl.BlockSpec((NORM_TM, D_MODEL), lambda i: (i, 0)),
            pl.BlockSpec((1, D_MODEL), lambda i: (0, 0)),
            _mod_spec(mod_off, bpb),
            _mod_spec(mod_off, bpb),
            pl.BlockSpec((D_MODEL, LANES), lambda i: (0, 0)),
        ],
        out_specs=[
            pl.BlockSpec((NORM_TM, D_MODEL), lambda i: (i, 0)),
            pl.BlockSpec((NORM_TM, LANES), lambda i: (i, 0)),
        ],
        out_shape=[
            jax.ShapeDtypeStruct((m, D_MODEL), BF16),
            jax.ShapeDtypeStruct((m, LANES), F32),
        ],
        compiler_params=_cp(("arbitrary",), 40),
        name="norm_modulate",
    )(x, g1, shift, scale, wa)


def _proj_kernel(h_ref, wt_ref, o_ref):
    o_ref[...] = lax.dot_general(h_ref[...], wt_ref[...].astype(BF16), NT_DIMS,
                                 preferred_element_type=F32).astype(o_ref.dtype)


def _proj(h, wt, col0, ncols, out_dtype):
    m = h.shape[0]
    j0 = col0 // PROJ_TN
    return pl.pallas_call(
        _proj_kernel,
        grid=(m // PROJ_TM, ncols // PROJ_TN),
        in_specs=[
            pl.BlockSpec((PROJ_TM, D_MODEL), lambda i, j: (i, 0)),
            pl.BlockSpec((PROJ_TN, D_MODEL), lambda i, j: (j0 + j, 0)),
        ],
        out_specs=pl.BlockSpec((PROJ_TM, PROJ_TN), lambda i, j: (i, j)),
        out_shape=jax.ShapeDtypeStruct((m, ncols), out_dtype),
        compiler_params=_cp(("arbitrary", "arbitrary"), 48),
        name="in_proj",
    )(h, wt)


def _sink_col(sink_ref, h, rows_per_head):
    rid = lax.broadcasted_iota(jnp.int32, (GQA_GROUP * rows_per_head, 1), 0) // rows_per_head
    col = jnp.full(rid.shape, sink_ref[h * GQA_GROUP], F32)
    for g in range(1, GQA_GROUP):
        col = jnp.where(rid == g, sink_ref[h * GQA_GROUP + g], col)
    return col


def _ctx_attn_kernel(sink_ref, q_ref, k_ref, v_ref, qg_ref, kg_ref, o_ref, nk_ref, nv_ref):
    scale = HEAD_DIM ** -0.5
    nv_ref[...] = v_ref[...]
    for h in range(N_KV_HEADS):
        hs = slice(h * HEAD_DIM, (h + 1) * HEAD_DIM)
        kn = _rms(k_ref[:, hs], kg_ref[...])
        nk_ref[:, hs] = kn
        kb = kn.astype(BF16)
        vb = v_ref[:, hs].astype(BF16)
        qs = []
        for g in range(GQA_GROUP):
            c0 = (h * GQA_GROUP + g) * HEAD_DIM
            qs.append(_rms(q_ref[:, c0:c0 + HEAD_DIM], qg_ref[...]).astype(BF16))
        q4 = jnp.concatenate(qs, axis=0)
        s = lax.dot_general(q4, kb, NT_DIMS, preferred_element_type=F32) * scale
        sk = _sink_col(sink_ref, h, SEQ)
        m = jnp.maximum(jnp.max(s, axis=1, keepdims=True), sk)
        p = jnp.exp(s - m)
        den = jnp.sum(p, axis=1, keepdims=True) + jnp.exp(sk - m)
        o = _dot(p.astype(BF16), vb) / den
        for g in range(GQA_GROUP):
            c0 = (h * GQA_GROUP + g) * HEAD_DIM
            o_ref[:, c0:c0 + HEAD_DIM] = o[g * SEQ:(g + 1) * SEQ, :].astype(BF16)


def _ctx_attn(proj, sink, qg, kg):
    m = proj.shape[0]
    kvw = N_KV_HEADS * HEAD_DIM
    return pl.pallas_call(
        _ctx_attn_kernel,
        grid=(m // SEQ,),
        in_specs=[
            pl.BlockSpec(memory_space=pltpu.SMEM),
            pl.BlockSpec((SEQ, D_MODEL), lambda b: (b, OFF_Q // D_MODEL)),
            pl.BlockSpec((SEQ, kvw), lambda b: (b, OFF_K // kvw)),
            pl.BlockSpec((SEQ, kvw), lambda b: (b, OFF_V // kvw)),
            pl.BlockSpec((1, HEAD_DIM), lambda b: (0, 0)),
            pl.BlockSpec((1, HEAD_DIM), lambda b: (0, 0)),
        ],
        out_specs=[
            pl.BlockSpec((SEQ, D_MODEL), lambda b: (b, 0)),
            pl.BlockSpec((SEQ, kvw), lambda b: (b, 0)),
            pl.BlockSpec((SEQ, kvw), lambda b: (b, 0)),
        ],
        out_shape=[
            jax.ShapeDtypeStruct((m, D_MODEL), BF16),
            jax.ShapeDtypeStruct((m, kvw), F32),
            jax.ShapeDtypeStruct((m, kvw), F32),
        ],
        compiler_params=_cp(("arbitrary",), 40),
        name="ctx_attention",
    )(sink, proj, proj, proj, qg, kg)


def _rope(x, c_ref, sa_ref, sb_ref):
    up = pltpu.roll(x, HEAD_DIM - 32, axis=1)
    dn = pltpu.roll(x, 32, axis=1)
    return x * c_ref[...] + up * sa_ref[...] + dn * sb_ref[...]


def _lat_attn_kernel(sink_ref, q_ref, k_ref, v_ref, ck_ref, cv_ref, qg_ref, kg_ref,
                     c_ref, sa_ref, sb_ref, o_ref, q_scr, k_scr, v_scr):
    t = DEC_SEQ
    scale = HEAD_DIM ** -0.5
    h = pl.program_id(1)
    zpad = jnp.zeros((BLOCK, HEAD_DIM), BF16)
    k_scr[0:BLOCK, :] = zpad
    k_scr[t + BLOCK:t + 2 * BLOCK, :] = zpad
    v_scr[0:BLOCK, :] = zpad
    v_scr[t + BLOCK:t + 2 * BLOCK, :] = zpad
    kr = _rope(_rms(k_ref[...], kg_ref[...]), c_ref, sa_ref, sb_ref)
    k_scr[BLOCK:t + BLOCK, :] = kr.astype(BF16)
    v_scr[BLOCK:t + BLOCK, :] = v_ref[...].astype(BF16)
    for g in range(GQA_GROUP):
        qn = _rms(q_ref[:, g * HEAD_DIM:(g + 1) * HEAD_DIM], qg_ref[...])
        q_scr[g] = _rope(qn, c_ref, sa_ref, sb_ref).astype(BF16)
    ckb = ck_ref[...].astype(BF16)
    cvb = cv_ref[...].astype(BF16)
    sk = _sink_col(sink_ref, h, BLOCK)
    rows = GQA_GROUP * BLOCK
    qi = lax.broadcasted_iota(jnp.int32, (rows, 3 * BLOCK), 0) % BLOCK
    kj = lax.broadcasted_iota(jnp.int32, (rows, 3 * BLOCK), 1)
    band = (kj >= qi) & (kj <= qi + 2 * WINDOW)
    neg = jnp.finfo(F32).min

    def body(n, carry):
        r0 = pl.multiple_of(n * BLOCK, BLOCK)
        q4 = jnp.concatenate([q_scr[g, pl.ds(r0, BLOCK), :] for g in range(GQA_GROUP)], axis=0)
        kw = k_scr[pl.ds(r0, 3 * BLOCK), :]
        vw = v_scr[pl.ds(r0, 3 * BLOCK), :]
        kjg = kj + (n - 1) * BLOCK
        valid = band & (kjg >= 0) & (kjg < t)
        s_win = lax.dot_general(q4, kw, NT_DIMS, preferred_element_type=F32) * scale
        s_win = jnp.where(valid, s_win, neg)
        s_ctx = lax.dot_general(q4, ckb, NT_DIMS, preferred_element_type=F32) * scale
        m = jnp.maximum(jnp.max(s_win, axis=1, keepdims=True),
                        jnp.max(s_ctx, axis=1, keepdims=True))
        m = jnp.maximum(m, sk)
        p_win = jnp.exp(s_win - m)
        p_ctx = jnp.exp(s_ctx - m)
        den = (jnp.sum(p_win, axis=1, keepdims=True) + jnp.sum(p_ctx, axis=1, keepdims=True)
               + jnp.exp(sk - m))
        o = (_dot(p_win.astype(BF16), vw) + _dot(p_ctx.astype(BF16), cvb)) / den
        for g in range(GQA_GROUP):
            o_ref[pl.ds(r0, BLOCK), g * HEAD_DIM:(g + 1) * HEAD_DIM] = (
                o[g * BLOCK:(g + 1) * BLOCK, :].astype(BF16))
        return carry

    lax.fori_loop(0, t // BLOCK, body, 0, unroll=2)


def _lat_attn(proj, ck, cv, sink, qg, kg, rope_c, rope_sa, rope_sb):
    m = proj.shape[0]
    t = DEC_SEQ
    gw = GQA_GROUP * HEAD_DIM
    tab = pl.BlockSpec((t, HEAD_DIM), lambda b, h: (0, 0))
    vec = pl.BlockSpec((1, HEAD_DIM), lambda b, h: (0, 0))
    cache = pl.BlockSpec((None, PAST_LEN, HEAD_DIM), lambda b, h: (b, 0, h))
    return pl.pallas_call(
        _lat_attn_kernel,
        grid=(m // t, N_KV_HEADS),
        in_specs=[
            pl.BlockSpec(memory_space=pltpu.SMEM),
            pl.BlockSpec((t, gw), lambda b, h: (b, OFF_Q // gw + h)),
            pl.BlockSpec((t, HEAD_DIM), lambda b, h: (b, OFF_K // HEAD_DIM + h)),
            pl.BlockSpec((t, HEAD_DIM), lambda b, h: (b, OFF_V // HEAD_DIM + h)),
            cache, cache, vec, vec, tab, tab, tab,
        ],
        out_specs=pl.BlockSpec((t, gw), lambda b, h: (b, h)),
        out_shape=jax.ShapeDtypeStruct((m, D_MODEL), BF16),
        scratch_shapes=[
            pltpu.VMEM((GQA_GROUP, t, HEAD_DIM), BF16),
            pltpu.VMEM((t + 2 * BLOCK, HEAD_DIM), BF16),
            pltpu.VMEM((t + 2 * BLOCK, HEAD_DIM), BF16),
        ],
        compiler_params=_cp(("arbitrary", "arbitrary"), 40),
        name="latent_attention",
    )(sink, proj, proj, proj, ck, cv, qg, kg, rope_c, rope_sa, rope_sb)


def _log_sigmoid(x):
    return jnp.minimum(x, 0.0) - jnp.log(1.0 + jnp.exp(-jnp.abs(x)))


def _split3(x):
    hi = x.astype(BF16)
    r = x - hi.astype(F32)
    mid = r.astype(BF16)
    lo = (r - mid.astype(F32)).astype(BF16)
    return hi, mid, lo


def _gla_kernel(*refs, t, has_state):
    if has_state:
        (q_ref, k_ref, v_ref, a_ref, waf_ref, wab_ref, baf_ref, bab_ref, gn_ref,
         s0f_ref, s0b_ref, og_ref, *scr) = refs
    else:
        (q_ref, k_ref, v_ref, a_ref, waf_ref, wab_ref, baf_ref, bab_ref, gn_ref,
         og_ref, sf_out, sb_out, *scr) = refs
    stf, stb, vb, ob = scr[:4]
    qsf, kuf, decf, *tmpf = scr[4:13]
    qsb, kub, decb, *tmpb = scr[13:]
    c = GLA_CHUNK
    sc = GLA_SUPER
    nsub = sc // c
    nsc = t // sc
    vb[...] = v_ref[...].astype(BF16)

    row = lax.broadcasted_iota(jnp.int32, (sc, sc), 0)
    col = lax.broadcasted_iota(jnp.int32, (sc, sc), 1)
    rc, cc = row // c, col // c

    def intra(si, fwd):
        r0 = pl.multiple_of(si * sc, sc)
        w_ref, b_ref, qs_scr, ku_scr, dec_scr, tmp, dst = (
            (waf_ref, baf_ref, qsf, kuf, decf, tmpf, og_ref) if fwd
            else (wab_ref, bab_ref, qsb, kub, decb, tmpb, ob))
        qd, kd, ke, q2, q3, amat = tmp
        tri = jnp.where((col <= row) if fwd else (col >= row), 1.0, 0.0).astype(BF16)
        dist = (rc - cc) if fwd else (cc - rc)
        x = _dot(a_ref[pl.ds(r0, sc), :].astype(BF16), w_ref[...]) + b_ref[...]
        hi, mid, lo = _split3(_log_sigmoid(x) / GLA_TAU)
        cum = _dot(tri, hi) + _dot(tri, mid) + _dot(tri, lo)
        zero = jnp.zeros((1, GLA_HK), F32)

        def at_start(j):
            if fwd:
                return cum[j * c - 1:j * c, :] if j > 0 else zero
            return cum[(j + 1) * c:(j + 1) * c + 1, :] if j < nsub - 1 else zero

        def at_end(j):
            return cum[(j + 1) * c - 1:(j + 1) * c, :] if fwd else cum[j * c:j * c + 1, :]

        total = at_end(nsub - 1) if fwd else at_end(0)
        dec_scr[pl.ds(pl.multiple_of(si * 8, 8), 8), :] = jnp.broadcast_to(jnp.exp(total), (8, GLA_HK))
        for j in range(nsub):
            rs = slice(j * c, (j + 1) * c)
            rows = pl.ds(pl.multiple_of(r0 + j * c, c), c)
            cj = cum[rs, :]
            cs, ce = at_start(j), at_end(j)
            q = q_ref[rows, :] * (GLA_HK ** -0.5)
            k = k_ref[rows, :]
            qdj = q * jnp.exp(cj - cs)
            kej = k * jnp.exp(ce - cj)
            qd[rs, :] = qdj.astype(BF16)
            kd[rs, :] = (k * jnp.exp(cs - cj)).astype(BF16)
            ke[rs, :] = kej.astype(BF16)
            qs_scr[rows, :] = (qdj * jnp.exp(cs)).astype(BF16)
            ku_scr[rows, :] = (kej * jnp.exp(total - ce)).astype(BF16)
            p2 = j - 2 if fwd else j + 2
            if 0 <= p2 < nsub:
                l2 = (j - 2) if fwd else j
                q2[l2 * c:(l2 + 1) * c, :] = (qdj * jnp.exp(cs - at_end(p2))).astype(BF16)
            p3 = j - 3 if fwd else j + 3
            if 0 <= p3 < nsub:
                q3[...] = (qdj * jnp.exp(cs - at_end(p3))).astype(BF16)
        nt = lambda a, b: lax.dot_general(a, b, NT_DIMS, preferred_element_type=F32)
        tril = (col <= row) if fwd else (col >= row)
        amat[...] = (jnp.where((dist == 0) & tril, nt(qd[...], kd[...]), 0.0)
                     + jnp.where(dist == 1, nt(qd[...], ke[...]), 0.0))
        r2 = slice(2 * c, sc) if fwd else slice(0, 2 * c)
        amat[r2, :] += jnp.where(dist[r2, :] == 2, nt(q2[...], ke[...]), 0.0)
        r3 = slice(3 * c, sc) if fwd else slice(0, c)
        amat[r3, :] += jnp.where(dist[r3, :] == 3, nt(q3[...], ke[...]), 0.0)
        dst[pl.ds(r0, sc), :] = _dot(amat[...].astype(BF16), vb[pl.ds(r0, sc), :])

    def intra_body(i, carry):
        intra(i, True)
        intra(nsc - 1 - i, False)
        return carry

    lax.fori_loop(0, nsc, intra_body, 0)

    if has_state:
        for i in range(nsc):
            for fwd in (True, False):
                si = i if fwd else nsc - 1 - i
                rows = slice(si * sc, (si + 1) * sc)
                st_ref, s0_ref, qs_scr, ku_scr, dec_scr, dst = (
                    (stf, s0f_ref, qsf, kuf, decf, og_ref) if fwd
                    else (stb, s0b_ref, qsb, kub, decb, ob))
                st = s0_ref[...] if i == 0 else st_ref[...]
                dst[rows, :] += _dot(qs_scr[rows, :], st.astype(BF16))
                if i < nsc - 1:
                    dec = jnp.broadcast_to(dec_scr[si * 8:si * 8 + 1, :], (LANES, GLA_HK)).T
                    dec = jnp.concatenate([dec] * (GLA_HV // LANES), axis=1)
                    st_ref[...] = st * dec + lax.dot_general(
                        ku_scr[rows, :], vb[rows, :], TN_DIMS, preferred_element_type=F32)
    else:
        sf_out[...] = lax.dot_general(kuf[...], vb[...], TN_DIMS, preferred_element_type=F32)
        sb_out[...] = lax.dot_general(kub[...], vb[...], TN_DIMS, preferred_element_type=F32)
    og_ref[...] = _rms(og_ref[...] + ob[...], gn_ref[...])


def _gla(proj, aproj, waf, wab, baf, bab, gn, t, s0f=None, s0b=None):
    m = proj.shape[0]
    nb = m // t
    has_state = s0f is not None
    assert t % GLA_SUPER == 0 and (has_state or t == GLA_SUPER)
    sc, c = GLA_SUPER, GLA_CHUNK
    st_shape_vmem = (GLA_HK, GLA_HV) if has_state else (8, LANES)
    per_direction = [
        pltpu.VMEM((t, GLA_HK), BF16),
        pltpu.VMEM((t, GLA_HK), BF16),
        pltpu.VMEM((8 * (t // sc), GLA_HK), F32),
        pltpu.VMEM((sc, GLA_HK), BF16),
        pltpu.VMEM((sc, GLA_HK), BF16),
        pltpu.VMEM((sc, GLA_HK), BF16),
        pltpu.VMEM((sc - 2 * c, GLA_HK), BF16),
        pltpu.VMEM((sc - 3 * c, GLA_HK), BF16),
        pltpu.VMEM((sc, sc), F32),
    ]
    st_spec = pl.BlockSpec((None, GLA_HK, GLA_HV), lambda b, h: (b * GLA_HEADS + h, 0, 0))
    in_specs = [
        pl.BlockSpec((t, GLA_HK), lambda b, h: (b, OFF_GQ // GLA_HK + h)),
        pl.BlockSpec((t, GLA_HK), lambda b, h: (b, OFF_GK // GLA_HK + h)),
        pl.BlockSpec((t, GLA_HV), lambda b, h: (b, OFF_GV // GLA_HV + h)),
        pl.BlockSpec((t, LANES), lambda b, h: (b, 0)),
        pl.BlockSpec((LANES, GLA_HK), lambda b, h: (0, h)),
        pl.BlockSpec((LANES, GLA_HK), lambda b, h: (0, h)),
        pl.BlockSpec((1, GLA_HK), lambda b, h: (0, h)),
        pl.BlockSpec((1, GLA_HK), lambda b, h: (0, h)),
        pl.BlockSpec((1, GLA_HV), lambda b, h: (0, h)),
    ]
    args = [proj, proj, proj, aproj, waf, wab, baf, bab, gn]
    og_spec = pl.BlockSpec((t, GLA_HV), lambda b, h: (b, h))
    og_shape = jax.ShapeDtypeStruct((m, GLA_DV), F32)
    if has_state:
        in_specs += [st_spec, st_spec]
        args += [s0f, s0b]
        out_specs, out_shape = og_spec, og_shape
    else:
        st_shape = jax.ShapeDtypeStruct((nb * GLA_HEADS, GLA_HK, GLA_HV), F32)
        out_specs, out_shape = [og_spec, st_spec, st_spec], [og_shape, st_shape, st_shape]
    return pl.pallas_call(
        functools.partial(_gla_kernel, t=t, has_state=has_state),
        grid=(nb, GLA_HEADS),
        in_specs=in_specs,
        out_specs=out_specs,
        out_shape=out_shape,
        scratch_shapes=[
            pltpu.VMEM(st_shape_vmem, F32),
            pltpu.VMEM(st_shape_vmem, F32),
            pltpu.VMEM((t, GLA_HV), BF16),
            pltpu.VMEM((t, GLA_HV), F32),
        ] + 2 * per_direction,
        compiler_params=_cp(("arbitrary", "arbitrary"), 56),
        name="gla_state" if has_state else "gla_zero",
    )(*args)


MERGE_TM = 256
MERGE_TN = 2048


def _merge_kernel(oa_ref, og_ref, gr_ref, ga_ref, gg_ref, o_ref):
    gr = gr_ref[...].astype(F32)
    o_gla = og_ref[...] * (gr * _sigmoid(gr))
    merged = (_sigmoid(ga_ref[...].astype(F32)) * oa_ref[...].astype(F32)
              + _sigmoid(gg_ref[...].astype(F32)) * o_gla)
    o_ref[...] = merged.astype(BF16)


def _merge(o_att, og, gates):
    m = o_att.shape[0]
    blk = lambda off: pl.BlockSpec((MERGE_TM, MERGE_TN), lambda i, j: (i, off // MERGE_TN + j))
    return pl.pallas_call(
        _merge_kernel,
        grid=(m // MERGE_TM, D_MODEL // MERGE_TN),
        in_specs=[blk(0), blk(0), blk(OFF_GR - OFF_GR), blk(OFF_GATT - OFF_GR), blk(OFF_GGLA - OFF_GR)],
        out_specs=blk(0),
        out_shape=jax.ShapeDtypeStruct((m, D_MODEL), BF16),
        compiler_params=_cp(("arbitrary", "arbitrary"), 40),
        name="branch_merge",
    )(o_att, og, gates, gates, gates)


OUT_TM = 512
OUT_TN = 1024


def _out_kernel(mg_ref, w_ref, x_ref, gt_ref, g2_ref, sh_ref, sc_ref, x1_ref, h2_ref, x1_scr):
    j = pl.program_id(1)
    nj = D_MODEL // OUT_TN
    x1 = x_ref[...] + gt_ref[...] * _dot(mg_ref[...], w_ref[...])
    x1_ref[...] = x1
    x1_scr[j] = x1

    @pl.when(j == nj - 1)
    def _():
        ssq = jnp.zeros((OUT_TM, 1), F32)
        for jj in range(nj):
            xs = x1_scr[jj]
            ssq = ssq + jnp.sum(xs * xs, axis=-1, keepdims=True)
        inv = lax.rsqrt(ssq / D_MODEL + EPS)
        for jj in range(nj):
            cs = slice(jj * OUT_TN, (jj + 1) * OUT_TN)
            y = x1_scr[jj] * inv * g2_ref[:, cs]
            h2_ref[:, cs] = (y * (1.0 + sc_ref[:, cs]) + sh_ref[:, cs]).astype(BF16)


def _out_proj(merged, w_out, x, gate1, g2, shift2, scale2, mod_off, rows_per_mod):
    m = x.shape[0]
    bpb = rows_per_mod // OUT_TM
    gate_spec = pl.BlockSpec((None, 1, OUT_TN), lambda i, j: (mod_off + i // bpb, 0, j))
    return pl.pallas_call(
        _out_kernel,
        grid=(m // OUT_TM, D_MODEL // OUT_TN),
        in_specs=[
            pl.BlockSpec((OUT_TM, D_MODEL), lambda i, j: (i, 0)),
            pl.BlockSpec((D_MODEL, OUT_TN), lambda i, j: (0, j)),
            pl.BlockSpec((OUT_TM, OUT_TN), lambda i, j: (i, j)),
            gate_spec,
            pl.BlockSpec((1, D_MODEL), lambda i, j: (0, 0)),
            _mod_spec(mod_off, bpb),
            _mod_spec(mod_off, bpb),
        ],
        out_specs=[
            pl.BlockSpec((OUT_TM, OUT_TN), lambda i, j: (i, j)),
            pl.BlockSpec((OUT_TM, D_MODEL), lambda i, j: (i, 0)),
        ],
        out_shape=[
            jax.ShapeDtypeStruct((m, D_MODEL), F32),
            jax.ShapeDtypeStruct((m, D_MODEL), BF16),
        ],
        scratch_shapes=[pltpu.VMEM((D_MODEL // OUT_TN, OUT_TM, OUT_TN), F32)],
        compiler_params=_cp(("arbitrary", "arbitrary"), 56),
        name="out_proj_residual_norm",
    )(merged, w_out, x, gate1, g2, shift2, scale2)


FFI_TM = 2048
FFI_TN = 256


def _ffn_in_kernel(h_ref, wg_ref, wu_ref, o_ref):
    h = h_ref[...]
    g = _dot(h, wg_ref[...].astype(BF16))
    u = _dot(h, wu_ref[...].astype(BF16))
    o_ref[...] = (g * _sigmoid(g) * u).astype(BF16)


def _ffn_in(h2, w):
    m = h2.shape[0]
    return pl.pallas_call(
        _ffn_in_kernel,
        grid=(m // FFI_TM, D_FF // FFI_TN),
        in_specs=[
            pl.BlockSpec((FFI_TM, D_MODEL), lambda i, j: (i, 0)),
            pl.BlockSpec((D_MODEL, FFI_TN), lambda i, j: (0, j)),
            pl.BlockSpec((D_MODEL, FFI_TN), lambda i, j: (0, D_FF // FFI_TN + j)),
        ],
        out_specs=pl.BlockSpec((FFI_TM, FFI_TN), lambda i, j: (i, j)),
        out_shape=jax.ShapeDtypeStruct((m, D_FF), BF16),
        compiler_params=_cp(("arbitrary", "arbitrary"), 58),
        name="ffn_in_swiglu",
    )(h2, w, w)


FFO_TM = 512
FFO_TN = 512


def _ffn_out_kernel(a_ref, w_ref, x_ref, gt_ref, o_ref):
    o_ref[...] = x_ref[...] + gt_ref[...] * _dot(a_ref[...], w_ref[...])


def _ffn_out(act, w, x1, gate2, mod_off, rows_per_mod):
    m = x1.shape[0]
    bpb = rows_per_mod // FFO_TM
    return pl.pallas_call(
        _ffn_out_kernel,
        grid=(m // FFO_TM, D_MODEL // FFO_TN),
        in_specs=[
            pl.BlockSpec((FFO_TM, D_FF), lambda i, j: (i, 0)),
            pl.BlockSpec((D_FF, FFO_TN), lambda i, j: (0, j)),
            pl.BlockSpec((FFO_TM, FFO_TN), lambda i, j: (i, j)),
            pl.BlockSpec((None, 1, FFO_TN), lambda i, j: (mod_off + i // bpb, 0, j)),
        ],
        out_specs=pl.BlockSpec((FFO_TM, FFO_TN), lambda i, j: (i, j)),
        out_shape=jax.ShapeDtypeStruct((m, D_MODEL), F32),
        compiler_params=_cp(("arbitrary", "arbitrary"), 56),
        name="ffn_out_residual",
    )(act, w, x1, gate2)


def _rope_tables(t):
    rows = t // GRID_W
    half = HEAD_DIM // 2
    row = jnp.repeat(jnp.arange(rows, dtype=F32), GRID_W)
    col = jnp.tile(jnp.arange(GRID_W, dtype=F32), rows)
    inv = ROPE_THETA ** (-jnp.arange(0, half, 2, dtype=F32) / half)
    ar = row[:, None] * inv[None, :]
    ac = col[:, None] * inv[None, :]
    cr, sr, cc, sc = jnp.cos(ar), jnp.sin(ar), jnp.cos(ac), jnp.sin(ac)
    z = jnp.zeros_like(sr)
    tab_c = jnp.concatenate([cr, cr, cc, cc], axis=1)
    tab_sa = jnp.concatenate([-sr, z, -sc, z], axis=1)
    tab_sb = jnp.concatenate([z, sr, z, sc], axis=1)
    return tab_c, tab_sa, tab_sb


def _trunk(x, mod, mod_off, rows_per_mod, wts, ctx):
    (g1, g2, w_in, wa, qg, kg, sink, waf, wab, baf, bab, gn, w_out, w_fi, w_fo) = wts
    shift1, scale1, gate1, shift2, scale2, gate2 = mod
    h, aproj = _norm_mod(x, g1, shift1, scale1, wa, mod_off, rows_per_mod)
    proj = _proj(h, w_in, 0, OFF_GR, F32)
    gates = _proj(h, w_in, OFF_GR, D_WIDE - OFF_GR, BF16)
    if ctx is None:
        o_att, new_k, new_v = _ctx_attn(proj, sink, qg, kg)
        og, s_f, s_b = _gla(proj, aproj, waf, wab, baf, bab, gn, SEQ)
        extra = (new_k, new_v, s_f, s_b)
    else:
        ck, cv, s0f, s0b, rope = ctx
        o_att = _lat_attn(proj, ck, cv, sink, qg, kg, *rope)
        og = _gla(proj, aproj, waf, wab, baf, bab, gn, DEC_SEQ, s0f, s0b)
        extra = None
    merged = _merge(o_att, og, gates)
    x1, h2 = _out_proj(merged, w_out, x, gate1, g2, shift2, scale2, mod_off, rows_per_mod)
    act = _ffn_in(h2, w_fi)
    y = _ffn_out(act, w_fo, x1, gate2, mod_off, rows_per_mod)
    return y, extra


def kernel(x_prompt, x_sample, c, cache_k, cache_v, state_gla_fwd, state_gla_bwd, c_ctx, w_ada, b_ada, norm1_g, norm2_g, w_in, q_norm_g, k_norm_g, attn_sink, w_a2_fwd, b_a_fwd, w_a2_bwd, b_a_bwd, gla_norm_g, w_out, w_ffn_in, w_ffn_out):
    assert w_ada.shape[0] == 1, "single trunk layer"
    cc = jnp.zeros((MOD_ROWS, D_MODEL), F32).at[0].set(c_ctx).at[1:1 + DEC_BATCH].set(c)
    mod_all = _ada(cc, w_ada[0], b_ada[0][None, :])
    mod = tuple(mod_all[:, i * D_MODEL:(i + 1) * D_MODEL].reshape(MOD_ROWS, 1, D_MODEL)
                for i in range(N_MOD))

    r = GLA_GATE_RANK
    wa = jnp.zeros((D_MODEL, LANES), BF16).at[:, :2 * r].set(w_in[0][:, D_WIDE:].astype(BF16))
    waf = jnp.zeros((LANES, GLA_DK), BF16).at[:r].set(w_a2_fwd[0].astype(BF16))
    wab = jnp.zeros((LANES, GLA_DK), BF16).at[r:2 * r].set(w_a2_bwd[0].astype(BF16))
    wts = (
        norm1_g[0][None, :], norm2_g[0][None, :],
        w_in[0].T, wa,
        q_norm_g[0][None, :], k_norm_g[0][None, :], attn_sink[0],
        waf, wab, b_a_fwd[0][None, :], b_a_bwd[0][None, :], gla_norm_g[0][None, :],
        w_out[0].astype(BF16),
        w_ffn_in[0],
        w_ffn_out[0].astype(BF16),
    )

    xp = x_prompt.reshape(BATCH * SEQ, D_MODEL)
    xs = x_sample.reshape(DEC_BATCH * DEC_SEQ, D_MODEL)
    yp, (new_k, new_v, s_f, s_b) = _trunk(xp, mod, 0, BATCH * SEQ, wts, None)
    kvw = N_KV_HEADS * HEAD_DIM
    ctx = (
        cache_k[:, 0].reshape(DEC_BATCH, PAST_LEN, kvw),
        cache_v[:, 0].reshape(DEC_BATCH, PAST_LEN, kvw),
        state_gla_fwd[:, 0].reshape(DEC_BATCH * GLA_HEADS, GLA_HK, GLA_HV),
        state_gla_bwd[:, 0].reshape(DEC_BATCH * GLA_HEADS, GLA_HK, GLA_HV),
        _rope_tables(DEC_SEQ),
    )
    ys, _ = _trunk(xs, mod, 1, DEC_SEQ, wts, ctx)
    return (
        yp.reshape(BATCH, SEQ, D_MODEL),
        ys.reshape(DEC_BATCH, DEC_SEQ, D_MODEL),
        new_k.reshape(BATCH, 1, SEQ, N_KV_HEADS, HEAD_DIM),
        new_v.reshape(BATCH, 1, SEQ, N_KV_HEADS, HEAD_DIM),
        s_f.reshape(BATCH, 1, GLA_HEADS, GLA_HK, GLA_HV),
        s_b.reshape(BATCH, 1, GLA_HEADS, GLA_HK, GLA_HV),
    )
```

```python
import functools

import jax
import jax.numpy as jnp
from jax import lax
from jax.experimental import pallas as pl
from jax.experimental.pallas import tpu as pltpu

F32 = jnp.float32
BF16 = jnp.bfloat16

D_MODEL = 4096
BATCH = 32
SEQ = 256
DEC_BATCH = 8
DEC_SEQ = 1024
PAST_LEN = 256
GRID_W = 64
HEAD_DIM = 128
N_Q_HEADS = 32
N_KV_HEADS = 8
GQA_GROUP = N_Q_HEADS // N_KV_HEADS
WINDOW = 128
BLOCK = 128
ROPE_THETA = 10000.0
GLA_HEADS = 4
GLA_DK = D_MODEL // 2
GLA_DV = D_MODEL
GLA_HK = GLA_DK // GLA_HEADS
GLA_HV = GLA_DV // GLA_HEADS
GLA_GATE_RANK = 16
GLA_TAU = 16.0
GLA_CHUNK = 64
GLA_SUPER = 256
D_FF = 11008
N_MOD = 6
EPS = 1e-6

OFF_Q = 0
OFF_K = 4096
OFF_V = 5120
OFF_GQ = 6144
OFF_GK = 8192
OFF_GV = 10240
OFF_GR = 14336
OFF_GATT = 18432
OFF_GGLA = 22528
D_WIDE = 26624
LANES = 128
MOD_ROWS = 16

MIB = 1024 * 1024
NT_DIMS = (((1,), (1,)), ((), ()))
TN_DIMS = (((0,), (0,)), ((), ()))


def _cp(sem, vmem_mib):
    return pltpu.CompilerParams(dimension_semantics=sem, vmem_limit_bytes=vmem_mib * MIB)


def _rms(x, g):
    ms = jnp.mean(x * x, axis=-1, keepdims=True)
    return x * lax.rsqrt(ms + EPS) * g


def _sigmoid(x):
    return 1.0 / (1.0 + jnp.exp(-x))


def _dot(a, b):
    return jnp.dot(a, b, preferred_element_type=F32)


ADA_TN = 512


def _ada_kernel(c_ref, w_ref, b_ref, o_ref):
    c = c_ref[...]
    s = (c * _sigmoid(c)).astype(BF16)
    o_ref[...] = _dot(s, w_ref[...].astype(BF16)) + b_ref[...]


def _ada(cc, w_ada, b_ada):
    n = w_ada.shape[1]
    return pl.pallas_call(
        _ada_kernel,
        grid=(n // ADA_TN,),
        in_specs=[
            pl.BlockSpec((MOD_ROWS, D_MODEL), lambda j: (0, 0)),
            pl.BlockSpec((D_MODEL, ADA_TN), lambda j: (0, j)),
            pl.BlockSpec((1, ADA_TN), lambda j: (0, j)),
        ],
        out_specs=pl.BlockSpec((MOD_ROWS, ADA_TN), lambda j: (0, j)),
        out_shape=jax.ShapeDtypeStruct((MOD_ROWS, n), F32),
        compiler_params=_cp(("arbitrary",), 40),
        name="ada_ln",
    )(cc, w_ada, b_ada)


NORM_TM = 512
PROJ_TM = 2048
PROJ_TN = 512


def _mod_spec(mod_off, bpb):
    return pl.BlockSpec((None, 1, D_MODEL), lambda i, *_: (mod_off + i // bpb, 0, 0))


def _norm_kernel(x_ref, g_ref, sh_ref, sc_ref, wa_ref, h_ref, a_ref):
    h = _rms(x_ref[...], g_ref[...]) * (1.0 + sc_ref[...]) + sh_ref[...]
    hb = h.astype(BF16)
    h_ref[...] = hb
    a_ref[...] = _dot(hb, wa_ref[...])


def _norm_mod(x, g1, shift, scale, wa, mod_off, rows_per_mod):
    m = x.shape[0]
    bpb = rows_per_mod // NORM_TM
    return pl.pallas_call(
        _norm_kernel,
        grid=(m // NORM_TM,),
        in_specs=[
            pl.BlockSpec((NORM_TM, D_MODEL), lambda i: (i, 0)),
            pl.BlockSpec((1, D_MODEL), lambda i: (0, 0)),
            _mod_spec(mod_off, bpb),
            _mod_spec(mod_off, bpb),
            pl.BlockSpec((D_MODEL, LANES), lambda i: (0, 0)),
        ],
        out_specs=[
            pl.BlockSpec((NORM_TM, D_MODEL), lambda i: (i, 0)),
            pl.BlockSpec((NORM_TM, LANES), lambda i: (i, 0)),
        ],
        out_shape=[
            jax.ShapeDtypeStruct((m, D_MODEL), BF16),
            jax.ShapeDtypeStruct((m, LANES), F32),
        ],
        compiler_params=_cp(("arbitrary",), 40),
        name="norm_modulate",
    )(x, g1, shift, scale, wa)


def _proj_kernel(h_ref, wt_ref, o_ref):
    o_ref[...] = lax.dot_general(h_ref[...], wt_ref[...].astype(BF16), NT_DIMS,
                                 preferred_element_type=F32).astype(o_ref.dtype)


def _proj(h, wt, col0, ncols, out_dtype):
    m = h.shape[0]
    j0 = col0 // PROJ_TN
    return pl.pallas_call(
        _proj_kernel,
        grid=(m // PROJ_TM, ncols // PROJ_TN),
        in_specs=[
            pl.BlockSpec((PROJ_TM, D_MODEL), lambda i, j: (i, 0), pipeline_mode=pl.Buffered(1)),
            pl.BlockSpec((PROJ_TN, D_MODEL), lambda i, j: (j0 + j, 0)),
        ],
        out_specs=pl.BlockSpec((PROJ_TM, PROJ_TN), lambda i, j: (i, j)),
        out_shape=jax.ShapeDtypeStruct((m, ncols), out_dtype),
        compiler_params=_cp(("arbitrary", "arbitrary"), 48),
        name="in_proj",
    )(h, wt)


def _sink_col(sink_ref, h, rows_per_head):
    rid = lax.broadcasted_iota(jnp.int32, (GQA_GROUP * rows_per_head, 1), 0) // rows_per_head
    col = jnp.full(rid.shape, sink_ref[h * GQA_GROUP], F32)
    for g in range(1, GQA_GROUP):
        col = jnp.where(rid == g, sink_ref[h * GQA_GROUP + g], col)
    return col


def _ctx_attn_kernel(sink_ref, q_ref, k_ref, v_ref, qg_ref, kg_ref, o_ref, nk_ref, nv_ref):
    scale = HEAD_DIM ** -0.5
    nv_ref[...] = v_ref[...]
    for h in range(N_KV_HEADS):
        hs = slice(h * HEAD_DIM, (h + 1) * HEAD_DIM)
        kn = _rms(k_ref[:, hs], kg_ref[...])
        nk_ref[:, hs] = kn
        kb = kn.astype(BF16)
        vb = v_ref[:, hs].astype(BF16)
        qs = []
        for g in range(GQA_GROUP):
            c0 = (h * GQA_GROUP + g) * HEAD_DIM
            qs.append(_rms(q_ref[:, c0:c0 + HEAD_DIM], qg_ref[...]).astype(BF16))
        q4 = jnp.concatenate(qs, axis=0)
        s = lax.dot_general(q4, kb, NT_DIMS, preferred_element_type=F32) * scale
        sk = _sink_col(sink_ref, h, SEQ)
        m = jnp.maximum(jnp.max(s, axis=1, keepdims=True), sk)
        p = jnp.exp(s - m)
        den = jnp.sum(p, axis=1, keepdims=True) + jnp.exp(sk - m)
        o = _dot(p.astype(BF16), vb) / den
        for g in range(GQA_GROUP):
            c0 = (h * GQA_GROUP + g) * HEAD_DIM
            o_ref[:, c0:c0 + HEAD_DIM] = o[g * SEQ:(g + 1) * SEQ, :].astype(BF16)


def _ctx_attn(proj, sink, qg, kg):
    m = proj.shape[0]
    kvw = N_KV_HEADS * HEAD_DIM
    return pl.pallas_call(
        _ctx_attn_kernel,
        grid=(m // SEQ,),
        in_specs=[
            pl.BlockSpec(memory_space=pltpu.SMEM),
            pl.BlockSpec((SEQ, D_MODEL), lambda b: (b, OFF_Q // D_MODEL)),
            pl.BlockSpec((SEQ, kvw), lambda b: (b, OFF_K // kvw)),
            pl.BlockSpec((SEQ, kvw), lambda b: (b, OFF_V // kvw)),
            pl.BlockSpec((1, HEAD_DIM), lambda b: (0, 0)),
            pl.BlockSpec((1, HEAD_DIM), lambda b: (0, 0)),
        ],
        out_specs=[
            pl.BlockSpec((SEQ, D_MODEL), lambda b: (b, 0)),
            pl.BlockSpec((SEQ, kvw), lambda b: (b, 0)),
            pl.BlockSpec((SEQ, kvw), lambda b: (b, 0)),
        ],
        out_shape=[
            jax.ShapeDtypeStruct((m, D_MODEL), BF16),
            jax.ShapeDtypeStruct((m, kvw), F32),
            jax.ShapeDtypeStruct((m, kvw), F32),
        ],
        compiler_params=_cp(("arbitrary",), 40),
        name="ctx_attention",
    )(sink, proj, proj, proj, qg, kg)


def _rope(x, c_ref, sa_ref, sb_ref):
    up = pltpu.roll(x, HEAD_DIM - 32, axis=1)
    dn = pltpu.roll(x, 32, axis=1)
    return x * c_ref[...] + up * sa_ref[...] + dn * sb_ref[...]


def _lat_attn_kernel(sink_ref, q_ref, k_ref, v_ref, ck_ref, cv_ref, qg_ref, kg_ref,
                     c_ref, sa_ref, sb_ref, o_ref, q_scr, k_scr, v_scr):
    t = DEC_SEQ
    scale = HEAD_DIM ** -0.5
    h = pl.program_id(1)
    zpad = jnp.zeros((BLOCK, HEAD_DIM), BF16)
    k_scr[0:BLOCK, :] = zpad
    k_scr[t + BLOCK:t + 2 * BLOCK, :] = zpad
    v_scr[0:BLOCK, :] = zpad
    v_scr[t + BLOCK:t + 2 * BLOCK, :] = zpad
    kr = _rope(_rms(k_ref[...], kg_ref[...]), c_ref, sa_ref, sb_ref)
    k_scr[BLOCK:t + BLOCK, :] = kr.astype(BF16)
    v_scr[BLOCK:t + BLOCK, :] = v_ref[...].astype(BF16)
    for g in range(GQA_GROUP):
        qn = _rms(q_ref[:, g * HEAD_DIM:(g + 1) * HEAD_DIM], qg_ref[...])
        q_scr[g] = _rope(qn, c_ref, sa_ref, sb_ref).astype(BF16)
    ckb = ck_ref[...].astype(BF16)
    cvb = cv_ref[...].astype(BF16)
    sk = _sink_col(sink_ref, h, BLOCK)
    rows = GQA_GROUP * BLOCK
    qi = lax.broadcasted_iota(jnp.int32, (rows, 3 * BLOCK), 0) % BLOCK
    kj = lax.broadcasted_iota(jnp.int32, (rows, 3 * BLOCK), 1)
    band = (kj >= qi) & (kj <= qi + 2 * WINDOW)
    neg = jnp.finfo(F32).min

    def body(n, carry):
        r0 = pl.multiple_of(n * BLOCK, BLOCK)
        q4 = jnp.concatenate([q_scr[g, pl.ds(r0, BLOCK), :] for g in range(GQA_GROUP)], axis=0)
        kw = k_scr[pl.ds(r0, 3 * BLOCK), :]
        vw = v_scr[pl.ds(r0, 3 * BLOCK), :]
        kjg = kj + (n - 1) * BLOCK
        valid = band & (kjg >= 0) & (kjg < t)
        s_win = lax.dot_general(q4, kw, NT_DIMS, preferred_element_type=F32) * scale
        s_win = jnp.where(valid, s_win, neg)
        s_ctx = lax.dot_general(q4, ckb, NT_DIMS, preferred_element_type=F32) * scale
        m = jnp.maximum(jnp.max(s_win, axis=1, keepdims=True),
                        jnp.max(s_ctx, axis=1, keepdims=True))
        m = jnp.maximum(m, sk)
        p_win = jnp.exp(s_win - m)
        p_ctx = jnp.exp(s_ctx - m)
        den = (jnp.sum(p_win, axis=1, keepdims=True) + jnp.sum(p_ctx, axis=1, keepdims=True)
               + jnp.exp(sk - m))
        o = (_dot(p_win.astype(BF16), vw) + _dot(p_ctx.astype(BF16), cvb)) / den
        for g in range(GQA_GROUP):
            o_ref[pl.ds(r0, BLOCK), g * HEAD_DIM:(g + 1) * HEAD_DIM] = (
                o[g * BLOCK:(g + 1) * BLOCK, :].astype(BF16))
        return carry

    lax.fori_loop(0, t // BLOCK, body, 0, unroll=2)


def _lat_attn(proj, ck, cv, sink, qg, kg, rope_c, rope_sa, rope_sb):
    m = proj.shape[0]
    t = DEC_SEQ
    gw = GQA_GROUP * HEAD_DIM
    tab = pl.BlockSpec((t, HEAD_DIM), lambda b, h: (0, 0))
    vec = pl.BlockSpec((1, HEAD_DIM), lambda b, h: (0, 0))
    cache = pl.BlockSpec((None, PAST_LEN, HEAD_DIM), lambda b, h: (b, 0, h))
    return pl.pallas_call(
        _lat_attn_kernel,
        grid=(m // t, N_KV_HEADS),
        in_specs=[
            pl.BlockSpec(memory_space=pltpu.SMEM),
            pl.BlockSpec((t, gw), lambda b, h: (b, OFF_Q // gw + h)),
            pl.BlockSpec((t, HEAD_DIM), lambda b, h: (b, OFF_K // HEAD_DIM + h)),
            pl.BlockSpec((t, HEAD_DIM), lambda b, h: (b, OFF_V // HEAD_DIM + h)),
            cache, cache, vec, vec, tab, tab, tab,
        ],
        out_specs=pl.BlockSpec((t, gw), lambda b, h: (b, h)),
        out_shape=jax.ShapeDtypeStruct((m, D_MODEL), BF16),
        scratch_shapes=[
            pltpu.VMEM((GQA_GROUP, t, HEAD_DIM), BF16),
            pltpu.VMEM((t + 2 * BLOCK, HEAD_DIM), BF16),
            pltpu.VMEM((t + 2 * BLOCK, HEAD_DIM), BF16),
        ],
        compiler_params=_cp(("arbitrary", "arbitrary"), 40),
        name="latent_attention",
    )(sink, proj, proj, proj, ck, cv, qg, kg, rope_c, rope_sa, rope_sb)


def _log_sigmoid(x):
    return jnp.minimum(x, 0.0) - jnp.log(1.0 + jnp.exp(-jnp.abs(x)))


def _split2(x):
    hi = x.astype(BF16)
    lo = (x - hi.astype(F32)).astype(BF16)
    return hi, lo


def _gla_kernel(*refs, t, has_state):
    if has_state:
        (q_ref, k_ref, v_ref, a_ref, waf_ref, wab_ref, baf_ref, bab_ref, gn_ref,
         s0f_ref, s0b_ref, og_ref, *scr) = refs
    else:
        (q_ref, k_ref, v_ref, a_ref, waf_ref, wab_ref, baf_ref, bab_ref, gn_ref,
         og_ref, sf_out, sb_out, *scr) = refs
    stf, stb, vb, ob = scr[:4]
    qsf, kuf, decf, *tmpf = scr[4:13]
    qsb, kub, decb, *tmpb = scr[13:]
    c = GLA_CHUNK
    sc = GLA_SUPER
    nsub = sc // c
    nsc = t // sc
    vb[...] = v_ref[...].astype(BF16)

    row = lax.broadcasted_iota(jnp.int32, (sc, sc), 0)
    col = lax.broadcasted_iota(jnp.int32, (sc, sc), 1)
    rc, cc = row // c, col // c

    def intra(si, fwd):
        r0 = pl.multiple_of(si * sc, sc)
        w_ref, b_ref, qs_scr, ku_scr, dec_scr, tmp, dst = (
            (waf_ref, baf_ref, qsf, kuf, decf, tmpf, og_ref) if fwd
            else (wab_ref, bab_ref, qsb, kub, decb, tmpb, ob))
        qd, kd, ke, q2, q3, amat = tmp
        tri = jnp.where((col <= row) if fwd else (col >= row), 1.0, 0.0).astype(BF16)
        dist = (rc - cc) if fwd else (cc - rc)
        x = _dot(a_ref[pl.ds(r0, sc), :].astype(BF16), w_ref[...]) + b_ref[...]
        hi, lo = _split2(_log_sigmoid(x) / GLA_TAU)
        cum = _dot(tri, hi) + _dot(tri, lo)
        zero = jnp.zeros((1, GLA_HK), F32)

        def at_start(j):
            if fwd:
                return cum[j * c - 1:j * c, :] if j > 0 else zero
            return cum[(j + 1) * c:(j + 1) * c + 1, :] if j < nsub - 1 else zero

        def at_end(j):
            return cum[(j + 1) * c - 1:(j + 1) * c, :] if fwd else cum[j * c:j * c + 1, :]

        total = at_end(nsub - 1) if fwd else at_end(0)
        dec_scr[pl.ds(pl.multiple_of(si * 8, 8), 8), :] = jnp.broadcast_to(jnp.exp(total), (8, GLA_HK))
        for j in range(nsub):
            rs = slice(j * c, (j + 1) * c)
            rows = pl.ds(pl.multiple_of(r0 + j * c, c), c)
            cj = cum[rs, :]
            cs, ce = at_start(j), at_end(j)
            q = q_ref[rows, :] * (GLA_HK ** -0.5)
            k = k_ref[rows, :]
            qdj = q * jnp.exp(cj - cs)
            kej = k * jnp.exp(ce - cj)
            qd[rs, :] = qdj.astype(BF16)
            kd[rs, :] = (k * jnp.exp(cs - cj)).astype(BF16)
            ke[rs, :] = kej.astype(BF16)
            qs_scr[rows, :] = (qdj * jnp.exp(cs)).astype(BF16)
            ku_scr[rows, :] = (kej * jnp.exp(total - ce)).astype(BF16)
            p2 = j - 2 if fwd else j + 2
            if 0 <= p2 < nsub:
                l2 = (j - 2) if fwd else j
                q2[l2 * c:(l2 + 1) * c, :] = (qdj * jnp.exp(cs - at_end(p2))).astype(BF16)
            p3 = j - 3 if fwd else j + 3
            if 0 <= p3 < nsub:
                q3[...] = (qdj * jnp.exp(cs - at_end(p3))).astype(BF16)
        nt = lambda a, b: lax.dot_general(a, b, NT_DIMS, preferred_element_type=F32)
        tril = (col <= row) if fwd else (col >= row)
        amat[...] = (jnp.where((dist == 0) & tril, nt(qd[...], kd[...]), 0.0)
                     + jnp.where(dist == 1, nt(qd[...], ke[...]), 0.0))
        r2 = slice(2 * c, sc) if fwd else slice(0, 2 * c)
        amat[r2, :] += jnp.where(dist[r2, :] == 2, nt(q2[...], ke[...]), 0.0)
        r3 = slice(3 * c, sc) if fwd else slice(0, c)
        amat[r3, :] += jnp.where(dist[r3, :] == 3, nt(q3[...], ke[...]), 0.0)
        dst[pl.ds(r0, sc), :] = _dot(amat[...].astype(BF16), vb[pl.ds(r0, sc), :])

    def intra_body(i, carry):
        intra(i, True)
        intra(nsc - 1 - i, False)
        return carry

    lax.fori_loop(0, nsc, intra_body, 0, unroll=min(nsc, 2))

    if has_state:
        for i in range(nsc):
            for fwd in (True, False):
                si = i if fwd else nsc - 1 - i
                rows = slice(si * sc, (si + 1) * sc)
                st_ref, s0_ref, qs_scr, ku_scr, dec_scr, dst = (
                    (stf, s0f_ref, qsf, kuf, decf, og_ref) if fwd
                    else (stb, s0b_ref, qsb, kub, decb, ob))
                st = s0_ref[...] if i == 0 else st_ref[...]
                dst[rows, :] += _dot(qs_scr[rows, :], st.astype(BF16))
                if i < nsc - 1:
                    dec = jnp.broadcast_to(dec_scr[si * 8:si * 8 + 1, :], (LANES, GLA_HK)).T
                    dec = jnp.concatenate([dec] * (GLA_HV // LANES), axis=1)
                    st_ref[...] = st * dec + lax.dot_general(
                        ku_scr[rows, :], vb[rows, :], TN_DIMS, preferred_element_type=F32)
    else:
        sf_out[...] = lax.dot_general(kuf[...], vb[...], TN_DIMS, preferred_element_type=F32)
        sb_out[...] = lax.dot_general(kub[...], vb[...], TN_DIMS, preferred_element_type=F32)
    og_ref[...] = _rms(og_ref[...] + ob[...], gn_ref[...])


def _gla(proj, aproj, waf, wab, baf, bab, gn, t, s0f=None, s0b=None):
    m = proj.shape[0]
    nb = m // t
    has_state = s0f is not None
    assert t % GLA_SUPER == 0 and (has_state or t == GLA_SUPER)
    sc, c = GLA_SUPER, GLA_CHUNK
    st_shape_vmem = (GLA_HK, GLA_HV) if has_state else (8, LANES)
    per_direction = [
        pltpu.VMEM((t, GLA_HK), BF16),
        pltpu.VMEM((t, GLA_HK), BF16),
        pltpu.VMEM((8 * (t // sc), GLA_HK), F32),
        pltpu.VMEM((sc, GLA_HK), BF16),
        pltpu.VMEM((sc, GLA_HK), BF16),
        pltpu.VMEM((sc, GLA_HK), BF16),
        pltpu.VMEM((sc - 2 * c, GLA_HK), BF16),
        pltpu.VMEM((sc - 3 * c, GLA_HK), BF16),
        pltpu.VMEM((sc, sc), F32),
    ]
    st_spec = pl.BlockSpec((None, GLA_HK, GLA_HV), lambda b, h: (b * GLA_HEADS + h, 0, 0))
    in_specs = [
        pl.BlockSpec((t, GLA_HK), lambda b, h: (b, OFF_GQ // GLA_HK + h)),
        pl.BlockSpec((t, GLA_HK), lambda b, h: (b, OFF_GK // GLA_HK + h)),
        pl.BlockSpec((t, GLA_HV), lambda b, h: (b, OFF_GV // GLA_HV + h)),
        pl.BlockSpec((t, LANES), lambda b, h: (b, 0)),
        pl.BlockSpec((LANES, GLA_HK), lambda b, h: (0, h)),
        pl.BlockSpec((LANES, GLA_HK), lambda b, h: (0, h)),
        pl.BlockSpec((1, GLA_HK), lambda b, h: (0, h)),
        pl.BlockSpec((1, GLA_HK), lambda b, h: (0, h)),
        pl.BlockSpec((1, GLA_HV), lambda b, h: (0, h)),
    ]
    args = [proj, proj, proj, aproj, waf, wab, baf, bab, gn]
    og_spec = pl.BlockSpec((t, GLA_HV), lambda b, h: (b, h))
    og_shape = jax.ShapeDtypeStruct((m, GLA_DV), F32)
    if has_state:
        in_specs += [st_spec, st_spec]
        args += [s0f, s0b]
        out_specs, out_shape = og_spec, og_shape
    else:
        st_shape = jax.ShapeDtypeStruct((nb * GLA_HEADS, GLA_HK, GLA_HV), F32)
        out_specs, out_shape = [og_spec, st_spec, st_spec], [og_shape, st_shape, st_shape]
    return pl.pallas_call(
        functools.partial(_gla_kernel, t=t, has_state=has_state),
        grid=(nb, GLA_HEADS),
        in_specs=in_specs,
        out_specs=out_specs,
        out_shape=out_shape,
        scratch_shapes=[
            pltpu.VMEM(st_shape_vmem, F32),
            pltpu.VMEM(st_shape_vmem, F32),
            pltpu.VMEM((t, GLA_HV), BF16),
            pltpu.VMEM((t, GLA_HV), F32),
        ] + 2 * per_direction,
        compiler_params=_cp(("arbitrary", "arbitrary"), 56),
        name="gla_state" if has_state else "gla_zero",
    )(*args)


MERGE_TM = 256
MERGE_TN = 2048


def _merge_kernel(oa_ref, og_ref, gr_ref, ga_ref, gg_ref, o_ref):
    gr = gr_ref[...].astype(F32)
    o_gla = og_ref[...] * (gr * _sigmoid(gr))
    merged = (_sigmoid(ga_ref[...].astype(F32)) * oa_ref[...].astype(F32)
              + _sigmoid(gg_ref[...].astype(F32)) * o_gla)
    o_ref[...] = merged.astype(BF16)


def _merge(o_att, og, gates):
    m = o_att.shape[0]
    blk = lambda off: pl.BlockSpec((MERGE_TM, MERGE_TN), lambda i, j: (i, off // MERGE_TN + j))
    return pl.pallas_call(
        _merge_kernel,
        grid=(m // MERGE_TM, D_MODEL // MERGE_TN),
        in_specs=[blk(0), blk(0), blk(OFF_GR - OFF_GR), blk(OFF_GATT - OFF_GR), blk(OFF_GGLA - OFF_GR)],
        out_specs=blk(0),
        out_shape=jax.ShapeDtypeStruct((m, D_MODEL), BF16),
        compiler_params=_cp(("arbitrary", "arbitrary"), 40),
        name="branch_merge",
    )(o_att, og, gates, gates, gates)


OUT_TM = 512
OUT_TN = 1024


def _out_kernel(mg_ref, w_ref, x_ref, gt_ref, g2_ref, sh_ref, sc_ref, x1_ref, h2_ref, x1_scr):
    j = pl.program_id(1)
    nj = D_MODEL // OUT_TN
    x1 = x_ref[...] + gt_ref[...] * _dot(mg_ref[...], w_ref[...])
    x1_ref[...] = x1
    x1_scr[j] = x1

    @pl.when(j == nj - 1)
    def _():
        ssq = jnp.zeros((OUT_TM, 1), F32)
        for jj in range(nj):
            xs = x1_scr[jj]
            ssq = ssq + jnp.sum(xs * xs, axis=-1, keepdims=True)
        inv = lax.rsqrt(ssq / D_MODEL + EPS)
        for jj in range(nj):
            cs = slice(jj * OUT_TN, (jj + 1) * OUT_TN)
            y = x1_scr[jj] * inv * g2_ref[:, cs]
            h2_ref[:, cs] = (y * (1.0 + sc_ref[:, cs]) + sh_ref[:, cs]).astype(BF16)


def _out_proj(merged, w_out, x, gate1, g2, shift2, scale2, mod_off, rows_per_mod):
    m = x.shape[0]
    bpb = rows_per_mod // OUT_TM
    gate_spec = pl.BlockSpec((None, 1, OUT_TN), lambda i, j: (mod_off + i // bpb, 0, j))
    return pl.pallas_call(
        _out_kernel,
        grid=(m // OUT_TM, D_MODEL // OUT_TN),
        in_specs=[
            pl.BlockSpec((OUT_TM, D_MODEL), lambda i, j: (i, 0)),
            pl.BlockSpec((D_MODEL, OUT_TN), lambda i, j: (0, j)),
            pl.BlockSpec((OUT_TM, OUT_TN), lambda i, j: (i, j)),
            gate_spec,
            pl.BlockSpec((1, D_MODEL), lambda i, j: (0, 0)),
            _mod_spec(mod_off, bpb),
            _mod_spec(mod_off, bpb),
        ],
        out_specs=[
            pl.BlockSpec((OUT_TM, OUT_TN), lambda i, j: (i, j)),
            pl.BlockSpec((OUT_TM, D_MODEL), lambda i, j: (i, 0)),
        ],
        out_shape=[
            jax.ShapeDtypeStruct((m, D_MODEL), F32),
            jax.ShapeDtypeStruct((m, D_MODEL), BF16),
        ],
        scratch_shapes=[pltpu.VMEM((D_MODEL // OUT_TN, OUT_TM, OUT_TN), F32)],
        compiler_params=_cp(("arbitrary", "arbitrary"), 56),
        name="out_proj_residual_norm",
    )(merged, w_out, x, gate1, g2, shift2, scale2)


FFI_TM = 2048
FFI_TN = 256


def _ffn_in_kernel(h_ref, wg_ref, wu_ref, o_ref):
    h = h_ref[...]
    g = _dot(h, wg_ref[...].astype(BF16))
    u = _dot(h, wu_ref[...].astype(BF16))
    o_ref[...] = (g * _sigmoid(g) * u).astype(BF16)


def _ffn_in(h2, w):
    m = h2.shape[0]
    return pl.pallas_call(
        _ffn_in_kernel,
        grid=(m // FFI_TM, D_FF // FFI_TN),
        in_specs=[
            pl.BlockSpec((FFI_TM, D_MODEL), lambda i, j: (i, 0)),
            pl.BlockSpec((D_MODEL, FFI_TN), lambda i, j: (0, j)),
            pl.BlockSpec((D_MODEL, FFI_TN), lambda i, j: (0, D_FF // FFI_TN + j)),
        ],
        out_specs=pl.BlockSpec((FFI_TM, FFI_TN), lambda i, j: (i, j)),
        out_shape=jax.ShapeDtypeStruct((m, D_FF), BF16),
        compiler_params=_cp(("arbitrary", "arbitrary"), 58),
        name="ffn_in_swiglu",
    )(h2, w, w)


FFO_TM = 512
FFO_TN = 512


def _ffn_out_kernel(a_ref, w_ref, x_ref, gt_ref, o_ref):
    o_ref[...] = x_ref[...] + gt_ref[...] * _dot(a_ref[...], w_ref[...])


def _ffn_out(act, w, x1, gate2, mod_off, rows_per_mod):
    m = x1.shape[0]
    bpb = rows_per_mod // FFO_TM
    return pl.pallas_call(
        _ffn_out_kernel,
        grid=(m // FFO_TM, D_MODEL // FFO_TN),
        in_specs=[
            pl.BlockSpec((FFO_TM, D_FF), lambda i, j: (i, 0)),
            pl.BlockSpec((D_FF, FFO_TN), lambda i, j: (0, j)),
            pl.BlockSpec((FFO_TM, FFO_TN), lambda i, j: (i, j)),
            pl.BlockSpec((None, 1, FFO_TN), lambda i, j: (mod_off + i // bpb, 0, j)),
        ],
        out_specs=pl.BlockSpec((FFO_TM, FFO_TN), lambda i, j: (i, j)),
        out_shape=jax.ShapeDtypeStruct((m, D_MODEL), F32),
        compiler_params=_cp(("arbitrary", "arbitrary"), 56),
        name="ffn_out_residual",
    )(act, w, x1, gate2)


def _rope_tables(t):
    rows = t // GRID_W
    half = HEAD_DIM // 2
    row = jnp.repeat(jnp.arange(rows, dtype=F32), GRID_W)
    col = jnp.tile(jnp.arange(GRID_W, dtype=F32), rows)
    inv = ROPE_THETA ** (-jnp.arange(0, half, 2, dtype=F32) / half)
    ar = row[:, None] * inv[None, :]
    ac = col[:, None] * inv[None, :]
    cr, sr, cc, sc = jnp.cos(ar), jnp.sin(ar), jnp.cos(ac), jnp.sin(ac)
    z = jnp.zeros_like(sr)
    tab_c = jnp.concatenate([cr, cr, cc, cc], axis=1)
    tab_sa = jnp.concatenate([-sr, z, -sc, z], axis=1)
    tab_sb = jnp.concatenate([z, sr, z, sc], axis=1)
    return tab_c, tab_sa, tab_sb


def _trunk(x, mod, mod_off, rows_per_mod, wts, ctx):
    (g1, g2, w_in, wa, qg, kg, sink, waf, wab, baf, bab, gn, w_out, w_fi, w_fo) = wts
    shift1, scale1, gate1, shift2, scale2, gate2 = mod
    h, aproj = _norm_mod(x, g1, shift1, scale1, wa, mod_off, rows_per_mod)
    proj = _proj(h, w_in, 0, OFF_GR, F32)
    gates = _proj(h, w_in, OFF_GR, D_WIDE - OFF_GR, BF16)
    if ctx is None:
        o_att, new_k, new_v = _ctx_attn(proj, sink, qg, kg)
        og, s_f, s_b = _gla(proj, aproj, waf, wab, baf, bab, gn, SEQ)
        extra = (new_k, new_v, s_f, s_b)
    else:
        ck, cv, s0f, s0b, rope = ctx
        o_att = _lat_attn(proj, ck, cv, sink, qg, kg, *rope)
        og = _gla(proj, aproj, waf, wab, baf, bab, gn, DEC_SEQ, s0f, s0b)
        extra = None
    merged = _merge(o_att, og, gates)
    x1, h2 = _out_proj(merged, w_out, x, gate1, g2, shift2, scale2, mod_off, rows_per_mod)
    act = _ffn_in(h2, w_fi)
    y = _ffn_out(act, w_fo, x1, gate2, mod_off, rows_per_mod)
    return y, extra


def kernel(x_prompt, x_sample, c, cache_k, cache_v, state_gla_fwd, state_gla_bwd, c_ctx, w_ada, b_ada, norm1_g, norm2_g, w_in, q_norm_g, k_norm_g, attn_sink, w_a2_fwd, b_a_fwd, w_a2_bwd, b_a_bwd, gla_norm_g, w_out, w_ffn_in, w_ffn_out):
    assert w_ada.shape[0] == 1, "single trunk layer"
    cc = jnp.zeros((MOD_ROWS, D_MODEL), F32).at[0].set(c_ctx).at[1:1 + DEC_BATCH].set(c)
    mod_all = _ada(cc, w_ada[0], b_ada[0][None, :])
    mod = tuple(mod_all[:, i * D_MODEL:(i + 1) * D_MODEL].reshape(MOD_ROWS, 1, D_MODEL)
                for i in range(N_MOD))

    r = GLA_GATE_RANK
    wa = jnp.zeros((D_MODEL, LANES), BF16).at[:, :2 * r].set(w_in[0][:, D_WIDE:].astype(BF16))
    waf = jnp.zeros((LANES, GLA_DK), BF16).at[:r].set(w_a2_fwd[0].astype(BF16))
    wab = jnp.zeros((LANES, GLA_DK), BF16).at[r:2 * r].set(w_a2_bwd[0].astype(BF16))
    wts = (
        norm1_g[0][None, :], norm2_g[0][None, :],
        w_in[0].T, wa,
        q_norm_g[0][None, :], k_norm_g[0][None, :], attn_sink[0],
        waf, wab, b_a_fwd[0][None, :], b_a_bwd[0][None, :], gla_norm_g[0][None, :],
        w_out[0].astype(BF16),
        w_ffn_in[0],
        w_ffn_out[0].astype(BF16),
    )

    xp = x_prompt.reshape(BATCH * SEQ, D_MODEL)
    xs = x_sample.reshape(DEC_BATCH * DEC_SEQ, D_MODEL)
    yp, (new_k, new_v, s_f, s_b) = _trunk(xp, mod, 0, BATCH * SEQ, wts, None)
    kvw = N_KV_HEADS * HEAD_DIM
    ctx = (
        cache_k[:, 0].reshape(DEC_BATCH, PAST_LEN, kvw),
        cache_v[:, 0].reshape(DEC_BATCH, PAST_LEN, kvw),
        state_gla_fwd[:, 0].reshape(DEC_BATCH * GLA_HEADS, GLA_HK, GLA_HV),
        state_gla_bwd[:, 0].reshape(DEC_BATCH * GLA_HEADS, GLA_HK, GLA_HV),
        _rope_tables(DEC_SEQ),
    )
    ys, _ = _trunk(xs, mod, 1, DEC_SEQ, wts, ctx)
    return (
        yp.reshape(BATCH, SEQ, D_MODEL),
        ys.reshape(DEC_BATCH, DEC_SEQ, D_MODEL),
        new_k.reshape(BATCH, 1, SEQ, N_KV_HEADS, HEAD_DIM),
        new_v.reshape(BATCH, 1, SEQ, N_KV_HEADS, HEAD_DIM),
        s_f.reshape(BATCH, 1, GLA_HEADS, GLA_HK, GLA_HV),
        s_b.reshape(BATCH, 1, GLA_HEADS, GLA_HK, GLA_HV),
    )
```

```python
import functools
from typing import Callable, NamedTuple

import jax
import jax.numpy as jnp
from jax import lax
from jax.experimental import pallas as pl
from jax.experimental.pallas import tpu as pltpu

F32 = jnp.float32
BF16 = jnp.bfloat16

D_MODEL = 4096
BATCH = 32
SEQ = 256
DEC_BATCH = 8
DEC_SEQ = 1024
PAST_LEN = 256
GRID_W = 64
HEAD_DIM = 128
N_Q_HEADS = 32
N_KV_HEADS = 8
GQA_GROUP = N_Q_HEADS // N_KV_HEADS
WINDOW = 128
BLOCK = 128
ROPE_THETA = 10000.0
GLA_HEADS = 4
GLA_DK = D_MODEL // 2
GLA_DV = D_MODEL
GLA_HK = GLA_DK // GLA_HEADS
GLA_HV = GLA_DV // GLA_HEADS
GLA_GATE_RANK = 16
GLA_TAU = 16.0
GLA_CHUNK = 64
GLA_SUPER = 256
D_FF = 11008
N_MOD = 6
EPS = 1e-6

OFF_Q = 0
OFF_K = 4096
OFF_V = 5120
OFF_GQ = 6144
OFF_GK = 8192
OFF_GV = 10240
OFF_GR = 14336
OFF_GATT = 18432
OFF_GGLA = 22528
D_WIDE = 26624
LANES = 128
MOD_ROWS = 16

MIB = 1024 * 1024
NT_DIMS = (((1,), (1,)), ((), ()))
TN_DIMS = (((0,), (0,)), ((), ()))


def _cp(sem, vmem_mib):
    return pltpu.CompilerParams(dimension_semantics=sem, vmem_limit_bytes=vmem_mib * MIB)


def _rms(x, g):
    ms = jnp.mean(x * x, axis=-1, keepdims=True)
    return x * lax.rsqrt(ms + EPS) * g


def _sigmoid(x):
    return 0.5 * jnp.tanh(0.5 * x) + 0.5


def _dot(a, b):
    return jnp.dot(a, b, preferred_element_type=F32)


ADA_TN = 512


def _ada_kernel(c_ref, w_ref, b_ref, o_ref):
    c = c_ref[...]
    s = (c * _sigmoid(c)).astype(BF16)
    o_ref[...] = _dot(s, w_ref[...].astype(BF16)) + b_ref[...]


def _ada(cc, w_ada, b_ada):
    n = w_ada.shape[1]
    return pl.pallas_call(
        _ada_kernel,
        grid=(n // ADA_TN,),
        in_specs=[
            pl.BlockSpec((MOD_ROWS, D_MODEL), lambda j: (0, 0)),
            pl.BlockSpec((D_MODEL, ADA_TN), lambda j: (0, j)),
            pl.BlockSpec((1, ADA_TN), lambda j: (0, j)),
        ],
        out_specs=pl.BlockSpec((MOD_ROWS, ADA_TN), lambda j: (0, j)),
        out_shape=jax.ShapeDtypeStruct((MOD_ROWS, n), F32),
        compiler_params=_cp(("arbitrary",), 40),
        name="ada_ln",
    )(cc, w_ada, b_ada)


NORM_TM = 512
PROJ_TM = 2048
PROJ_TN = 512


def _mod_spec(mod_off, bpb):
    return pl.BlockSpec((None, 1, D_MODEL), lambda i, *_: (mod_off + i // bpb, 0, 0))


def _norm_kernel(x_ref, g_ref, sh_ref, sc_ref, wa_ref, h_ref, a_ref):
    h = _rms(x_ref[...], g_ref[...]) * (1.0 + sc_ref[...]) + sh_ref[...]
    hb = h.astype(BF16)
    h_ref[...] = hb
    a_ref[...] = _dot(hb, wa_ref[...])


def _norm_mod(x, g1, shift, scale, wa, mod_off, rows_per_mod):
    m = x.shape[0]
    bpb = rows_per_mod // NORM_TM
    return pl.pallas_call(
        _norm_kernel,
        grid=(m // NORM_TM,),
        in_specs=[
            pl.BlockSpec((NORM_TM, D_MODEL), lambda i: (i, 0)),
            pl.BlockSpec((1, D_MODEL), lambda i: (0, 0)),
            _mod_spec(mod_off, bpb),
            _mod_spec(mod_off, bpb),
            pl.BlockSpec((D_MODEL, LANES), lambda i: (0, 0)),
        ],
        out_specs=[
            pl.BlockSpec((NORM_TM, D_MODEL), lambda i: (i, 0)),
            pl.BlockSpec((NORM_TM, LANES), lambda i: (i, 0)),
        ],
        out_shape=[
            jax.ShapeDtypeStruct((m, D_MODEL), BF16),
            jax.ShapeDtypeStruct((m, LANES), F32),
        ],
        compiler_params=_cp(("arbitrary",), 40),
        name="norm_modulate",
    )(x, g1, shift, scale, wa)


class _Side(NamedTuple):
    fn: Callable
    ins: tuple
    in_cols: tuple
    outs: tuple
    rows: int
    nblk: int

    def specs(self, nj):
        def spec(shape_cols, cb):
            return pl.BlockSpec((self.rows, shape_cols),
                                lambda i, j: (jnp.minimum(i * nj + j, self.nblk - 1), cb))
        in_specs = [spec(self.outs[0].shape[1], cb) for cb in self.in_cols]
        out_specs = [spec(o.shape[1], 0) for o in self.outs]
        return in_specs, out_specs


def _host_kernel(body, n_in, n_out, side):
    def kern(*refs):
        n_sin = len(side.ins) if side else 0
        if side:
            side.fn(*refs[n_in:n_in + n_sin], *refs[n_in + n_sin + n_out:])
        body(*refs[:n_in], *refs[n_in + n_sin:n_in + n_sin + n_out])
    return kern


def _host_call(body, grid, in_specs, out_spec, out_shape, args, side, vmem_mib, name):
    if side is None:
        s_in, s_out = [], []
    else:
        assert grid[0] * grid[1] >= side.nblk
        s_in, s_out = side.specs(grid[1])
    res = pl.pallas_call(
        _host_kernel(body, len(in_specs), 1, side),
        grid=grid,
        in_specs=in_specs + s_in,
        out_specs=[out_spec] + s_out,
        out_shape=[out_shape] + (list(side.outs) if side else []),
        compiler_params=_cp(("arbitrary", "arbitrary"), vmem_mib),
        name=name,
    )(*args, *(side.ins if side else ()))
    return res[0] if side is None else res


def _cast_kernel(src_ref, dst_ref):
    dst_ref[...] = src_ref[...].astype(BF16)


def _cast_side(w, rows):
    n, d = w.shape
    return _Side(_cast_kernel, (w,), (0,), (jax.ShapeDtypeStruct((n, d), BF16),), rows, n // rows)


def _proj_kernel(h_ref, wt_ref, o_ref):
    o_ref[...] = lax.dot_general(h_ref[...], wt_ref[...].astype(BF16), NT_DIMS,
                                 preferred_element_type=F32).astype(o_ref.dtype)


def _proj(h, wt, col0, ncols, out_dtype, side=None):
    m = h.shape[0]
    j0 = col0 // PROJ_TN
    in_specs = [
        pl.BlockSpec((PROJ_TM, D_MODEL), lambda i, j: (i, 0), pipeline_mode=pl.Buffered(1)),
        pl.BlockSpec((PROJ_TN, D_MODEL), lambda i, j: (j0 + j, 0)),
    ]
    return _host_call(
        _proj_kernel, (m // PROJ_TM, ncols // PROJ_TN), in_specs,
        pl.BlockSpec((PROJ_TM, PROJ_TN), lambda i, j: (i, j)),
        jax.ShapeDtypeStruct((m, ncols), out_dtype), (h, wt), side, 60, "in_proj")


def _sink_col(sink_ref, h, rows_per_head):
    rid = lax.broadcasted_iota(jnp.int32, (GQA_GROUP * rows_per_head, 1), 0) // rows_per_head
    col = jnp.full(rid.shape, sink_ref[h * GQA_GROUP], F32)
    for g in range(1, GQA_GROUP):
        col = jnp.where(rid == g, sink_ref[h * GQA_GROUP + g], col)
    return col


def _ctx_attn_kernel(sink_ref, q_ref, k_ref, v_ref, qg_ref, kg_ref, o_ref, nk_ref, nv_ref):
    scale = HEAD_DIM ** -0.5
    nv_ref[...] = v_ref[...]
    for h in range(N_KV_HEADS):
        hs = slice(h * HEAD_DIM, (h + 1) * HEAD_DIM)
        kn = _rms(k_ref[:, hs], kg_ref[...])
        nk_ref[:, hs] = kn
        kb = kn.astype(BF16)
        vb = v_ref[:, hs].astype(BF16)
        qs = []
        for g in range(GQA_GROUP):
            c0 = (h * GQA_GROUP + g) * HEAD_DIM
            qs.append(_rms(q_ref[:, c0:c0 + HEAD_DIM], qg_ref[...]).astype(BF16))
        q4 = jnp.concatenate(qs, axis=0)
        s = lax.dot_general(q4, kb, NT_DIMS, preferred_element_type=F32) * scale
        sk = _sink_col(sink_ref, h, SEQ)
        m = jnp.maximum(jnp.max(s, axis=1, keepdims=True), sk)
        p = jnp.exp(s - m)
        den = jnp.sum(p, axis=1, keepdims=True) + jnp.exp(sk - m)
        o = _dot(p.astype(BF16), vb) / den
        for g in range(GQA_GROUP):
            c0 = (h * GQA_GROUP + g) * HEAD_DIM
            o_ref[:, c0:c0 + HEAD_DIM] = o[g * SEQ:(g + 1) * SEQ, :].astype(BF16)


def _ctx_attn(proj, sink, qg, kg):
    m = proj.shape[0]
    kvw = N_KV_HEADS * HEAD_DIM
    return pl.pallas_call(
        _ctx_attn_kernel,
        grid=(m // SEQ,),
        in_specs=[
            pl.BlockSpec(memory_space=pltpu.SMEM),
            pl.BlockSpec((SEQ, D_MODEL), lambda b: (b, OFF_Q // D_MODEL)),
            pl.BlockSpec((SEQ, kvw), lambda b: (b, OFF_K // kvw)),
            pl.BlockSpec((SEQ, kvw), lambda b: (b, OFF_V // kvw)),
            pl.BlockSpec((1, HEAD_DIM), lambda b: (0, 0)),
            pl.BlockSpec((1, HEAD_DIM), lambda b: (0, 0)),
        ],
        out_specs=[
            pl.BlockSpec((SEQ, D_MODEL), lambda b: (b, 0)),
            pl.BlockSpec((SEQ, kvw), lambda b: (b, 0)),
            pl.BlockSpec((SEQ, kvw), lambda b: (b, 0)),
        ],
        out_shape=[
            jax.ShapeDtypeStruct((m, D_MODEL), BF16),
            jax.ShapeDtypeStruct((m, kvw), F32),
            jax.ShapeDtypeStruct((m, kvw), F32),
        ],
        compiler_params=_cp(("arbitrary",), 40),
        name="ctx_attention",
    )(sink, proj, proj, proj, qg, kg)


def _rope(x, c_ref, sa_ref, sb_ref):
    up = pltpu.roll(x, HEAD_DIM - 32, axis=1)
    dn = pltpu.roll(x, 32, axis=1)
    return x * c_ref[...] + up * sa_ref[...] + dn * sb_ref[...]


def _lat_attn_kernel(sink_ref, q_ref, k_ref, v_ref, ck_ref, cv_ref, qg_ref, kg_ref,
                     c_ref, sa_ref, sb_ref, o_ref, q_scr, k_scr, v_scr):
    t = DEC_SEQ
    scale = HEAD_DIM ** -0.5
    h = pl.program_id(1)
    zpad = jnp.zeros((BLOCK, HEAD_DIM), BF16)
    k_scr[0:BLOCK, :] = zpad
    k_scr[t + BLOCK:t + 2 * BLOCK, :] = zpad
    v_scr[0:BLOCK, :] = zpad
    v_scr[t + BLOCK:t + 2 * BLOCK, :] = zpad
    kr = _rope(_rms(k_ref[...], kg_ref[...]), c_ref, sa_ref, sb_ref)
    k_scr[BLOCK:t + BLOCK, :] = kr.astype(BF16)
    v_scr[BLOCK:t + BLOCK, :] = v_ref[...].astype(BF16)
    for g in range(GQA_GROUP):
        qn = _rms(q_ref[:, g * HEAD_DIM:(g + 1) * HEAD_DIM], qg_ref[...])
        q_scr[g] = _rope(qn, c_ref, sa_ref, sb_ref).astype(BF16)
    ckb = ck_ref[...].astype(BF16)
    cvb = cv_ref[...].astype(BF16)
    sk = _sink_col(sink_ref, h, BLOCK)
    rows = GQA_GROUP * BLOCK
    qi = lax.broadcasted_iota(jnp.int32, (rows, 3 * BLOCK), 0) % BLOCK
    kj = lax.broadcasted_iota(jnp.int32, (rows, 3 * BLOCK), 1)
    band = (kj >= qi) & (kj <= qi + 2 * WINDOW)
    neg = jnp.finfo(F32).min

    def body(n, carry):
        r0 = pl.multiple_of(n * BLOCK, BLOCK)
        q4 = jnp.concatenate([q_scr[g, pl.ds(r0, BLOCK), :] for g in range(GQA_GROUP)], axis=0)
        kw = k_scr[pl.ds(r0, 3 * BLOCK), :]
        vw = v_scr[pl.ds(r0, 3 * BLOCK), :]
        kjg = kj + (n - 1) * BLOCK
        valid = band & (kjg >= 0) & (kjg < t)
        s_win = lax.dot_general(q4, kw, NT_DIMS, preferred_element_type=F32) * scale
        s_win = jnp.where(valid, s_win, neg)
        s_ctx = lax.dot_general(q4, ckb, NT_DIMS, preferred_element_type=F32) * scale
        m = jnp.maximum(jnp.max(s_win, axis=1, keepdims=True),
                        jnp.max(s_ctx, axis=1, keepdims=True))
        m = jnp.maximum(m, sk)
        p_win = jnp.exp(s_win - m)
        p_ctx = jnp.exp(s_ctx - m)
        den = (jnp.sum(p_win, axis=1, keepdims=True) + jnp.sum(p_ctx, axis=1, keepdims=True)
               + jnp.exp(sk - m))
        o = (_dot(p_win.astype(BF16), vw) + _dot(p_ctx.astype(BF16), cvb)) / den
        for g in range(GQA_GROUP):
            o_ref[pl.ds(r0, BLOCK), g * HEAD_DIM:(g + 1) * HEAD_DIM] = (
                o[g * BLOCK:(g + 1) * BLOCK, :].astype(BF16))
        return carry

    lax.fori_loop(0, t // BLOCK, body, 0, unroll=2)


def _lat_attn(proj, ck, cv, sink, qg, kg, rope_c, rope_sa, rope_sb):
    m = proj.shape[0]
    t = DEC_SEQ
    gw = GQA_GROUP * HEAD_DIM
    tab = pl.BlockSpec((t, HEAD_DIM), lambda b, h: (0, 0))
    vec = pl.BlockSpec((1, HEAD_DIM), lambda b, h: (0, 0))
    cache = pl.BlockSpec((None, PAST_LEN, HEAD_DIM), lambda b, h: (b, 0, h))
    return pl.pallas_call(
        _lat_attn_kernel,
        grid=(m // t, N_KV_HEADS),
        in_specs=[
            pl.BlockSpec(memory_space=pltpu.SMEM),
            pl.BlockSpec((t, gw), lambda b, h: (b, OFF_Q // gw + h)),
            pl.BlockSpec((t, HEAD_DIM), lambda b, h: (b, OFF_K // HEAD_DIM + h)),
            pl.BlockSpec((t, HEAD_DIM), lambda b, h: (b, OFF_V // HEAD_DIM + h)),
            cache, cache, vec, vec, tab, tab, tab,
        ],
        out_specs=pl.BlockSpec((t, gw), lambda b, h: (b, h)),
        out_shape=jax.ShapeDtypeStruct((m, D_MODEL), BF16),
        scratch_shapes=[
            pltpu.VMEM((GQA_GROUP, t, HEAD_DIM), BF16),
            pltpu.VMEM((t + 2 * BLOCK, HEAD_DIM), BF16),
            pltpu.VMEM((t + 2 * BLOCK, HEAD_DIM), BF16),
        ],
        compiler_params=_cp(("arbitrary", "arbitrary"), 40),
        name="latent_attention",
    )(sink, proj, proj, proj, ck, cv, qg, kg, rope_c, rope_sa, rope_sb)


def _log_sigmoid(x):
    return jnp.minimum(x, 0.0) - jnp.log(1.0 + jnp.exp(-jnp.abs(x)))


def _split2(x):
    hi = x.astype(BF16)
    lo = (x - hi.astype(F32)).astype(BF16)
    return hi, lo


def _gla_kernel(*refs, t, has_state):
    if has_state:
        (q_ref, k_ref, v_ref, a_ref, waf_ref, wab_ref, baf_ref, bab_ref, gn_ref,
         s0f_ref, s0b_ref, og_ref, *scr) = refs
    else:
        (q_ref, k_ref, v_ref, a_ref, waf_ref, wab_ref, baf_ref, bab_ref, gn_ref,
         og_ref, sf_out, sb_out, *scr) = refs
    stf, stb, vb, ob = scr[:4]
    qsf, kuf, decf, *tmpf = scr[4:13]
    qsb, kub, decb, *tmpb = scr[13:]
    c = GLA_CHUNK
    sc = GLA_SUPER
    nsub = sc // c
    nsc = t // sc
    vb[...] = v_ref[...].astype(BF16)

    row = lax.broadcasted_iota(jnp.int32, (sc, sc), 0)
    col = lax.broadcasted_iota(jnp.int32, (sc, sc), 1)
    rc, cc = row // c, col // c

    def intra(si, fwd):
        r0 = pl.multiple_of(si * sc, sc)
        w_ref, b_ref, qs_scr, ku_scr, dec_scr, tmp, dst = (
            (waf_ref, baf_ref, qsf, kuf, decf, tmpf, og_ref) if fwd
            else (wab_ref, bab_ref, qsb, kub, decb, tmpb, ob))
        qd, kd, ke, q2, q3, amat = tmp
        tri = jnp.where((col <= row) if fwd else (col >= row), 1.0, 0.0).astype(BF16)
        dist = (rc - cc) if fwd else (cc - rc)
        x = _dot(a_ref[pl.ds(r0, sc), :].astype(BF16), w_ref[...]) + b_ref[...]
        hi, lo = _split2(_log_sigmoid(x) / GLA_TAU)
        cum = _dot(tri, hi) + _dot(tri, lo)
        zero = jnp.zeros((1, GLA_HK), F32)

        def at_start(j):
            if fwd:
                return cum[j * c - 1:j * c, :] if j > 0 else zero
            return cum[(j + 1) * c:(j + 1) * c + 1, :] if j < nsub - 1 else zero

        def at_end(j):
            return cum[(j + 1) * c - 1:(j + 1) * c, :] if fwd else cum[j * c:j * c + 1, :]

        total = at_end(nsub - 1) if fwd else at_end(0)
        dec_scr[pl.ds(pl.multiple_of(si * 8, 8), 8), :] = jnp.broadcast_to(jnp.exp(total), (8, GLA_HK))
        for j in range(nsub):
            rs = slice(j * c, (j + 1) * c)
            rows = pl.ds(pl.multiple_of(r0 + j * c, c), c)
            cj = cum[rs, :]
            cs, ce = at_start(j), at_end(j)
            q = q_ref[rows, :] * (GLA_HK ** -0.5)
            k = k_ref[rows, :]
            qdj = q * jnp.exp(cj - cs)
            kej = k * jnp.exp(ce - cj)
            qd[rs, :] = qdj.astype(BF16)
            kd[rs, :] = (k * jnp.exp(cs - cj)).astype(BF16)
            ke[rs, :] = kej.astype(BF16)
            qs_scr[rows, :] = (qdj * jnp.exp(cs)).astype(BF16)
            ku_scr[rows, :] = (kej * jnp.exp(total - ce)).astype(BF16)
            p2 = j - 2 if fwd else j + 2
            if 0 <= p2 < nsub:
                l2 = (j - 2) if fwd else j
                q2[l2 * c:(l2 + 1) * c, :] = (qdj * jnp.exp(cs - at_end(p2))).astype(BF16)
            p3 = j - 3 if fwd else j + 3
            if 0 <= p3 < nsub:
                q3[...] = (qdj * jnp.exp(cs - at_end(p3))).astype(BF16)
        nt = lambda a, b: lax.dot_general(a, b, NT_DIMS, preferred_element_type=F32)
        tril = (col <= row) if fwd else (col >= row)
        amat[...] = (jnp.where((dist == 0) & tril, nt(qd[...], kd[...]), 0.0)
                     + jnp.where(dist == 1, nt(qd[...], ke[...]), 0.0))
        r2 = slice(2 * c, sc) if fwd else slice(0, 2 * c)
        amat[r2, :] += jnp.where(dist[r2, :] == 2, nt(q2[...], ke[...]), 0.0)
        r3 = slice(3 * c, sc) if fwd else slice(0, c)
        amat[r3, :] += jnp.where(dist[r3, :] == 3, nt(q3[...], ke[...]), 0.0)
        dst[pl.ds(r0, sc), :] = _dot(amat[...].astype(BF16), vb[pl.ds(r0, sc), :])

    def intra_body(i, carry):
        intra(i, True)
        intra(nsc - 1 - i, False)
        return carry

    lax.fori_loop(0, nsc, intra_body, 0, unroll=min(nsc, 2))

    if has_state:
        for i in range(nsc):
            for fwd in (True, False):
                si = i if fwd else nsc - 1 - i
                rows = slice(si * sc, (si + 1) * sc)
                st_ref, s0_ref, qs_scr, ku_scr, dec_scr, dst = (
                    (stf, s0f_ref, qsf, kuf, decf, og_ref) if fwd
                    else (stb, s0b_ref, qsb, kub, decb, ob))
                st = s0_ref[...] if i == 0 else st_ref[...]
                dst[rows, :] += _dot(qs_scr[rows, :], st.astype(BF16))
                if i < nsc - 1:
                    dec = jnp.broadcast_to(dec_scr[si * 8:si * 8 + 1, :], (LANES, GLA_HK)).T
                    dec = jnp.concatenate([dec] * (GLA_HV // LANES), axis=1)
                    st_ref[...] = st * dec + lax.dot_general(
                        ku_scr[rows, :], vb[rows, :], TN_DIMS, preferred_element_type=F32)
    else:
        sf_out[...] = lax.dot_general(kuf[...], vb[...], TN_DIMS, preferred_element_type=F32)
        sb_out[...] = lax.dot_general(kub[...], vb[...], TN_DIMS, preferred_element_type=F32)
    og_ref[...] = _rms(og_ref[...] + ob[...], gn_ref[...])


def _gla(proj, aproj, waf, wab, baf, bab, gn, t, s0f=None, s0b=None):
    m = proj.shape[0]
    nb = m // t
    has_state = s0f is not None
    assert t % GLA_SUPER == 0 and (has_state or t == GLA_SUPER)
    sc, c = GLA_SUPER, GLA_CHUNK
    st_shape_vmem = (GLA_HK, GLA_HV) if has_state else (8, LANES)
    per_direction = [
        pltpu.VMEM((t, GLA_HK), BF16),
        pltpu.VMEM((t, GLA_HK), BF16),
        pltpu.VMEM((8 * (t // sc), GLA_HK), F32),
        pltpu.VMEM((sc, GLA_HK), BF16),
        pltpu.VMEM((sc, GLA_HK), BF16),
        pltpu.VMEM((sc, GLA_HK), BF16),
        pltpu.VMEM((sc - 2 * c, GLA_HK), BF16),
        pltpu.VMEM((sc - 3 * c, GLA_HK), BF16),
        pltpu.VMEM((sc, sc), F32),
    ]
    st_spec = pl.BlockSpec((None, GLA_HK, GLA_HV), lambda b, h: (b * GLA_HEADS + h, 0, 0))
    in_specs = [
        pl.BlockSpec((t, GLA_HK), lambda b, h: (b, OFF_GQ // GLA_HK + h)),
        pl.BlockSpec((t, GLA_HK), lambda b, h: (b, OFF_GK // GLA_HK + h)),
        pl.BlockSpec((t, GLA_HV), lambda b, h: (b, OFF_GV // GLA_HV + h)),
        pl.BlockSpec((t, LANES), lambda b, h: (b, 0)),
        pl.BlockSpec((LANES, GLA_HK), lambda b, h: (0, h)),
        pl.BlockSpec((LANES, GLA_HK), lambda b, h: (0, h)),
        pl.BlockSpec((1, GLA_HK), lambda b, h: (0, h)),
        pl.BlockSpec((1, GLA_HK), lambda b, h: (0, h)),
        pl.BlockSpec((1, GLA_HV), lambda b, h: (0, h)),
    ]
    args = [proj, proj, proj, aproj, waf, wab, baf, bab, gn]
    og_spec = pl.BlockSpec((t, GLA_HV), lambda b, h: (b, h))
    og_shape = jax.ShapeDtypeStruct((m, GLA_DV), F32)
    if has_state:
        in_specs += [st_spec, st_spec]
        args += [s0f, s0b]
        out_specs, out_shape = og_spec, og_shape
    else:
        st_shape = jax.ShapeDtypeStruct((nb * GLA_HEADS, GLA_HK, GLA_HV), F32)
        out_specs, out_shape = [og_spec, st_spec, st_spec], [og_shape, st_shape, st_shape]
    return pl.pallas_call(
        functools.partial(_gla_kernel, t=t, has_state=has_state),
        grid=(nb, GLA_HEADS),
        in_specs=in_specs,
        out_specs=out_specs,
        out_shape=out_shape,
        scratch_shapes=[
            pltpu.VMEM(st_shape_vmem, F32),
            pltpu.VMEM(st_shape_vmem, F32),
            pltpu.VMEM((t, GLA_HV), BF16),
            pltpu.VMEM((t, GLA_HV), F32),
        ] + 2 * per_direction,
        compiler_params=_cp(("arbitrary", "arbitrary"), 56),
        name="gla_state" if has_state else "gla_zero",
    )(*args)


MERGE_SLAB = (16, 1024)


def _merge_kernel(oa_ref, og_ref, gr_ref, ga_ref, gg_ref, o_ref):
    rows, cols = o_ref.shape
    for r0 in range(0, rows, MERGE_SLAB[0]):
        for c0 in range(0, cols, MERGE_SLAB[1]):
            sl = (slice(r0, r0 + MERGE_SLAB[0]), slice(c0, c0 + MERGE_SLAB[1]))
            gr = gr_ref[sl].astype(F32)
            o_gla = og_ref[sl] * (gr * _sigmoid(gr))
            merged = (_sigmoid(ga_ref[sl].astype(F32)) * oa_ref[sl].astype(F32)
                      + _sigmoid(gg_ref[sl].astype(F32)) * o_gla)
            o_ref[sl] = merged.astype(BF16)


MERGE_TM = 256


def _merge(o_att, og, gates):
    m = o_att.shape[0]
    blk = lambda cb: pl.BlockSpec((MERGE_TM, D_MODEL), lambda i: (i, cb))
    cols = ((OFF_GR - OFF_GR) // D_MODEL, (OFF_GATT - OFF_GR) // D_MODEL, (OFF_GGLA - OFF_GR) // D_MODEL)
    return pl.pallas_call(
        _merge_kernel,
        grid=(m // MERGE_TM,),
        in_specs=[blk(0), blk(0)] + [blk(cb) for cb in cols],
        out_specs=blk(0),
        out_shape=jax.ShapeDtypeStruct((m, D_MODEL), BF16),
        compiler_params=_cp(("arbitrary",), 40),
        name="branch_merge",
    )(o_att, og, gates, gates, gates)


def _merge_side(o_att, og, gates, rows):
    m = o_att.shape[0]
    cols = ((OFF_GR - OFF_GR) // D_MODEL, (OFF_GATT - OFF_GR) // D_MODEL, (OFF_GGLA - OFF_GR) // D_MODEL)
    return _Side(_merge_kernel, (o_att, og, gates, gates, gates), (0, 0) + cols,
                 (jax.ShapeDtypeStruct((m, D_MODEL), BF16),), rows, m // rows)


OUT_TM = 512
OUT_TN = 1024


def _out_kernel(mg_ref, w_ref, x_ref, gt_ref, g2_ref, sh_ref, sc_ref, x1_ref, h2_ref, x1_scr):
    j = pl.program_id(1)
    nj = D_MODEL // OUT_TN
    x1 = x_ref[...] + gt_ref[...] * _dot(mg_ref[...], w_ref[...])
    x1_ref[...] = x1
    x1_scr[j] = x1

    @pl.when(j == nj - 1)
    def _():
        ssq = jnp.zeros((OUT_TM, 1), F32)
        for jj in range(nj):
            xs = x1_scr[jj]
            ssq = ssq + jnp.sum(xs * xs, axis=-1, keepdims=True)
        inv = lax.rsqrt(ssq / D_MODEL + EPS)
        for jj in range(nj):
            cs = slice(jj * OUT_TN, (jj + 1) * OUT_TN)
            y = x1_scr[jj] * inv * g2_ref[:, cs]
            h2_ref[:, cs] = (y * (1.0 + sc_ref[:, cs]) + sh_ref[:, cs]).astype(BF16)


def _out_proj(merged, w_out, x, gate1, g2, shift2, scale2, mod_off, rows_per_mod):
    m = x.shape[0]
    bpb = rows_per_mod // OUT_TM
    gate_spec = pl.BlockSpec((None, 1, OUT_TN), lambda i, j: (mod_off + i // bpb, 0, j))
    return pl.pallas_call(
        _out_kernel,
        grid=(m // OUT_TM, D_MODEL // OUT_TN),
        in_specs=[
            pl.BlockSpec((OUT_TM, D_MODEL), lambda i, j: (i, 0)),
            pl.BlockSpec((D_MODEL, OUT_TN), lambda i, j: (0, j)),
            pl.BlockSpec((OUT_TM, OUT_TN), lambda i, j: (i, j)),
            gate_spec,
            pl.BlockSpec((1, D_MODEL), lambda i, j: (0, 0)),
            _mod_spec(mod_off, bpb),
            _mod_spec(mod_off, bpb),
        ],
        out_specs=[
            pl.BlockSpec((OUT_TM, OUT_TN), lambda i, j: (i, j)),
            pl.BlockSpec((OUT_TM, D_MODEL), lambda i, j: (i, 0)),
        ],
        out_shape=[
            jax.ShapeDtypeStruct((m, D_MODEL), F32),
            jax.ShapeDtypeStruct((m, D_MODEL), BF16),
        ],
        scratch_shapes=[pltpu.VMEM((D_MODEL // OUT_TN, OUT_TM, OUT_TN), F32)],
        compiler_params=_cp(("arbitrary", "arbitrary"), 56),
        name="out_proj_residual_norm",
    )(merged, w_out, x, gate1, g2, shift2, scale2)


FFI_TM = 2048
FFI_TN = 256


def _ffn_in_kernel(h_ref, wg_ref, wu_ref, o_ref):
    h = h_ref[...]
    g = _dot(h, wg_ref[...].astype(BF16))
    u = _dot(h, wu_ref[...].astype(BF16))
    o_ref[...] = (g * _sigmoid(g) * u).astype(BF16)


def _ffn_in(h2, w):
    m = h2.shape[0]
    return pl.pallas_call(
        _ffn_in_kernel,
        grid=(m // FFI_TM, D_FF // FFI_TN),
        in_specs=[
            pl.BlockSpec((FFI_TM, D_MODEL), lambda i, j: (i, 0)),
            pl.BlockSpec((D_MODEL, FFI_TN), lambda i, j: (0, j)),
            pl.BlockSpec((D_MODEL, FFI_TN), lambda i, j: (0, D_FF // FFI_TN + j)),
        ],
        out_specs=pl.BlockSpec((FFI_TM, FFI_TN), lambda i, j: (i, j)),
        out_shape=jax.ShapeDtypeStruct((m, D_FF), BF16),
        compiler_params=_cp(("arbitrary", "arbitrary"), 58),
        name="ffn_in_swiglu",
    )(h2, w, w)


FFO_TM = 512
FFO_TN = 512


def _ffn_out_kernel(a_ref, w_ref, x_ref, gt_ref, o_ref):
    o_ref[...] = x_ref[...] + gt_ref[...] * _dot(a_ref[...], w_ref[...])


def _ffn_out(act, w, x1, gate2, mod_off, rows_per_mod, side=None):
    m = x1.shape[0]
    bpb = rows_per_mod // FFO_TM
    in_specs = [
        pl.BlockSpec((FFO_TM, D_FF), lambda i, j: (i, 0)),
        pl.BlockSpec((D_FF, FFO_TN), lambda i, j: (0, j)),
        pl.BlockSpec((FFO_TM, FFO_TN), lambda i, j: (i, j)),
        pl.BlockSpec((None, 1, FFO_TN), lambda i, j: (mod_off + i // bpb, 0, j)),
    ]
    return _host_call(
        _ffn_out_kernel, (m // FFO_TM, D_MODEL // FFO_TN), in_specs,
        pl.BlockSpec((FFO_TM, FFO_TN), lambda i, j: (i, j)),
        jax.ShapeDtypeStruct((m, D_MODEL), F32), (act, w, x1, gate2), side, 60, "ffn_out_residual")


def _rope_tables(t):
    rows = t // GRID_W
    half = HEAD_DIM // 2
    row = jnp.repeat(jnp.arange(rows, dtype=F32), GRID_W)
    col = jnp.tile(jnp.arange(GRID_W, dtype=F32), rows)
    inv = ROPE_THETA ** (-jnp.arange(0, half, 2, dtype=F32) / half)
    ar = row[:, None] * inv[None, :]
    ac = col[:, None] * inv[None, :]
    cr, sr, cc, sc = jnp.cos(ar), jnp.sin(ar), jnp.cos(ac), jnp.sin(ac)
    z = jnp.zeros_like(sr)
    tab_c = jnp.concatenate([cr, cr, cc, cc], axis=1)
    tab_sa = jnp.concatenate([-sr, z, -sc, z], axis=1)
    tab_sb = jnp.concatenate([z, sr, z, sc], axis=1)
    return tab_c, tab_sa, tab_sb


def kernel(x_prompt, x_sample, c, cache_k, cache_v, state_gla_fwd, state_gla_bwd, c_ctx, w_ada, b_ada, norm1_g, norm2_g, w_in, q_norm_g, k_norm_g, attn_sink, w_a2_fwd, b_a_fwd, w_a2_bwd, b_a_bwd, gla_norm_g, w_out, w_ffn_in, w_ffn_out):
    assert w_ada.shape[0] == 1, "single trunk layer"
    cc = jnp.zeros((MOD_ROWS, D_MODEL), F32).at[0].set(c_ctx).at[1:1 + DEC_BATCH].set(c)
    mod_all = _ada(cc, w_ada[0], b_ada[0][None, :])
    mod = tuple(mod_all[:, i * D_MODEL:(i + 1) * D_MODEL].reshape(MOD_ROWS, 1, D_MODEL)
                for i in range(N_MOD))

    r = GLA_GATE_RANK
    wa = jnp.zeros((D_MODEL, LANES), BF16).at[:, :2 * r].set(w_in[0][:, D_WIDE:].astype(BF16))
    waf = jnp.zeros((LANES, GLA_DK), BF16).at[:r].set(w_a2_fwd[0].astype(BF16))
    wab = jnp.zeros((LANES, GLA_DK), BF16).at[r:2 * r].set(w_a2_bwd[0].astype(BF16))
    g1, g2 = norm1_g[0][None, :], norm2_g[0][None, :]
    w_in_t = w_in[0].T
    qg, kg, sink = q_norm_g[0][None, :], k_norm_g[0][None, :], attn_sink[0]
    gla_w = (waf, wab, b_a_fwd[0][None, :], b_a_bwd[0][None, :], gla_norm_g[0][None, :])
    shift1, scale1, gate1, shift2, scale2, gate2 = mod
    kvw = N_KV_HEADS * HEAD_DIM
    n_wide, n_gate = OFF_GR, D_WIDE - OFF_GR
    m_ctx, m_lat = BATCH * SEQ, DEC_BATCH * DEC_SEQ
    ctx_mod, lat_mod = (0, m_ctx), (1, DEC_SEQ)

    xp = x_prompt.reshape(m_ctx, D_MODEL)
    xs = x_sample.reshape(m_lat, D_MODEL)

    h_c, a_c = _norm_mod(xp, g1, shift1, scale1, wa, *ctx_mod)
    proj_c, w_fo = _proj(h_c, w_in_t, 0, n_wide, F32, _cast_side(w_ffn_out[0], 128))
    gates_c, w_o = _proj(h_c, w_in_t, n_wide, n_gate, BF16, _cast_side(w_out[0], 64))
    oatt_c, new_k, new_v = _ctx_attn(proj_c, sink, qg, kg)
    og_c, s_f, s_b = _gla(proj_c, a_c, *gla_w, SEQ)

    h_l, a_l = _norm_mod(xs, g1, shift1, scale1, wa, *lat_mod)
    merged_c = _merge(oatt_c, og_c, gates_c)
    proj_l = _proj(h_l, w_in_t, 0, n_wide, F32)
    gates_l = _proj(h_l, w_in_t, n_wide, n_gate, BF16)
    oatt_l = _lat_attn(
        proj_l,
        cache_k[:, 0].reshape(DEC_BATCH, PAST_LEN, kvw),
        cache_v[:, 0].reshape(DEC_BATCH, PAST_LEN, kvw),
        sink, qg, kg, *_rope_tables(DEC_SEQ))
    og_l = _gla(proj_l, a_l, *gla_w, DEC_SEQ,
                state_gla_fwd[:, 0].reshape(DEC_BATCH * GLA_HEADS, GLA_HK, GLA_HV),
                state_gla_bwd[:, 0].reshape(DEC_BATCH * GLA_HEADS, GLA_HK, GLA_HV))

    x1_c, h2_c = _out_proj(merged_c, w_o, xp, gate1, g2, shift2, scale2, *ctx_mod)
    act_c = _ffn_in(h2_c, w_ffn_in[0])
    yp, merged_l = _ffn_out(act_c, w_fo, x1_c, gate2, *ctx_mod,
                            side=_merge_side(oatt_l, og_l, gates_l, 64))

    x1_l, h2_l = _out_proj(merged_l, w_o, xs, gate1, g2, shift2, scale2, *lat_mod)
    act_l = _ffn_in(h2_l, w_ffn_in[0])
    ys = _ffn_out(act_l, w_fo, x1_l, gate2, *lat_mod)
    return (
        yp.reshape(BATCH, SEQ, D_MODEL),
        ys.reshape(DEC_BATCH, DEC_SEQ, D_MODEL),
        new_k.reshape(BATCH, 1, SEQ, N_KV_HEADS, HEAD_DIM),
        new_v.reshape(BATCH, 1, SEQ, N_KV_HEADS, HEAD_DIM),
        s_f.reshape(BATCH, 1, GLA_HEADS, GLA_HK, GLA_HV),
        s_b.reshape(BATCH, 1, GLA_HEADS, GLA_HK, GLA_HV),
    )
```

```python
import functools
from typing import Callable, NamedTuple

import jax
import jax.numpy as jnp
from jax import lax
from jax.experimental import pallas as pl
from jax.experimental.pallas import tpu as pltpu

F32 = jnp.float32
BF16 = jnp.bfloat16

D_MODEL = 4096
BATCH = 32
SEQ = 256
DEC_BATCH = 8
DEC_SEQ = 1024
PAST_LEN = 256
GRID_W = 64
HEAD_DIM = 128
N_Q_HEADS = 32
N_KV_HEADS = 8
GQA_GROUP = N_Q_HEADS // N_KV_HEADS
WINDOW = 128
BLOCK = 128
ROPE_THETA = 10000.0
GLA_HEADS = 4
GLA_DK = D_MODEL // 2
GLA_DV = D_MODEL
GLA_HK = GLA_DK // GLA_HEADS
GLA_HV = GLA_DV // GLA_HEADS
GLA_GATE_RANK = 16
GLA_TAU = 16.0
GLA_CHUNK = 64
GLA_SUPER = 256
D_FF = 11008
N_MOD = 6
EPS = 1e-6

OFF_Q = 0
OFF_K = 4096
OFF_V = 5120
OFF_GQ = 6144
OFF_GK = 8192
OFF_GV = 10240
OFF_GR = 14336
OFF_GATT = 18432
OFF_GGLA = 22528
D_WIDE = 26624
LANES = 128
MOD_ROWS = 16

MIB = 1024 * 1024
NT_DIMS = (((1,), (1,)), ((), ()))
TN_DIMS = (((0,), (0,)), ((), ()))


def _cp(sem, vmem_mib):
    return pltpu.CompilerParams(dimension_semantics=sem, vmem_limit_bytes=vmem_mib * MIB)


def _rms(x, g):
    ms = jnp.mean(x * x, axis=-1, keepdims=True)
    return x * lax.rsqrt(ms + EPS) * g


def _sigmoid(x):
    return 0.5 * jnp.tanh(0.5 * x) + 0.5


def _dot(a, b):
    return jnp.dot(a, b, preferred_element_type=F32)


ADA_TN = 512


def _ada_kernel(c_ref, w_ref, b_ref, o_ref):
    c = c_ref[...]
    s = (c * _sigmoid(c)).astype(BF16)
    o_ref[...] = _dot(s, w_ref[...].astype(BF16)) + b_ref[...]


def _ada(cc, w_ada, b_ada):
    n = w_ada.shape[1]
    return pl.pallas_call(
        _ada_kernel,
        grid=(n // ADA_TN,),
        in_specs=[
            pl.BlockSpec((MOD_ROWS, D_MODEL), lambda j: (0, 0)),
            pl.BlockSpec((D_MODEL, ADA_TN), lambda j: (0, j)),
            pl.BlockSpec((1, ADA_TN), lambda j: (0, j)),
        ],
        out_specs=pl.BlockSpec((MOD_ROWS, ADA_TN), lambda j: (0, j)),
        out_shape=jax.ShapeDtypeStruct((MOD_ROWS, n), F32),
        compiler_params=_cp(("arbitrary",), 40),
        name="ada_ln",
    )(cc, w_ada, b_ada)


NORM_TM = 512
PROJ_TM = 2048
PROJ_TN = 512


def _mod_spec(mod_off, bpb):
    return pl.BlockSpec((None, 1, D_MODEL), lambda i, *_: (mod_off + i // bpb, 0, 0))


def _norm_kernel(x_ref, g_ref, sh_ref, sc_ref, wa_ref, h_ref, a_ref):
    h = _rms(x_ref[...], g_ref[...]) * (1.0 + sc_ref[...]) + sh_ref[...]
    hb = h.astype(BF16)
    h_ref[...] = hb
    a_ref[...] = _dot(hb, wa_ref[...])


def _norm_mod(x, g1, shift, scale, wa, mod_off, rows_per_mod):
    m = x.shape[0]
    bpb = rows_per_mod // NORM_TM
    return pl.pallas_call(
        _norm_kernel,
        grid=(m // NORM_TM,),
        in_specs=[
            pl.BlockSpec((NORM_TM, D_MODEL), lambda i: (i, 0)),
            pl.BlockSpec((1, D_MODEL), lambda i: (0, 0)),
            _mod_spec(mod_off, bpb),
            _mod_spec(mod_off, bpb),
            pl.BlockSpec((D_MODEL, LANES), lambda i: (0, 0)),
        ],
        out_specs=[
            pl.BlockSpec((NORM_TM, D_MODEL), lambda i: (i, 0)),
            pl.BlockSpec((NORM_TM, LANES), lambda i: (i, 0)),
        ],
        out_shape=[
            jax.ShapeDtypeStruct((m, D_MODEL), BF16),
            jax.ShapeDtypeStruct((m, LANES), F32),
        ],
        compiler_params=_cp(("arbitrary",), 40),
        name="norm_modulate",
    )(x, g1, shift, scale, wa)


class _Side(NamedTuple):
    fn: Callable
    ins: tuple
    in_cols: tuple
    outs: tuple
    rows: int
    nblk: int

    def specs(self, nj):
        def spec(shape_cols, cb):
            return pl.BlockSpec((self.rows, shape_cols),
                                lambda i, j: (jnp.minimum(i * nj + j, self.nblk - 1), cb))
        in_specs = [spec(self.outs[0].shape[1], cb) for cb in self.in_cols]
        out_specs = [spec(o.shape[1], 0) for o in self.outs]
        return in_specs, out_specs


def _host_kernel(body, n_in, n_out, side):
    def kern(*refs):
        n_sin = len(side.ins) if side else 0
        if side:
            side.fn(*refs[n_in:n_in + n_sin], *refs[n_in + n_sin + n_out:])
        body(*refs[:n_in], *refs[n_in + n_sin:n_in + n_sin + n_out])
    return kern


def _host_call(body, grid, in_specs, out_spec, out_shape, args, side, vmem_mib, name):
    if side is None:
        s_in, s_out = [], []
    else:
        assert grid[0] * grid[1] >= side.nblk
        s_in, s_out = side.specs(grid[1])
    res = pl.pallas_call(
        _host_kernel(body, len(in_specs), 1, side),
        grid=grid,
        in_specs=in_specs + s_in,
        out_specs=[out_spec] + s_out,
        out_shape=[out_shape] + (list(side.outs) if side else []),
        compiler_params=_cp(("arbitrary", "arbitrary"), vmem_mib),
        name=name,
    )(*args, *(side.ins if side else ()))
    return res[0] if side is None else res


def _cast_kernel(src_ref, dst_ref):
    dst_ref[...] = src_ref[...].astype(BF16)


def _cast_side(w, rows):
    n, d = w.shape
    return _Side(_cast_kernel, (w,), (0,), (jax.ShapeDtypeStruct((n, d), BF16),), rows, n // rows)


def _proj_kernel(h_ref, wt_ref, o_ref):
    o_ref[...] = lax.dot_general(h_ref[...], wt_ref[...].astype(BF16), NT_DIMS,
                                 preferred_element_type=F32).astype(o_ref.dtype)


def _proj(h, wt, col0, ncols, out_dtype, side=None):
    m = h.shape[0]
    j0 = col0 // PROJ_TN
    in_specs = [
        pl.BlockSpec((PROJ_TM, D_MODEL), lambda i, j: (i, 0), pipeline_mode=pl.Buffered(1)),
        pl.BlockSpec((PROJ_TN, D_MODEL), lambda i, j: (j0 + j, 0)),
    ]
    return _host_call(
        _proj_kernel, (m // PROJ_TM, ncols // PROJ_TN), in_specs,
        pl.BlockSpec((PROJ_TM, PROJ_TN), lambda i, j: (i, j)),
        jax.ShapeDtypeStruct((m, ncols), out_dtype), (h, wt), side, 60, "in_proj")


def _sink_col(sink_ref, h, rows_per_head):
    rid = lax.broadcasted_iota(jnp.int32, (GQA_GROUP * rows_per_head, 1), 0) // rows_per_head
    col = jnp.full(rid.shape, sink_ref[h * GQA_GROUP], F32)
    for g in range(1, GQA_GROUP):
        col = jnp.where(rid == g, sink_ref[h * GQA_GROUP + g], col)
    return col


def _ctx_attn_kernel(sink_ref, q_ref, k_ref, v_ref, qg_ref, kg_ref, o_ref, nk_ref, nv_ref):
    scale = HEAD_DIM ** -0.5
    nv_ref[...] = v_ref[...]
    for h in range(N_KV_HEADS):
        hs = slice(h * HEAD_DIM, (h + 1) * HEAD_DIM)
        kn = _rms(k_ref[:, hs], kg_ref[...])
        nk_ref[:, hs] = kn
        kb = kn.astype(BF16)
        vb = v_ref[:, hs].astype(BF16)
        qs = []
        for g in range(GQA_GROUP):
            c0 = (h * GQA_GROUP + g) * HEAD_DIM
            qs.append(_rms(q_ref[:, c0:c0 + HEAD_DIM], qg_ref[...]).astype(BF16))
        q4 = jnp.concatenate(qs, axis=0)
        s = lax.dot_general(q4, kb, NT_DIMS, preferred_element_type=F32) * scale
        sk = _sink_col(sink_ref, h, SEQ)
        m = jnp.maximum(jnp.max(s, axis=1, keepdims=True), sk)
        p = jnp.exp(s - m)
        den = jnp.sum(p, axis=1, keepdims=True) + jnp.exp(sk - m)
        o = _dot(p.astype(BF16), vb) / den
        for g in range(GQA_GROUP):
            c0 = (h * GQA_GROUP + g) * HEAD_DIM
            o_ref[:, c0:c0 + HEAD_DIM] = o[g * SEQ:(g + 1) * SEQ, :].astype(BF16)


def _ctx_attn(proj, sink, qg, kg):
    m = proj.shape[0]
    kvw = N_KV_HEADS * HEAD_DIM
    return pl.pallas_call(
        _ctx_attn_kernel,
        grid=(m // SEQ,),
        in_specs=[
            pl.BlockSpec(memory_space=pltpu.SMEM),
            pl.BlockSpec((SEQ, D_MODEL), lambda b: (b, OFF_Q // D_MODEL)),
            pl.BlockSpec((SEQ, kvw), lambda b: (b, OFF_K // kvw)),
            pl.BlockSpec((SEQ, kvw), lambda b: (b, OFF_V // kvw)),
            pl.BlockSpec((1, HEAD_DIM), lambda b: (0, 0)),
            pl.BlockSpec((1, HEAD_DIM), lambda b: (0, 0)),
        ],
        out_specs=[
            pl.BlockSpec((SEQ, D_MODEL), lambda b: (b, 0)),
            pl.BlockSpec((SEQ, kvw), lambda b: (b, 0)),
            pl.BlockSpec((SEQ, kvw), lambda b: (b, 0)),
        ],
        out_shape=[
            jax.ShapeDtypeStruct((m, D_MODEL), BF16),
            jax.ShapeDtypeStruct((m, kvw), F32),
            jax.ShapeDtypeStruct((m, kvw), F32),
        ],
        compiler_params=_cp(("arbitrary",), 40),
        name="ctx_attention",
    )(sink, proj, proj, proj, qg, kg)


def _rope(x, c_ref, sa_ref, sb_ref):
    up = pltpu.roll(x, HEAD_DIM - 32, axis=1)
    dn = pltpu.roll(x, 32, axis=1)
    return x * c_ref[...] + up * sa_ref[...] + dn * sb_ref[...]


def _lat_attn_kernel(sink_ref, q_ref, k_ref, v_ref, ck_ref, cv_ref, qg_ref, kg_ref,
                     c_ref, sa_ref, sb_ref, o_ref, q_scr, k_scr, v_scr):
    t = DEC_SEQ
    scale = HEAD_DIM ** -0.5
    h = pl.program_id(1)
    nb = t // BLOCK
    kr = _rope(_rms(k_ref[...], kg_ref[...]), c_ref, sa_ref, sb_ref)
    k_scr[...] = kr.astype(BF16)
    v_scr[...] = v_ref[...].astype(BF16)
    for g in range(GQA_GROUP):
        qn = _rms(q_ref[:, g * HEAD_DIM:(g + 1) * HEAD_DIM], qg_ref[...])
        q_scr[g] = _rope(qn, c_ref, sa_ref, sb_ref).astype(BF16)
    ckb = ck_ref[...].astype(BF16)
    cvb = cv_ref[...].astype(BF16)
    sk = _sink_col(sink_ref, h, BLOCK)
    rows = GQA_GROUP * BLOCK
    qi = lax.broadcasted_iota(jnp.int32, (rows, BLOCK), 0) % BLOCK
    kj = lax.broadcasted_iota(jnp.int32, (rows, BLOCK), 1)
    prev_ok = kj >= qi
    next_ok = kj <= qi
    neg = jnp.finfo(F32).min

    def attend(n, has_prev, has_next):
        if isinstance(n, int):
            r0, k0 = n * BLOCK, (n - has_prev) * BLOCK
        else:
            r0 = pl.multiple_of(n * BLOCK, BLOCK)
            k0 = pl.multiple_of((n - has_prev) * BLOCK, BLOCK)
        nk = (1 + has_prev + has_next) * BLOCK
        q4 = jnp.concatenate([q_scr[g, pl.ds(r0, BLOCK), :] for g in range(GQA_GROUP)], axis=0)
        s_win = lax.dot_general(q4, k_scr[pl.ds(k0, nk), :], NT_DIMS,
                                preferred_element_type=F32) * scale
        tiles = [s_win[:, i * BLOCK:(i + 1) * BLOCK] for i in range(nk // BLOCK)]
        if has_prev:
            tiles[0] = jnp.where(prev_ok, tiles[0], neg)
        if has_next:
            tiles[-1] = jnp.where(next_ok, tiles[-1], neg)
        s_win = jnp.concatenate(tiles, axis=1)
        s_ctx = lax.dot_general(q4, ckb, NT_DIMS, preferred_element_type=F32) * scale
        m = jnp.maximum(jnp.max(s_win, axis=1, keepdims=True),
                        jnp.max(s_ctx, axis=1, keepdims=True))
        m = jnp.maximum(m, sk)
        p_win = jnp.exp(s_win - m)
        p_ctx = jnp.exp(s_ctx - m)
        den = (jnp.sum(p_win, axis=1, keepdims=True) + jnp.sum(p_ctx, axis=1, keepdims=True)
               + jnp.exp(sk - m))
        o = (_dot(p_win.astype(BF16), v_scr[pl.ds(k0, nk), :]) + _dot(p_ctx.astype(BF16), cvb)) / den
        for g in range(GQA_GROUP):
            o_ref[pl.ds(r0, BLOCK), g * HEAD_DIM:(g + 1) * HEAD_DIM] = (
                o[g * BLOCK:(g + 1) * BLOCK, :].astype(BF16))

    def interior(n, carry):
        attend(n, True, True)
        return carry

    attend(0, False, True)
    lax.fori_loop(1, nb - 1, interior, 0, unroll=3)
    attend(nb - 1, True, False)


def _lat_attn(proj, ck, cv, sink, qg, kg, rope_c, rope_sa, rope_sb):
    m = proj.shape[0]
    t = DEC_SEQ
    gw = GQA_GROUP * HEAD_DIM
    tab = pl.BlockSpec((t, HEAD_DIM), lambda b, h: (0, 0))
    vec = pl.BlockSpec((1, HEAD_DIM), lambda b, h: (0, 0))
    cache = pl.BlockSpec((None, PAST_LEN, HEAD_DIM), lambda b, h: (b, 0, h))
    return pl.pallas_call(
        _lat_attn_kernel,
        grid=(m // t, N_KV_HEADS),
        in_specs=[
            pl.BlockSpec(memory_space=pltpu.SMEM),
            pl.BlockSpec((t, gw), lambda b, h: (b, OFF_Q // gw + h)),
            pl.BlockSpec((t, HEAD_DIM), lambda b, h: (b, OFF_K // HEAD_DIM + h)),
            pl.BlockSpec((t, HEAD_DIM), lambda b, h: (b, OFF_V // HEAD_DIM + h)),
            cache, cache, vec, vec, tab, tab, tab,
        ],
        out_specs=pl.BlockSpec((t, gw), lambda b, h: (b, h)),
        out_shape=jax.ShapeDtypeStruct((m, D_MODEL), BF16),
        scratch_shapes=[
            pltpu.VMEM((GQA_GROUP, t, HEAD_DIM), BF16),
            pltpu.VMEM((t, HEAD_DIM), BF16),
            pltpu.VMEM((t, HEAD_DIM), BF16),
        ],
        compiler_params=_cp(("arbitrary", "arbitrary"), 40),
        name="latent_attention",
    )(sink, proj, proj, proj, ck, cv, qg, kg, rope_c, rope_sa, rope_sb)


def _log_sigmoid(x):
    return jnp.minimum(x, 0.0) - jnp.log(1.0 + jnp.exp(-jnp.abs(x)))


def _split2(x):
    hi = x.astype(BF16)
    lo = (x - hi.astype(F32)).astype(BF16)
    return hi, lo


def _gla_kernel(*refs, t, has_state):
    if has_state:
        (q_ref, k_ref, v_ref, a_ref, waf_ref, wab_ref, baf_ref, bab_ref, gn_ref,
         s0f_ref, s0b_ref, og_ref, *scr) = refs
    else:
        (q_ref, k_ref, v_ref, a_ref, waf_ref, wab_ref, baf_ref, bab_ref, gn_ref,
         og_ref, sf_out, sb_out, *scr) = refs
    stf, stb, vb, ob = scr[:4]
    qsf, kuf, decf, *tmpf = scr[4:13]
    qsb, kub, decb, *tmpb = scr[13:]
    c = GLA_CHUNK
    sc = GLA_SUPER
    nsub = sc // c
    nsc = t // sc
    vb[...] = v_ref[...].astype(BF16)

    row = lax.broadcasted_iota(jnp.int32, (sc, sc), 0)
    col = lax.broadcasted_iota(jnp.int32, (sc, sc), 1)
    rc, cc = row // c, col // c

    def intra(si, fwd):
        r0 = pl.multiple_of(si * sc, sc)
        w_ref, b_ref, qs_scr, ku_scr, dec_scr, tmp, dst = (
            (waf_ref, baf_ref, qsf, kuf, decf, tmpf, og_ref) if fwd
            else (wab_ref, bab_ref, qsb, kub, decb, tmpb, ob))
        qd, kd, ke, q2, q3, amat = tmp
        tri = jnp.where((col <= row) if fwd else (col >= row), 1.0, 0.0).astype(BF16)
        dist = (rc - cc) if fwd else (cc - rc)
        x = _dot(a_ref[pl.ds(r0, sc), :].astype(BF16), w_ref[...]) + b_ref[...]
        hi, lo = _split2(_log_sigmoid(x) / GLA_TAU)
        cum = _dot(tri, hi) + _dot(tri, lo)
        zero = jnp.zeros((1, GLA_HK), F32)

        def at_start(j):
            if fwd:
                return cum[j * c - 1:j * c, :] if j > 0 else zero
            return cum[(j + 1) * c:(j + 1) * c + 1, :] if j < nsub - 1 else zero

        def at_end(j):
            return cum[(j + 1) * c - 1:(j + 1) * c, :] if fwd else cum[j * c:j * c + 1, :]

        total = at_end(nsub - 1) if fwd else at_end(0)
        dec_scr[pl.ds(pl.multiple_of(si * 8, 8), 8), :] = jnp.broadcast_to(jnp.exp(total), (8, GLA_HK))
        for j in range(nsub):
            rs = slice(j * c, (j + 1) * c)
            rows = pl.ds(pl.multiple_of(r0 + j * c, c), c)
            cj = cum[rs, :]
            cs, ce = at_start(j), at_end(j)
            q = q_ref[rows, :] * (GLA_HK ** -0.5)
            k = k_ref[rows, :]
            qdj = q * jnp.exp(cj - cs)
            kej = k * jnp.exp(ce - cj)
            qd[rs, :] = qdj.astype(BF16)
            kd[rs, :] = (k * jnp.exp(cs - cj)).astype(BF16)
            ke[rs, :] = kej.astype(BF16)
            qs_scr[rows, :] = (qdj * jnp.exp(cs)).astype(BF16)
            ku_scr[rows, :] = (kej * jnp.exp(total - ce)).astype(BF16)
            p2 = j - 2 if fwd else j + 2
            if 0 <= p2 < nsub:
                l2 = (j - 2) if fwd else j
                q2[l2 * c:(l2 + 1) * c, :] = (qdj * jnp.exp(cs - at_end(p2))).astype(BF16)
            p3 = j - 3 if fwd else j + 3
            if 0 <= p3 < nsub:
                q3[...] = (qdj * jnp.exp(cs - at_end(p3))).astype(BF16)
        nt = lambda a, b: lax.dot_general(a, b, NT_DIMS, preferred_element_type=F32)
        tril = (col <= row) if fwd else (col >= row)
        amat[...] = (jnp.where((dist == 0) & tril, nt(qd[...], kd[...]), 0.0)
                     + jnp.where(dist == 1, nt(qd[...], ke[...]), 0.0))
        r2 = slice(2 * c, sc) if fwd else slice(0, 2 * c)
        amat[r2, :] += jnp.where(dist[r2, :] == 2, nt(q2[...], ke[...]), 0.0)
        r3 = slice(3 * c, sc) if fwd else slice(0, c)
        amat[r3, :] += jnp.where(dist[r3, :] == 3, nt(q3[...], ke[...]), 0.0)
        dst[pl.ds(r0, sc), :] = _dot(amat[...].astype(BF16), vb[pl.ds(r0, sc), :])

    def intra_body(i, carry):
        intra(i, True)
        intra(nsc - 1 - i, False)
        return carry

    lax.fori_loop(0, nsc, intra_body, 0, unroll=min(nsc, 2))

    if has_state:
        for i in range(nsc):
            for fwd in (True, False):
                si = i if fwd else nsc - 1 - i
                rows = slice(si * sc, (si + 1) * sc)
                st_ref, s0_ref, qs_scr, ku_scr, dec_scr, dst = (
                    (stf, s0f_ref, qsf, kuf, decf, og_ref) if fwd
                    else (stb, s0b_ref, qsb, kub, decb, ob))
                st = s0_ref[...] if i == 0 else st_ref[...]
                dst[rows, :] += _dot(qs_scr[rows, :], st.astype(BF16))
                if i < nsc - 1:
                    dec = jnp.broadcast_to(dec_scr[si * 8:si * 8 + 1, :], (LANES, GLA_HK)).T
                    dec = jnp.concatenate([dec] * (GLA_HV // LANES), axis=1)
                    st_ref[...] = st * dec + lax.dot_general(
                        ku_scr[rows, :], vb[rows, :], TN_DIMS, preferred_element_type=F32)
    else:
        sf_out[...] = lax.dot_general(kuf[...], vb[...], TN_DIMS, preferred_element_type=F32)
        sb_out[...] = lax.dot_general(kub[...], vb[...], TN_DIMS, preferred_element_type=F32)
    og_ref[...] = _rms(og_ref[...] + ob[...], gn_ref[...])


def _gla(proj, aproj, waf, wab, baf, bab, gn, t, s0f=None, s0b=None):
    m = proj.shape[0]
    nb = m // t
    has_state = s0f is not None
    assert t % GLA_SUPER == 0 and (has_state or t == GLA_SUPER)
    sc, c = GLA_SUPER, GLA_CHUNK
    st_shape_vmem = (GLA_HK, GLA_HV) if has_state else (8, LANES)
    per_direction = [
        pltpu.VMEM((t, GLA_HK), BF16),
        pltpu.VMEM((t, GLA_HK), BF16),
        pltpu.VMEM((8 * (t // sc), GLA_HK), F32),
        pltpu.VMEM((sc, GLA_HK), BF16),
        pltpu.VMEM((sc, GLA_HK), BF16),
        pltpu.VMEM((sc, GLA_HK), BF16),
        pltpu.VMEM((sc - 2 * c, GLA_HK), BF16),
        pltpu.VMEM((sc - 3 * c, GLA_HK), BF16),
        pltpu.VMEM((sc, sc), F32),
    ]
    st_spec = pl.BlockSpec((None, GLA_HK, GLA_HV), lambda b, h: (b * GLA_HEADS + h, 0, 0))
    in_specs = [
        pl.BlockSpec((t, GLA_HK), lambda b, h: (b, OFF_GQ // GLA_HK + h)),
        pl.BlockSpec((t, GLA_HK), lambda b, h: (b, OFF_GK // GLA_HK + h)),
        pl.BlockSpec((t, GLA_HV), lambda b, h: (b, OFF_GV // GLA_HV + h)),
        pl.BlockSpec((t, LANES), lambda b, h: (b, 0)),
        pl.BlockSpec((LANES, GLA_HK), lambda b, h: (0, h)),
        pl.BlockSpec((LANES, GLA_HK), lambda b, h: (0, h)),
        pl.BlockSpec((1, GLA_HK), lambda b, h: (0, h)),
        pl.BlockSpec((1, GLA_HK), lambda b, h: (0, h)),
        pl.BlockSpec((1, GLA_HV), lambda b, h: (0, h)),
    ]
    args = [proj, proj, proj, aproj, waf, wab, baf, bab, gn]
    og_spec = pl.BlockSpec((t, GLA_HV), lambda b, h: (b, h))
    og_shape = jax.ShapeDtypeStruct((m, GLA_DV), F32)
    if has_state:
        in_specs += [st_spec, st_spec]
        args += [s0f, s0b]
        out_specs, out_shape = og_spec, og_shape
    else:
        st_shape = jax.ShapeDtypeStruct((nb * GLA_HEADS, GLA_HK, GLA_HV), F32)
        out_specs, out_shape = [og_spec, st_spec, st_spec], [og_shape, st_shape, st_shape]
    return pl.pallas_call(
        functools.partial(_gla_kernel, t=t, has_state=has_state),
        grid=(nb, GLA_HEADS),
        in_specs=in_specs,
        out_specs=out_specs,
        out_shape=out_shape,
        scratch_shapes=[
            pltpu.VMEM(st_shape_vmem, F32),
            pltpu.VMEM(st_shape_vmem, F32),
            pltpu.VMEM((t, GLA_HV), BF16),
            pltpu.VMEM((t, GLA_HV), F32),
        ] + 2 * per_direction,
        compiler_params=_cp(("arbitrary", "arbitrary"), 56),
        name="gla_state" if has_state else "gla_zero",
    )(*args)


MERGE_SLAB = (16, 1024)


def _merge_kernel(oa_ref, og_ref, gr_ref, ga_ref, gg_ref, o_ref):
    rows, cols = o_ref.shape
    for r0 in range(0, rows, MERGE_SLAB[0]):
        for c0 in range(0, cols, MERGE_SLAB[1]):
            sl = (slice(r0, r0 + MERGE_SLAB[0]), slice(c0, c0 + MERGE_SLAB[1]))
            gr = gr_ref[sl].astype(F32)
            o_gla = og_ref[sl] * (gr * _sigmoid(gr))
            merged = (_sigmoid(ga_ref[sl].astype(F32)) * oa_ref[sl].astype(F32)
                      + _sigmoid(gg_ref[sl].astype(F32)) * o_gla)
            o_ref[sl] = merged.astype(BF16)


MERGE_TM = 256


def _merge(o_att, og, gates):
    m = o_att.shape[0]
    blk = lambda cb: pl.BlockSpec((MERGE_TM, D_MODEL), lambda i: (i, cb))
    cols = ((OFF_GR - OFF_GR) // D_MODEL, (OFF_GATT - OFF_GR) // D_MODEL, (OFF_GGLA - OFF_GR) // D_MODEL)
    return pl.pallas_call(
        _merge_kernel,
        grid=(m // MERGE_TM,),
        in_specs=[blk(0), blk(0)] + [blk(cb) for cb in cols],
        out_specs=blk(0),
        out_shape=jax.ShapeDtypeStruct((m, D_MODEL), BF16),
        compiler_params=_cp(("arbitrary",), 40),
        name="branch_merge",
    )(o_att, og, gates, gates, gates)


def _merge_side(o_att, og, gates, rows):
    m = o_att.shape[0]
    cols = ((OFF_GR - OFF_GR) // D_MODEL, (OFF_GATT - OFF_GR) // D_MODEL, (OFF_GGLA - OFF_GR) // D_MODEL)
    return _Side(_merge_kernel, (o_att, og, gates, gates, gates), (0, 0) + cols,
                 (jax.ShapeDtypeStruct((m, D_MODEL), BF16),), rows, m // rows)


OUT_TM = 512
OUT_TN = 1024


def _out_kernel(mg_ref, w_ref, x_ref, gt_ref, g2_ref, sh_ref, sc_ref, x1_ref, h2_ref, x1_scr):
    j = pl.program_id(1)
    nj = D_MODEL // OUT_TN
    x1 = x_ref[...] + gt_ref[...] * _dot(mg_ref[...], w_ref[...])
    x1_ref[...] = x1
    x1_scr[j] = x1

    @pl.when(j == nj - 1)
    def _():
        ssq = jnp.zeros((OUT_TM, 1), F32)
        for jj in range(nj):
            xs = x1_scr[jj]
            ssq = ssq + jnp.sum(xs * xs, axis=-1, keepdims=True)
        inv = lax.rsqrt(ssq / D_MODEL + EPS)
        for jj in range(nj):
            cs = slice(jj * OUT_TN, (jj + 1) * OUT_TN)
            y = x1_scr[jj] * inv * g2_ref[:, cs]
            h2_ref[:, cs] = (y * (1.0 + sc_ref[:, cs]) + sh_ref[:, cs]).astype(BF16)


def _out_proj(merged, w_out, x, gate1, g2, shift2, scale2, mod_off, rows_per_mod):
    m = x.shape[0]
    bpb = rows_per_mod // OUT_TM
    gate_spec = pl.BlockSpec((None, 1, OUT_TN), lambda i, j: (mod_off + i // bpb, 0, j))
    return pl.pallas_call(
        _out_kernel,
        grid=(m // OUT_TM, D_MODEL // OUT_TN),
        in_specs=[
            pl.BlockSpec((OUT_TM, D_MODEL), lambda i, j: (i, 0)),
            pl.BlockSpec((D_MODEL, OUT_TN), lambda i, j: (0, j)),
            pl.BlockSpec((OUT_TM, OUT_TN), lambda i, j: (i, j)),
            gate_spec,
            pl.BlockSpec((1, D_MODEL), lambda i, j: (0, 0)),
            _mod_spec(mod_off, bpb),
            _mod_spec(mod_off, bpb),
        ],
        out_specs=[
            pl.BlockSpec((OUT_TM, OUT_TN), lambda i, j: (i, j)),
            pl.BlockSpec((OUT_TM, D_MODEL), lambda i, j: (i, 0)),
        ],
        out_shape=[
            jax.ShapeDtypeStruct((m, D_MODEL), F32),
            jax.ShapeDtypeStruct((m, D_MODEL), BF16),
        ],
        scratch_shapes=[pltpu.VMEM((D_MODEL // OUT_TN, OUT_TM, OUT_TN), F32)],
        compiler_params=_cp(("arbitrary", "arbitrary"), 56),
        name="out_proj_residual_norm",
    )(merged, w_out, x, gate1, g2, shift2, scale2)


FFI_TM = 2048
FFI_TN = 256


def _ffn_in_kernel(h_ref, wg_ref, wu_ref, o_ref):
    h = h_ref[...]
    g = _dot(h, wg_ref[...].astype(BF16))
    u = _dot(h, wu_ref[...].astype(BF16))
    o_ref[...] = (g * _sigmoid(g) * u).astype(BF16)


def _ffn_in(h2, w):
    m = h2.shape[0]
    return pl.pallas_call(
        _ffn_in_kernel,
        grid=(m // FFI_TM, D_FF // FFI_TN),
        in_specs=[
            pl.BlockSpec((FFI_TM, D_MODEL), lambda i, j: (i, 0)),
            pl.BlockSpec((D_MODEL, FFI_TN), lambda i, j: (0, j)),
            pl.BlockSpec((D_MODEL, FFI_TN), lambda i, j: (0, D_FF // FFI_TN + j)),
        ],
        out_specs=pl.BlockSpec((FFI_TM, FFI_TN), lambda i, j: (i, j)),
        out_shape=jax.ShapeDtypeStruct((m, D_FF), BF16),
        compiler_params=_cp(("arbitrary", "arbitrary"), 58),
        name="ffn_in_swiglu",
    )(h2, w, w)


FFO_TM = 512
FFO_TN = 512


def _ffn_out_kernel(a_ref, w_ref, x_ref, gt_ref, o_ref):
    o_ref[...] = x_ref[...] + gt_ref[...] * _dot(a_ref[...], w_ref[...])


def _ffn_out(act, w, x1, gate2, mod_off, rows_per_mod, side=None):
    m = x1.shape[0]
    bpb = rows_per_mod // FFO_TM
    in_specs = [
        pl.BlockSpec((FFO_TM, D_FF), lambda i, j: (i, 0)),
        pl.BlockSpec((D_FF, FFO_TN), lambda i, j: (0, j)),
        pl.BlockSpec((FFO_TM, FFO_TN), lambda i, j: (i, j)),
        pl.BlockSpec((None, 1, FFO_TN), lambda i, j: (mod_off + i // bpb, 0, j)),
    ]
    return _host_call(
        _ffn_out_kernel, (m // FFO_TM, D_MODEL // FFO_TN), in_specs,
        pl.BlockSpec((FFO_TM, FFO_TN), lambda i, j: (i, j)),
        jax.ShapeDtypeStruct((m, D_MODEL), F32), (act, w, x1, gate2), side, 60, "ffn_out_residual")


def _rope_tables(t):
    rows = t // GRID_W
    half = HEAD_DIM // 2
    row = jnp.repeat(jnp.arange(rows, dtype=F32), GRID_W)
    col = jnp.tile(jnp.arange(GRID_W, dtype=F32), rows)
    inv = ROPE_THETA ** (-jnp.arange(0, half, 2, dtype=F32) / half)
    ar = row[:, None] * inv[None, :]
    ac = col[:, None] * inv[None, :]
    cr, sr, cc, sc = jnp.cos(ar), jnp.sin(ar), jnp.cos(ac), jnp.sin(ac)
    z = jnp.zeros_like(sr)
    tab_c = jnp.concatenate([cr, cr, cc, cc], axis=1)
    tab_sa = jnp.concatenate([-sr, z, -sc, z], axis=1)
    tab_sb = jnp.concatenate([z, sr, z, sc], axis=1)
    return tab_c, tab_sa, tab_sb


def kernel(x_prompt, x_sample, c, cache_k, cache_v, state_gla_fwd, state_gla_bwd, c_ctx, w_ada, b_ada, norm1_g, norm2_g, w_in, q_norm_g, k_norm_g, attn_sink, w_a2_fwd, b_a_fwd, w_a2_bwd, b_a_bwd, gla_norm_g, w_out, w_ffn_in, w_ffn_out):
    assert w_ada.shape[0] == 1, "single trunk layer"
    cc = jnp.zeros((MOD_ROWS, D_MODEL), F32).at[0].set(c_ctx).at[1:1 + DEC_BATCH].set(c)
    mod_all = _ada(cc, w_ada[0], b_ada[0][None, :])
    mod = tuple(mod_all[:, i * D_MODEL:(i + 1) * D_MODEL].reshape(MOD_ROWS, 1, D_MODEL)
                for i in range(N_MOD))

    r = GLA_GATE_RANK
    wa = jnp.zeros((D_MODEL, LANES), BF16).at[:, :2 * r].set(w_in[0][:, D_WIDE:].astype(BF16))
    waf = jnp.zeros((LANES, GLA_DK), BF16).at[:r].set(w_a2_fwd[0].astype(BF16))
    wab = jnp.zeros((LANES, GLA_DK), BF16).at[r:2 * r].set(w_a2_bwd[0].astype(BF16))
    g1, g2 = norm1_g[0][None, :], norm2_g[0][None, :]
    w_in_t = w_in[0].T
    qg, kg, sink = q_norm_g[0][None, :], k_norm_g[0][None, :], attn_sink[0]
    gla_w = (waf, wab, b_a_fwd[0][None, :], b_a_bwd[0][None, :], gla_norm_g[0][None, :])
    shift1, scale1, gate1, shift2, scale2, gate2 = mod
    kvw = N_KV_HEADS * HEAD_DIM
    n_wide, n_gate = OFF_GR, D_WIDE - OFF_GR
    m_ctx, m_lat = BATCH * SEQ, DEC_BATCH * DEC_SEQ
    ctx_mod, lat_mod = (0, m_ctx), (1, DEC_SEQ)

    xp = x_prompt.reshape(m_ctx, D_MODEL)
    xs = x_sample.reshape(m_lat, D_MODEL)

    h_c, a_c = _norm_mod(xp, g1, shift1, scale1, wa, *ctx_mod)
    proj_c, w_fo = _proj(h_c, w_in_t, 0, n_wide, F32, _cast_side(w_ffn_out[0], 128))
    gates_c, w_o = _proj(h_c, w_in_t, n_wide, n_gate, BF16, _cast_side(w_out[0], 64))
    oatt_c, new_k, new_v = _ctx_attn(proj_c, sink, qg, kg)
    og_c, s_f, s_b = _gla(proj_c, a_c, *gla_w, SEQ)

    h_l, a_l = _norm_mod(xs, g1, shift1, scale1, wa, *lat_mod)
    merged_c = _merge(oatt_c, og_c, gates_c)
    proj_l = _proj(h_l, w_in_t, 0, n_wide, F32)
    gates_l = _proj(h_l, w_in_t, n_wide, n_gate, BF16)
    oatt_l = _lat_attn(
        proj_l,
        cache_k[:, 0].reshape(DEC_BATCH, PAST_LEN, kvw),
        cache_v[:, 0].reshape(DEC_BATCH, PAST_LEN, kvw),
        sink, qg, kg, *_rope_tables(DEC_SEQ))
    og_l = _gla(proj_l, a_l, *gla_w, DEC_SEQ,
                state_gla_fwd[:, 0].reshape(DEC_BATCH * GLA_HEADS, GLA_HK, GLA_HV),
                state_gla_bwd[:, 0].reshape(DEC_BATCH * GLA_HEADS, GLA_HK, GLA_HV))

    x1_c, h2_c = _out_proj(merged_c, w_o, xp, gate1, g2, shift2, scale2, *ctx_mod)
    act_c = _ffn_in(h2_c, w_ffn_in[0])
    yp, merged_l = _ffn_out(act_c, w_fo, x1_c, gate2, *ctx_mod,
                            side=_merge_side(oatt_l, og_l, gates_l, 64))

    x1_l, h2_l = _out_proj(merged_l, w_o, xs, gate1, g2, shift2, scale2, *lat_mod)
    act_l = _ffn_in(h2_l, w_ffn_in[0])
    ys = _ffn_out(act_l, w_fo, x1_l, gate2, *lat_mod)
    return (
        yp.reshape(BATCH, SEQ, D_MODEL),
        ys.reshape(DEC_BATCH, DEC_SEQ, D_MODEL),
        new_k.reshape(BATCH, 1, SEQ, N_KV_HEADS, HEAD_DIM),
        new_v.reshape(BATCH, 1, SEQ, N_KV_HEADS, HEAD_DIM),
        s_f.reshape(BATCH, 1, GLA_HEADS, GLA_HK, GLA_HV),
        s_b.reshape(BATCH, 1, GLA_HEADS, GLA_HK, GLA_HV),
    )
```

```python
import functools
from typing import Callable, NamedTuple

import jax
import jax.numpy as jnp
from jax import lax
from jax.experimental import pallas as pl
from jax.experimental.pallas import tpu as pltpu

F32 = jnp.float32
BF16 = jnp.bfloat16

D_MODEL = 4096
BATCH = 32
SEQ = 256
DEC_BATCH = 8
DEC_SEQ = 1024
PAST_LEN = 256
GRID_W = 64
HEAD_DIM = 128
N_Q_HEADS = 32
N_KV_HEADS = 8
GQA_GROUP = N_Q_HEADS // N_KV_HEADS
WINDOW = 128
BLOCK = 128
ROPE_THETA = 10000.0
GLA_HEADS = 4
GLA_DK = D_MODEL // 2
GLA_DV = D_MODEL
GLA_HK = GLA_DK // GLA_HEADS
GLA_HV = GLA_DV // GLA_HEADS
GLA_GATE_RANK = 16
GLA_TAU = 16.0
GLA_CHUNK = 64
GLA_SUPER = 256
D_FF = 11008
N_MOD = 6
EPS = 1e-6

OFF_Q = 0
OFF_K = 4096
OFF_V = 5120
OFF_GQ = 6144
OFF_GK = 8192
OFF_GV = 10240
OFF_GR = 14336
OFF_GATT = 18432
OFF_GGLA = 22528
D_WIDE = 26624
LANES = 128
MXU_ACC_ROWS = 512
MOD_ROWS = 16

MIB = 1024 * 1024
NT_DIMS = (((1,), (1,)), ((), ()))
TN_DIMS = (((0,), (0,)), ((), ()))


def _cp(sem, vmem_mib):
    return pltpu.CompilerParams(dimension_semantics=sem, vmem_limit_bytes=vmem_mib * MIB)


def _rms(x, g):
    ms = jnp.mean(x * x, axis=-1, keepdims=True)
    return x * lax.rsqrt(ms + EPS) * g


def _sigmoid(x):
    return 0.5 * jnp.tanh(0.5 * x) + 0.5


def _dot(a, b):
    return jnp.dot(a, b, preferred_element_type=F32)


ADA_TN = 512


def _ada_kernel(c_ref, w_ref, b_ref, o_ref):
    c = c_ref[...]
    s = (c * _sigmoid(c)).astype(BF16)
    o_ref[...] = _dot(s, w_ref[...].astype(BF16)) + b_ref[...]


def _ada(cc, w_ada, b_ada):
    n = w_ada.shape[1]
    return pl.pallas_call(
        _ada_kernel,
        grid=(n // ADA_TN,),
        in_specs=[
            pl.BlockSpec((MOD_ROWS, D_MODEL), lambda j: (0, 0)),
            pl.BlockSpec((D_MODEL, ADA_TN), lambda j: (0, j)),
            pl.BlockSpec((1, ADA_TN), lambda j: (0, j)),
        ],
        out_specs=pl.BlockSpec((MOD_ROWS, ADA_TN), lambda j: (0, j)),
        out_shape=jax.ShapeDtypeStruct((MOD_ROWS, n), F32),
        compiler_params=_cp(("arbitrary",), 40),
        name="ada_ln",
    )(cc, w_ada, b_ada)


NORM_TM = 512
PROJ_TM = 2048
PROJ_TN = 512


def _mod_spec(mod_off, bpb):
    return pl.BlockSpec((None, 1, D_MODEL), lambda i, *_: (mod_off + i // bpb, 0, 0))


def _norm_kernel(x_ref, g_ref, sh_ref, sc_ref, wa_ref, h_ref, a_ref):
    h = _rms(x_ref[...], g_ref[...]) * (1.0 + sc_ref[...]) + sh_ref[...]
    hb = h.astype(BF16)
    h_ref[...] = hb
    a_ref[...] = _dot(hb, wa_ref[...])


def _norm_mod(x, g1, shift, scale, wa, mod_off, rows_per_mod):
    m = x.shape[0]
    bpb = rows_per_mod // NORM_TM
    return pl.pallas_call(
        _norm_kernel,
        grid=(m // NORM_TM,),
        in_specs=[
            pl.BlockSpec((NORM_TM, D_MODEL), lambda i: (i, 0)),
            pl.BlockSpec((1, D_MODEL), lambda i: (0, 0)),
            _mod_spec(mod_off, bpb),
            _mod_spec(mod_off, bpb),
            pl.BlockSpec((D_MODEL, LANES), lambda i: (0, 0)),
        ],
        out_specs=[
            pl.BlockSpec((NORM_TM, D_MODEL), lambda i: (i, 0)),
            pl.BlockSpec((NORM_TM, LANES), lambda i: (i, 0)),
        ],
        out_shape=[
            jax.ShapeDtypeStruct((m, D_MODEL), BF16),
            jax.ShapeDtypeStruct((m, LANES), F32),
        ],
        compiler_params=_cp(("arbitrary",), 40),
        name="norm_modulate",
    )(x, g1, shift, scale, wa)


class _Side(NamedTuple):
    fn: Callable
    ins: tuple
    in_cols: tuple
    outs: tuple
    rows: int
    nblk: int

    def specs(self, nj):
        def spec(shape_cols, cb):
            return pl.BlockSpec((self.rows, shape_cols),
                                lambda i, j: (jnp.minimum(i * nj + j, self.nblk - 1), cb))
        in_specs = [spec(self.outs[0].shape[1], cb) for cb in self.in_cols]
        out_specs = [spec(o.shape[1], 0) for o in self.outs]
        return in_specs, out_specs


def _host_kernel(body, n_in, n_out, side):
    def kern(*refs):
        n_sin = len(side.ins) if side else 0
        if side:
            side.fn(*refs[n_in:n_in + n_sin], *refs[n_in + n_sin + n_out:])
        body(*refs[:n_in], *refs[n_in + n_sin:n_in + n_sin + n_out])
    return kern


def _host_call(body, grid, in_specs, out_spec, out_shape, args, side, vmem_mib, name):
    if side is None:
        s_in, s_out = [], []
    else:
        assert grid[0] * grid[1] >= side.nblk
        s_in, s_out = side.specs(grid[1])
    res = pl.pallas_call(
        _host_kernel(body, len(in_specs), 1, side),
        grid=grid,
        in_specs=in_specs + s_in,
        out_specs=[out_spec] + s_out,
        out_shape=[out_shape] + (list(side.outs) if side else []),
        compiler_params=_cp(("arbitrary", "arbitrary"), vmem_mib),
        name=name,
    )(*args, *(side.ins if side else ()))
    return res[0] if side is None else res


def _cast_kernel(src_ref, dst_ref):
    dst_ref[...] = src_ref[...].astype(BF16)


def _cast_side(w, rows):
    n, d = w.shape
    return _Side(_cast_kernel, (w,), (0,), (jax.ShapeDtypeStruct((n, d), BF16),), rows, n // rows)


def _proj_kernel(h_ref, wt_ref, o_ref):
    wt = wt_ref[...].astype(BF16)
    for r0 in range(0, PROJ_TM, MXU_ACC_ROWS):
        rs = slice(r0, r0 + MXU_ACC_ROWS)
        o_ref[rs, :] = lax.dot_general(h_ref[rs, :], wt, NT_DIMS,
                                       preferred_element_type=F32).astype(o_ref.dtype)


def _proj(h, wt, col0, ncols, out_dtype, side=None):
    m = h.shape[0]
    j0 = col0 // PROJ_TN
    in_specs = [
        pl.BlockSpec((PROJ_TM, D_MODEL), lambda i, j: (i, 0), pipeline_mode=pl.Buffered(1)),
        pl.BlockSpec((PROJ_TN, D_MODEL), lambda i, j: (j0 + j, 0)),
    ]
    return _host_call(
        _proj_kernel, (m // PROJ_TM, ncols // PROJ_TN), in_specs,
        pl.BlockSpec((PROJ_TM, PROJ_TN), lambda i, j: (i, j)),
        jax.ShapeDtypeStruct((m, ncols), out_dtype), (h, wt), side, 60, "in_proj")


def _sink_col(sink_ref, h, rows_per_head):
    rid = lax.broadcasted_iota(jnp.int32, (GQA_GROUP * rows_per_head, 1), 0) // rows_per_head
    col = jnp.full(rid.shape, sink_ref[h * GQA_GROUP], F32)
    for g in range(1, GQA_GROUP):
        col = jnp.where(rid == g, sink_ref[h * GQA_GROUP + g], col)
    return col


def _ctx_attn_kernel(sink_ref, q_ref, k_ref, v_ref, qg_ref, kg_ref, o_ref, nk_ref, nv_ref):
    scale = HEAD_DIM ** -0.5
    nv_ref[...] = v_ref[...]
    for h in range(N_KV_HEADS):
        hs = slice(h * HEAD_DIM, (h + 1) * HEAD_DIM)
        kn = _rms(k_ref[:, hs], kg_ref[...])
        nk_ref[:, hs] = kn
        kb = kn.astype(BF16)
        vb = v_ref[:, hs].astype(BF16)
        qs = []
        for g in range(GQA_GROUP):
            c0 = (h * GQA_GROUP + g) * HEAD_DIM
            qs.append(_rms(q_ref[:, c0:c0 + HEAD_DIM], qg_ref[...]).astype(BF16))
        q4 = jnp.concatenate(qs, axis=0)
        s = lax.dot_general(q4, kb, NT_DIMS, preferred_element_type=F32) * scale
        sk = _sink_col(sink_ref, h, SEQ)
        m = jnp.maximum(jnp.max(s, axis=1, keepdims=True), sk)
        p = jnp.exp(s - m)
        den = jnp.sum(p, axis=1, keepdims=True) + jnp.exp(sk - m)
        o = _dot(p.astype(BF16), vb) / den
        for g in range(GQA_GROUP):
            c0 = (h * GQA_GROUP + g) * HEAD_DIM
            o_ref[:, c0:c0 + HEAD_DIM] = o[g * SEQ:(g + 1) * SEQ, :].astype(BF16)


def _ctx_attn(proj, sink, qg, kg):
    m = proj.shape[0]
    kvw = N_KV_HEADS * HEAD_DIM
    return pl.pallas_call(
        _ctx_attn_kernel,
        grid=(m // SEQ,),
        in_specs=[
            pl.BlockSpec(memory_space=pltpu.SMEM),
            pl.BlockSpec((SEQ, D_MODEL), lambda b: (b, OFF_Q // D_MODEL)),
            pl.BlockSpec((SEQ, kvw), lambda b: (b, OFF_K // kvw)),
            pl.BlockSpec((SEQ, kvw), lambda b: (b, OFF_V // kvw)),
            pl.BlockSpec((1, HEAD_DIM), lambda b: (0, 0)),
            pl.BlockSpec((1, HEAD_DIM), lambda b: (0, 0)),
        ],
        out_specs=[
            pl.BlockSpec((SEQ, D_MODEL), lambda b: (b, 0)),
            pl.BlockSpec((SEQ, kvw), lambda b: (b, 0)),
            pl.BlockSpec((SEQ, kvw), lambda b: (b, 0)),
        ],
        out_shape=[
            jax.ShapeDtypeStruct((m, D_MODEL), BF16),
            jax.ShapeDtypeStruct((m, kvw), F32),
            jax.ShapeDtypeStruct((m, kvw), F32),
        ],
        compiler_params=_cp(("arbitrary",), 40),
        name="ctx_attention",
    )(sink, proj, proj, proj, qg, kg)


def _rope(x, c_ref, sa_ref, sb_ref):
    up = pltpu.roll(x, HEAD_DIM - 32, axis=1)
    dn = pltpu.roll(x, 32, axis=1)
    return x * c_ref[...] + up * sa_ref[...] + dn * sb_ref[...]


def _lat_attn_kernel(sink_ref, q_ref, k_ref, v_ref, ck_ref, cv_ref, qg_ref, kg_ref,
                     c_ref, sa_ref, sb_ref, o_ref, q_scr, k_scr, v_scr):
    t = DEC_SEQ
    scale = HEAD_DIM ** -0.5
    h = pl.program_id(1)
    nb = t // BLOCK
    kr = _rope(_rms(k_ref[...], kg_ref[...]), c_ref, sa_ref, sb_ref)
    k_scr[...] = kr.astype(BF16)
    v_scr[...] = v_ref[...].astype(BF16)
    for g in range(GQA_GROUP):
        qn = _rms(q_ref[:, g * HEAD_DIM:(g + 1) * HEAD_DIM], qg_ref[...])
        q_scr[g] = _rope(qn, c_ref, sa_ref, sb_ref).astype(BF16)
    ckb = ck_ref[...].astype(BF16)
    cvb = cv_ref[...].astype(BF16)
    sk = _sink_col(sink_ref, h, BLOCK)
    rows = GQA_GROUP * BLOCK
    qi = lax.broadcasted_iota(jnp.int32, (rows, BLOCK), 0) % BLOCK
    kj = lax.broadcasted_iota(jnp.int32, (rows, BLOCK), 1)
    prev_ok = kj >= qi
    next_ok = kj <= qi
    neg = jnp.finfo(F32).min

    def attend(n, has_prev, has_next):
        if isinstance(n, int):
            r0, k0 = n * BLOCK, (n - has_prev) * BLOCK
        else:
            r0 = pl.multiple_of(n * BLOCK, BLOCK)
            k0 = pl.multiple_of((n - has_prev) * BLOCK, BLOCK)
        nk = (1 + has_prev + has_next) * BLOCK
        q4 = jnp.concatenate([q_scr[g, pl.ds(r0, BLOCK), :] for g in range(GQA_GROUP)], axis=0)
        s_win = lax.dot_general(q4, k_scr[pl.ds(k0, nk), :], NT_DIMS,
                                preferred_element_type=F32) * scale
        tiles = [s_win[:, i * BLOCK:(i + 1) * BLOCK] for i in range(nk // BLOCK)]
        if has_prev:
            tiles[0] = jnp.where(prev_ok, tiles[0], neg)
        if has_next:
            tiles[-1] = jnp.where(next_ok, tiles[-1], neg)
        s_win = jnp.concatenate(tiles, axis=1)
        s_ctx = lax.dot_general(q4, ckb, NT_DIMS, preferred_element_type=F32) * scale
        m = jnp.maximum(jnp.max(s_win, axis=1, keepdims=True),
                        jnp.max(s_ctx, axis=1, keepdims=True))
        m = jnp.maximum(m, sk)
        p_win = jnp.exp(s_win - m)
        p_ctx = jnp.exp(s_ctx - m)
        den = (jnp.sum(p_win, axis=1, keepdims=True) + jnp.sum(p_ctx, axis=1, keepdims=True)
               + jnp.exp(sk - m))
        o = (_dot(p_win.astype(BF16), v_scr[pl.ds(k0, nk), :]) + _dot(p_ctx.astype(BF16), cvb)) / den
        for g in range(GQA_GROUP):
            o_ref[pl.ds(r0, BLOCK), g * HEAD_DIM:(g + 1) * HEAD_DIM] = (
                o[g * BLOCK:(g + 1) * BLOCK, :].astype(BF16))

    def interior(n, carry):
        attend(n, True, True)
        return carry

    attend(0, False, True)
    lax.fori_loop(1, nb - 1, interior, 0, unroll=3)
    attend(nb - 1, True, False)


def _lat_attn(proj, ck, cv, sink, qg, kg, rope_c, rope_sa, rope_sb):
    m = proj.shape[0]
    t = DEC_SEQ
    gw = GQA_GROUP * HEAD_DIM
    tab = pl.BlockSpec((t, HEAD_DIM), lambda b, h: (0, 0))
    vec = pl.BlockSpec((1, HEAD_DIM), lambda b, h: (0, 0))
    cache = pl.BlockSpec((None, PAST_LEN, HEAD_DIM), lambda b, h: (b, 0, h))
    return pl.pallas_call(
        _lat_attn_kernel,
        grid=(m // t, N_KV_HEADS),
        in_specs=[
            pl.BlockSpec(memory_space=pltpu.SMEM),
            pl.BlockSpec((t, gw), lambda b, h: (b, OFF_Q // gw + h)),
            pl.BlockSpec((t, HEAD_DIM), lambda b, h: (b, OFF_K // HEAD_DIM + h)),
            pl.BlockSpec((t, HEAD_DIM), lambda b, h: (b, OFF_V // HEAD_DIM + h)),
            cache, cache, vec, vec, tab, tab, tab,
        ],
        out_specs=pl.BlockSpec((t, gw), lambda b, h: (b, h)),
        out_shape=jax.ShapeDtypeStruct((m, D_MODEL), BF16),
        scratch_shapes=[
            pltpu.VMEM((GQA_GROUP, t, HEAD_DIM), BF16),
            pltpu.VMEM((t, HEAD_DIM), BF16),
            pltpu.VMEM((t, HEAD_DIM), BF16),
        ],
        compiler_params=_cp(("arbitrary", "arbitrary"), 40),
        name="latent_attention",
    )(sink, proj, proj, proj, ck, cv, qg, kg, rope_c, rope_sa, rope_sb)


def _log_sigmoid(x):
    return jnp.minimum(x, 0.0) - jnp.log(1.0 + jnp.exp(-jnp.abs(x)))


def _split2(x):
    hi = x.astype(BF16)
    lo = (x - hi.astype(F32)).astype(BF16)
    return hi, lo


def _gla_kernel(*refs, t, has_state):
    if has_state:
        (q_ref, k_ref, v_ref, a_ref, waf_ref, wab_ref, baf_ref, bab_ref, gn_ref,
         s0f_ref, s0b_ref, og_ref, *scr) = refs
    else:
        (q_ref, k_ref, v_ref, a_ref, waf_ref, wab_ref, baf_ref, bab_ref, gn_ref,
         og_ref, sf_out, sb_out, *scr) = refs
    stf, stb, vb, ob = scr[:4]
    qsf, kuf, decf, *tmpf = scr[4:13]
    qsb, kub, decb, *tmpb = scr[13:]
    c = GLA_CHUNK
    sc = GLA_SUPER
    nsub = sc // c
    nsc = t // sc
    vb[...] = v_ref[...].astype(BF16)

    row = lax.broadcasted_iota(jnp.int32, (sc, sc), 0)
    col = lax.broadcasted_iota(jnp.int32, (sc, sc), 1)
    rc, cc = row // c, col // c

    def intra(si, fwd):
        r0 = pl.multiple_of(si * sc, sc)
        w_ref, b_ref, qs_scr, ku_scr, dec_scr, tmp, dst = (
            (waf_ref, baf_ref, qsf, kuf, decf, tmpf, og_ref) if fwd
            else (wab_ref, bab_ref, qsb, kub, decb, tmpb, ob))
        qd, kd, ke, q2, q3, amat = tmp
        tri = jnp.where((col <= row) if fwd else (col >= row), 1.0, 0.0).astype(BF16)
        dist = (rc - cc) if fwd else (cc - rc)
        x = _dot(a_ref[pl.ds(r0, sc), :].astype(BF16), w_ref[...]) + b_ref[...]
        hi, lo = _split2(_log_sigmoid(x) / GLA_TAU)
        cum = _dot(tri, hi) + _dot(tri, lo)
        zero = jnp.zeros((1, GLA_HK), F32)

        def at_start(j):
            if fwd:
                return cum[j * c - 1:j * c, :] if j > 0 else zero
            return cum[(j + 1) * c:(j + 1) * c + 1, :] if j < nsub - 1 else zero

        def at_end(j):
            return cum[(j + 1) * c - 1:(j + 1) * c, :] if fwd else cum[j * c:j * c + 1, :]

        total = at_end(nsub - 1) if fwd else at_end(0)
        dec_scr[pl.ds(pl.multiple_of(si * 8, 8), 8), :] = jnp.broadcast_to(jnp.exp(total), (8, GLA_HK))
        for j in range(nsub):
            rs = slice(j * c, (j + 1) * c)
            rows = pl.ds(pl.multiple_of(r0 + j * c, c), c)
            cj = cum[rs, :]
            cs, ce = at_start(j), at_end(j)
            q = q_ref[rows, :] * (GLA_HK ** -0.5)
            k = k_ref[rows, :]
            qdj = q * jnp.exp(cj - cs)
            kej = k * jnp.exp(ce - cj)
            qd[rs, :] = qdj.astype(BF16)
            kd[rs, :] = (k * jnp.exp(cs - cj)).astype(BF16)
            ke[rs, :] = kej.astype(BF16)
            qs_scr[rows, :] = (qdj * jnp.exp(cs)).astype(BF16)
            ku_scr[rows, :] = (kej * jnp.exp(total - ce)).astype(BF16)
            p2 = j - 2 if fwd else j + 2
            if 0 <= p2 < nsub:
                l2 = (j - 2) if fwd else j
                q2[l2 * c:(l2 + 1) * c, :] = (qdj * jnp.exp(cs - at_end(p2))).astype(BF16)
            p3 = j - 3 if fwd else j + 3
            if 0 <= p3 < nsub:
                q3[...] = (qdj * jnp.exp(cs - at_end(p3))).astype(BF16)
        nt = lambda a, b: lax.dot_general(a, b, NT_DIMS, preferred_element_type=F32)
        tril = (col <= row) if fwd else (col >= row)
        amat[...] = (jnp.where((dist == 0) & tril, nt(qd[...], kd[...]), 0.0)
                     + jnp.where(dist == 1, nt(qd[...], ke[...]), 0.0))
        r2 = slice(2 * c, sc) if fwd else slice(0, 2 * c)
        amat[r2, :] += jnp.where(dist[r2, :] == 2, nt(q2[...], ke[...]), 0.0)
        r3 = slice(3 * c, sc) if fwd else slice(0, c)
        amat[r3, :] += jnp.where(dist[r3, :] == 3, nt(q3[...], ke[...]), 0.0)
        dst[pl.ds(r0, sc), :] = _dot(amat[...].astype(BF16), vb[pl.ds(r0, sc), :])

    def intra_body(i, carry):
        intra(i, True)
        intra(nsc - 1 - i, False)
        return carry

    lax.fori_loop(0, nsc, intra_body, 0, unroll=min(nsc, 2))

    if has_state:
        for i in range(nsc):
            for fwd in (True, False):
                si = i if fwd else nsc - 1 - i
                rows = slice(si * sc, (si + 1) * sc)
                st_ref, s0_ref, qs_scr, ku_scr, dec_scr, dst = (
                    (stf, s0f_ref, qsf, kuf, decf, og_ref) if fwd
                    else (stb, s0b_ref, qsb, kub, decb, ob))
                st = s0_ref[...] if i == 0 else st_ref[...]
                dst[rows, :] += _dot(qs_scr[rows, :], st.astype(BF16))
                if i < nsc - 1:
                    dec = jnp.broadcast_to(dec_scr[si * 8:si * 8 + 1, :], (LANES, GLA_HK)).T
                    dec = jnp.concatenate([dec] * (GLA_HV // LANES), axis=1)
                    st_ref[...] = st * dec + lax.dot_general(
                        ku_scr[rows, :], vb[rows, :], TN_DIMS, preferred_element_type=F32)
    else:
        sf_out[...] = lax.dot_general(kuf[...], vb[...], TN_DIMS, preferred_element_type=F32)
        sb_out[...] = lax.dot_general(kub[...], vb[...], TN_DIMS, preferred_element_type=F32)
    og_ref[...] = _rms(og_ref[...] + ob[...], gn_ref[...])


def _gla(proj, aproj, waf, wab, baf, bab, gn, t, s0f=None, s0b=None):
    m = proj.shape[0]
    nb = m // t
    has_state = s0f is not None
    assert t % GLA_SUPER == 0 and (has_state or t == GLA_SUPER)
    sc, c = GLA_SUPER, GLA_CHUNK
    st_shape_vmem = (GLA_HK, GLA_HV) if has_state else (8, LANES)
    per_direction = [
        pltpu.VMEM((t, GLA_HK), BF16),
        pltpu.VMEM((t, GLA_HK), BF16),
        pltpu.VMEM((8 * (t // sc), GLA_HK), F32),
        pltpu.VMEM((sc, GLA_HK), BF16),
        pltpu.VMEM((sc, GLA_HK), BF16),
        pltpu.VMEM((sc, GLA_HK), BF16),
        pltpu.VMEM((sc - 2 * c, GLA_HK), BF16),
        pltpu.VMEM((sc - 3 * c, GLA_HK), BF16),
        pltpu.VMEM((sc, sc), F32),
    ]
    st_spec = pl.BlockSpec((None, GLA_HK, GLA_HV), lambda b, h: (b * GLA_HEADS + h, 0, 0))
    in_specs = [
        pl.BlockSpec((t, GLA_HK), lambda b, h: (b, OFF_GQ // GLA_HK + h)),
        pl.BlockSpec((t, GLA_HK), lambda b, h: (b, OFF_GK // GLA_HK + h)),
        pl.BlockSpec((t, GLA_HV), lambda b, h: (b, OFF_GV // GLA_HV + h)),
        pl.BlockSpec((t, LANES), lambda b, h: (b, 0)),
        pl.BlockSpec((LANES, GLA_HK), lambda b, h: (0, h)),
        pl.BlockSpec((LANES, GLA_HK), lambda b, h: (0, h)),
        pl.BlockSpec((1, GLA_HK), lambda b, h: (0, h)),
        pl.BlockSpec((1, GLA_HK), lambda b, h: (0, h)),
        pl.BlockSpec((1, GLA_HV), lambda b, h: (0, h)),
    ]
    args = [proj, proj, proj, aproj, waf, wab, baf, bab, gn]
    og_spec = pl.BlockSpec((t, GLA_HV), lambda b, h: (b, h))
    og_shape = jax.ShapeDtypeStruct((m, GLA_DV), F32)
    if has_state:
        in_specs += [st_spec, st_spec]
        args += [s0f, s0b]
        out_specs, out_shape = og_spec, og_shape
    else:
        st_shape = jax.ShapeDtypeStruct((nb * GLA_HEADS, GLA_HK, GLA_HV), F32)
        out_specs, out_shape = [og_spec, st_spec, st_spec], [og_shape, st_shape, st_shape]
    return pl.pallas_call(
        functools.partial(_gla_kernel, t=t, has_state=has_state),
        grid=(nb, GLA_HEADS),
        in_specs=in_specs,
        out_specs=out_specs,
        out_shape=out_shape,
        scratch_shapes=[
            pltpu.VMEM(st_shape_vmem, F32),
            pltpu.VMEM(st_shape_vmem, F32),
            pltpu.VMEM((t, GLA_HV), BF16),
            pltpu.VMEM((t, GLA_HV), F32),
        ] + 2 * per_direction,
        compiler_params=_cp(("arbitrary", "arbitrary"), 56),
        name="gla_state" if has_state else "gla_zero",
    )(*args)


MERGE_SLAB = (16, 1024)


def _merge_kernel(oa_ref, og_ref, gr_ref, ga_ref, gg_ref, o_ref):
    rows, cols = o_ref.shape
    for r0 in range(0, rows, MERGE_SLAB[0]):
        for c0 in range(0, cols, MERGE_SLAB[1]):
            sl = (slice(r0, r0 + MERGE_SLAB[0]), slice(c0, c0 + MERGE_SLAB[1]))
            gr = gr_ref[sl].astype(F32)
            o_gla = og_ref[sl] * (gr * _sigmoid(gr))
            merged = (_sigmoid(ga_ref[sl].astype(F32)) * oa_ref[sl].astype(F32)
                      + _sigmoid(gg_ref[sl].astype(F32)) * o_gla)
            o_ref[sl] = merged.astype(BF16)


MERGE_TM = 256


def _merge(o_att, og, gates):
    m = o_att.shape[0]
    blk = lambda cb: pl.BlockSpec((MERGE_TM, D_MODEL), lambda i: (i, cb))
    cols = ((OFF_GR - OFF_GR) // D_MODEL, (OFF_GATT - OFF_GR) // D_MODEL, (OFF_GGLA - OFF_GR) // D_MODEL)
    return pl.pallas_call(
        _merge_kernel,
        grid=(m // MERGE_TM,),
        in_specs=[blk(0), blk(0)] + [blk(cb) for cb in cols],
        out_specs=blk(0),
        out_shape=jax.ShapeDtypeStruct((m, D_MODEL), BF16),
        compiler_params=_cp(("arbitrary",), 40),
        name="branch_merge",
    )(o_att, og, gates, gates, gates)


def _merge_side(o_att, og, gates, rows):
    m = o_att.shape[0]
    cols = ((OFF_GR - OFF_GR) // D_MODEL, (OFF_GATT - OFF_GR) // D_MODEL, (OFF_GGLA - OFF_GR) // D_MODEL)
    return _Side(_merge_kernel, (o_att, og, gates, gates, gates), (0, 0) + cols,
                 (jax.ShapeDtypeStruct((m, D_MODEL), BF16),), rows, m // rows)


OUT_TM = 512
OUT_TN = 1024


def _out_kernel(mg_ref, w_ref, x_ref, gt_ref, g2_ref, sh_ref, sc_ref, x1_ref, h2_ref, x1_scr):
    j = pl.program_id(1)
    nj = D_MODEL // OUT_TN
    slab = MXU_ACC_ROWS * 512 // OUT_TN
    for r0 in range(0, OUT_TM, slab):
        rs = slice(r0, r0 + slab)
        x1 = x_ref[rs, :] + gt_ref[...] * _dot(mg_ref[rs, :], w_ref[...])
        x1_ref[rs, :] = x1
        x1_scr[j, rs, :] = x1

    @pl.when(j == nj - 1)
    def _():
        ssq = jnp.zeros((OUT_TM, 1), F32)
        for jj in range(nj):
            xs = x1_scr[jj]
            ssq = ssq + jnp.sum(xs * xs, axis=-1, keepdims=True)
        inv = lax.rsqrt(ssq / D_MODEL + EPS)
        for jj in range(nj):
            cs = slice(jj * OUT_TN, (jj + 1) * OUT_TN)
            y = x1_scr[jj] * inv * g2_ref[:, cs]
            h2_ref[:, cs] = (y * (1.0 + sc_ref[:, cs]) + sh_ref[:, cs]).astype(BF16)


def _out_proj(merged, w_out, x, gate1, g2, shift2, scale2, mod_off, rows_per_mod):
    m = x.shape[0]
    bpb = rows_per_mod // OUT_TM
    gate_spec = pl.BlockSpec((None, 1, OUT_TN), lambda i, j: (mod_off + i // bpb, 0, j))
    return pl.pallas_call(
        _out_kernel,
        grid=(m // OUT_TM, D_MODEL // OUT_TN),
        in_specs=[
            pl.BlockSpec((OUT_TM, D_MODEL), lambda i, j: (i, 0)),
            pl.BlockSpec((D_MODEL, OUT_TN), lambda i, j: (0, j)),
            pl.BlockSpec((OUT_TM, OUT_TN), lambda i, j: (i, j)),
            gate_spec,
            pl.BlockSpec((1, D_MODEL), lambda i, j: (0, 0)),
            _mod_spec(mod_off, bpb),
            _mod_spec(mod_off, bpb),
        ],
        out_specs=[
            pl.BlockSpec((OUT_TM, OUT_TN), lambda i, j: (i, j)),
            pl.BlockSpec((OUT_TM, D_MODEL), lambda i, j: (i, 0)),
        ],
        out_shape=[
            jax.ShapeDtypeStruct((m, D_MODEL), F32),
            jax.ShapeDtypeStruct((m, D_MODEL), BF16),
        ],
        scratch_shapes=[pltpu.VMEM((D_MODEL // OUT_TN, OUT_TM, OUT_TN), F32)],
        compiler_params=_cp(("arbitrary", "arbitrary"), 56),
        name="out_proj_residual_norm",
    )(merged, w_out, x, gate1, g2, shift2, scale2)


FFI_TM = 2048
FFI_TN = 256


def _ffn_in_kernel(h_ref, wg_ref, wu_ref, o_ref):
    wg = wg_ref[...].astype(BF16)
    wu = wu_ref[...].astype(BF16)
    slab = MXU_ACC_ROWS * 512 // FFI_TN
    for r0 in range(0, FFI_TM, slab):
        rs = slice(r0, r0 + slab)
        h = h_ref[rs, :]
        g = _dot(h, wg)
        u = _dot(h, wu)
        o_ref[rs, :] = (g * _sigmoid(g) * u).astype(BF16)


def _ffn_in(h2, w):
    m = h2.shape[0]
    return pl.pallas_call(
        _ffn_in_kernel,
        grid=(m // FFI_TM, D_FF // FFI_TN),
        in_specs=[
            pl.BlockSpec((FFI_TM, D_MODEL), lambda i, j: (i, 0)),
            pl.BlockSpec((D_MODEL, FFI_TN), lambda i, j: (0, j)),
            pl.BlockSpec((D_MODEL, FFI_TN), lambda i, j: (0, D_FF // FFI_TN + j)),
        ],
        out_specs=pl.BlockSpec((FFI_TM, FFI_TN), lambda i, j: (i, j)),
        out_shape=jax.ShapeDtypeStruct((m, D_FF), BF16),
        compiler_params=_cp(("arbitrary", "arbitrary"), 58),
        name="ffn_in_swiglu",
    )(h2, w, w)


FFO_TM = 512
FFO_TN = 512


def _ffn_out_kernel(a_ref, w_ref, x_ref, gt_ref, o_ref):
    o_ref[...] = x_ref[...] + gt_ref[...] * _dot(a_ref[...], w_ref[...])


def _ffn_out(act, w, x1, gate2, mod_off, rows_per_mod, side=None):
    m = x1.shape[0]
    bpb = rows_per_mod // FFO_TM
    in_specs = [
        pl.BlockSpec((FFO_TM, D_FF), lambda i, j: (i, 0)),
        pl.BlockSpec((D_FF, FFO_TN), lambda i, j: (0, j)),
        pl.BlockSpec((FFO_TM, FFO_TN), lambda i, j: (i, j)),
        pl.BlockSpec((None, 1, FFO_TN), lambda i, j: (mod_off + i // bpb, 0, j)),
    ]
    return _host_call(
        _ffn_out_kernel, (m // FFO_TM, D_MODEL // FFO_TN), in_specs,
        pl.BlockSpec((FFO_TM, FFO_TN), lambda i, j: (i, j)),
        jax.ShapeDtypeStruct((m, D_MODEL), F32), (act, w, x1, gate2), side, 60, "ffn_out_residual")


def _rope_tables(t):
    rows = t // GRID_W
    half = HEAD_DIM // 2
    row = jnp.repeat(jnp.arange(rows, dtype=F32), GRID_W)
    col = jnp.tile(jnp.arange(GRID_W, dtype=F32), rows)
    inv = ROPE_THETA ** (-jnp.arange(0, half, 2, dtype=F32) / half)
    ar = row[:, None] * inv[None, :]
    ac = col[:, None] * inv[None, :]
    cr, sr, cc, sc = jnp.cos(ar), jnp.sin(ar), jnp.cos(ac), jnp.sin(ac)
    z = jnp.zeros_like(sr)
    tab_c = jnp.concatenate([cr, cr, cc, cc], axis=1)
    tab_sa = jnp.concatenate([-sr, z, -sc, z], axis=1)
    tab_sb = jnp.concatenate([z, sr, z, sc], axis=1)
    return tab_c, tab_sa, tab_sb


def kernel(x_prompt, x_sample, c, cache_k, cache_v, state_gla_fwd, state_gla_bwd, c_ctx, w_ada, b_ada, norm1_g, norm2_g, w_in, q_norm_g, k_norm_g, attn_sink, w_a2_fwd, b_a_fwd, w_a2_bwd, b_a_bwd, gla_norm_g, w_out, w_ffn_in, w_ffn_out):
    assert w_ada.shape[0] == 1, "single trunk layer"
    cc = jnp.zeros((MOD_ROWS, D_MODEL), F32).at[0].set(c_ctx).at[1:1 + DEC_BATCH].set(c)
    mod_all = _ada(cc, w_ada[0], b_ada[0][None, :])
    mod = tuple(mod_all[:, i * D_MODEL:(i + 1) * D_MODEL].reshape(MOD_ROWS, 1, D_MODEL)
                for i in range(N_MOD))

    r = GLA_GATE_RANK
    wa = jnp.zeros((D_MODEL, LANES), BF16).at[:, :2 * r].set(w_in[0][:, D_WIDE:].astype(BF16))
    waf = jnp.zeros((LANES, GLA_DK), BF16).at[:r].set(w_a2_fwd[0].astype(BF16))
    wab = jnp.zeros((LANES, GLA_DK), BF16).at[r:2 * r].set(w_a2_bwd[0].astype(BF16))
    g1, g2 = norm1_g[0][None, :], norm2_g[0][None, :]
    w_in_t = w_in[0].T
    qg, kg, sink = q_norm_g[0][None, :], k_norm_g[0][None, :], attn_sink[0]
    gla_w = (waf, wab, b_a_fwd[0][None, :], b_a_bwd[0][None, :], gla_norm_g[0][None, :])
    shift1, scale1, gate1, shift2, scale2, gate2 = mod
    kvw = N_KV_HEADS * HEAD_DIM
    n_wide, n_gate = OFF_GR, D_WIDE - OFF_GR
    m_ctx, m_lat = BATCH * SEQ, DEC_BATCH * DEC_SEQ
    ctx_mod, lat_mod = (0, m_ctx), (1, DEC_SEQ)

    xp = x_prompt.reshape(m_ctx, D_MODEL)
    xs = x_sample.reshape(m_lat, D_MODEL)

    h_c, a_c = _norm_mod(xp, g1, shift1, scale1, wa, *ctx_mod)
    proj_c, w_fo = _proj(h_c, w_in_t, 0, n_wide, F32, _cast_side(w_ffn_out[0], 128))
    gates_c, w_o = _proj(h_c, w_in_t, n_wide, n_gate, BF16, _cast_side(w_out[0], 64))
    oatt_c, new_k, new_v = _ctx_attn(proj_c, sink, qg, kg)
    og_c, s_f, s_b = _gla(proj_c, a_c, *gla_w, SEQ)

    h_l, a_l = _norm_mod(xs, g1, shift1, scale1, wa, *lat_mod)
    merged_c = _merge(oatt_c, og_c, gates_c)
    proj_l = _proj(h_l, w_in_t, 0, n_wide, F32)
    gates_l = _proj(h_l, w_in_t, n_wide, n_gate, BF16)
    oatt_l = _lat_attn(
        proj_l,
        cache_k[:, 0].reshape(DEC_BATCH, PAST_LEN, kvw),
        cache_v[:, 0].reshape(DEC_BATCH, PAST_LEN, kvw),
        sink, qg, kg, *_rope_tables(DEC_SEQ))
    og_l = _gla(proj_l, a_l, *gla_w, DEC_SEQ,
                state_gla_fwd[:, 0].reshape(DEC_BATCH * GLA_HEADS, GLA_HK, GLA_HV),
                state_gla_bwd[:, 0].reshape(DEC_BATCH * GLA_HEADS, GLA_HK, GLA_HV))

    x1_c, h2_c = _out_proj(merged_c, w_o, xp, gate1, g2, shift2, scale2, *ctx_mod)
    act_c = _ffn_in(h2_c, w_ffn_in[0])
    yp, merged_l = _ffn_out(act_c, w_fo, x1_c, gate2, *ctx_mod,
                            side=_merge_side(oatt_l, og_l, gates_l, 64))

    x1_l, h2_l = _out_proj(merged_l, w_o, xs, gate1, g2, shift2, scale2, *lat_mod)
    act_l = _ffn_in(h2_l, w_ffn_in[0])
    ys = _ffn_out(act_l, w_fo, x1_l, gate2, *lat_mod)
    return (
        yp.reshape(BATCH, SEQ, D_MODEL),
        ys.reshape(DEC_BATCH, DEC_SEQ, D_MODEL),
        new_k.reshape(BATCH, 1, SEQ, N_KV_HEADS, HEAD_DIM),
        new_v.reshape(BATCH, 1, SEQ, N_KV_HEADS, HEAD_DIM),
        s_f.reshape(BATCH, 1, GLA_HEADS, GLA_HK, GLA_HV),
        s_b.reshape(BATCH, 1, GLA_HEADS, GLA_HK, GLA_HV),
    )
```

```python
import functools
from typing import Callable, NamedTuple

import jax
import jax.numpy as jnp
from jax import lax
from jax.experimental import pallas as pl
from jax.experimental.pallas import tpu as pltpu

F32 = jnp.float32
BF16 = jnp.bfloat16

D_MODEL = 4096
BATCH = 32
SEQ = 256
DEC_BATCH = 8
DEC_SEQ = 1024
PAST_LEN = 256
GRID_W = 64
HEAD_DIM = 128
N_Q_HEADS = 32
N_KV_HEADS = 8
GQA_GROUP = N_Q_HEADS // N_KV_HEADS
WINDOW = 128
BLOCK = 128
ROPE_THETA = 10000.0
GLA_HEADS = 4
GLA_DK = D_MODEL // 2
GLA_DV = D_MODEL
GLA_HK = GLA_DK // GLA_HEADS
GLA_HV = GLA_DV // GLA_HEADS
GLA_GATE_RANK = 16
GLA_TAU = 16.0
GLA_CHUNK = 64
GLA_SUPER = 256
D_FF = 11008
N_MOD = 6
EPS = 1e-6

OFF_Q = 0
OFF_K = 4096
OFF_V = 5120
OFF_GQ = 6144
OFF_GK = 8192
OFF_GV = 10240
OFF_GR = 14336
OFF_GATT = 18432
OFF_GGLA = 22528
D_WIDE = 26624
LANES = 128
MXU_ACC_ROWS = 512
MOD_ROWS = 16

MIB = 1024 * 1024
VMEM_BUDGET = 60 * MIB
VMEM_SLACK = 2 * MIB
NT_DIMS = (((1,), (1,)), ((), ()))
TN_DIMS = (((0,), (0,)), ((), ()))


def _cp(sem, vmem_mib):
    return pltpu.CompilerParams(dimension_semantics=sem, vmem_limit_bytes=vmem_mib * MIB)


def _rms(x, g):
    ms = jnp.mean(x * x, axis=-1, keepdims=True)
    return x * lax.rsqrt(ms + EPS) * g


def _sigmoid(x):
    return 0.5 * jnp.tanh(0.5 * x) + 0.5


def _dot(a, b):
    return jnp.dot(a, b, preferred_element_type=F32)


ADA_TN = 512


def _ada_kernel(c_ref, w_ref, b_ref, o_ref):
    c = c_ref[...]
    s = (c * _sigmoid(c)).astype(BF16)
    o_ref[...] = _dot(s, w_ref[...].astype(BF16)) + b_ref[...]


def _ada(cc, w_ada, b_ada):
    n = w_ada.shape[1]
    return pl.pallas_call(
        _ada_kernel,
        grid=(n // ADA_TN,),
        in_specs=[
            pl.BlockSpec((MOD_ROWS, D_MODEL), lambda j: (0, 0)),
            pl.BlockSpec((D_MODEL, ADA_TN), lambda j: (0, j)),
            pl.BlockSpec((1, ADA_TN), lambda j: (0, j)),
        ],
        out_specs=pl.BlockSpec((MOD_ROWS, ADA_TN), lambda j: (0, j)),
        out_shape=jax.ShapeDtypeStruct((MOD_ROWS, n), F32),
        compiler_params=_cp(("arbitrary",), 40),
        name="ada_ln",
    )(cc, w_ada, b_ada)


NORM_TM = 512
PROJ_TM = 2048
PROJ_TN = 512


def _mod_spec(mod_off, bpb):
    return pl.BlockSpec((None, 1, D_MODEL), lambda i, *_: (mod_off + i // bpb, 0, 0))


def _norm_kernel(x_ref, g_ref, sh_ref, sc_ref, wa_ref, h_ref, a_ref):
    h = _rms(x_ref[...], g_ref[...]) * (1.0 + sc_ref[...]) + sh_ref[...]
    hb = h.astype(BF16)
    h_ref[...] = hb
    a_ref[...] = _dot(hb, wa_ref[...])


def _norm_mod(x, g1, shift, scale, wa, mod_off, rows_per_mod):
    m = x.shape[0]
    bpb = rows_per_mod // NORM_TM
    return pl.pallas_call(
        _norm_kernel,
        grid=(m // NORM_TM,),
        in_specs=[
            pl.BlockSpec((NORM_TM, D_MODEL), lambda i: (i, 0)),
            pl.BlockSpec((1, D_MODEL), lambda i: (0, 0)),
            _mod_spec(mod_off, bpb),
            _mod_spec(mod_off, bpb),
            pl.BlockSpec((D_MODEL, LANES), lambda i: (0, 0)),
        ],
        out_specs=[
            pl.BlockSpec((NORM_TM, D_MODEL), lambda i: (i, 0)),
            pl.BlockSpec((NORM_TM, LANES), lambda i: (i, 0)),
        ],
        out_shape=[
            jax.ShapeDtypeStruct((m, D_MODEL), BF16),
            jax.ShapeDtypeStruct((m, LANES), F32),
        ],
        compiler_params=_cp(("arbitrary",), 40),
        name="norm_modulate",
    )(x, g1, shift, scale, wa)


class _Side(NamedTuple):
    fn: Callable
    ins: tuple
    in_cols: tuple
    outs: tuple
    rows: int
    nblk: int

    def vmem_bytes(self):
        width = self.outs[0].shape[1]
        per_row = sum(a.dtype.itemsize for a in self.ins) + sum(o.dtype.itemsize for o in self.outs)
        return 2 * self.rows * width * per_row

    def specs(self, nj):
        def spec(shape_cols, cb):
            return pl.BlockSpec((self.rows, shape_cols),
                                lambda i, j: (jnp.minimum(i * nj + j, self.nblk - 1), cb))
        in_specs = [spec(self.outs[0].shape[1], cb) for cb in self.in_cols]
        out_specs = [spec(o.shape[1], 0) for o in self.outs]
        return in_specs, out_specs


def _host_kernel(body, n_in, n_out, side):
    def kern(*refs):
        n_sin = len(side.ins) if side else 0
        if side:
            side.fn(*refs[n_in:n_in + n_sin], *refs[n_in + n_sin + n_out:])
        body(*refs[:n_in], *refs[n_in + n_sin:n_in + n_sin + n_out])
    return kern


def _host_call(body, grid, in_specs, out_spec, out_shape, args, side, vmem_mib, name):
    if side is None:
        s_in, s_out = [], []
    else:
        assert grid[0] * grid[1] >= side.nblk
        s_in, s_out = side.specs(grid[1])
    res = pl.pallas_call(
        _host_kernel(body, len(in_specs), 1, side),
        grid=grid,
        in_specs=in_specs + s_in,
        out_specs=[out_spec] + s_out,
        out_shape=[out_shape] + (list(side.outs) if side else []),
        compiler_params=_cp(("arbitrary", "arbitrary"), vmem_mib),
        name=name,
    )(*args, *(side.ins if side else ()))
    return res[0] if side is None else res


def _cast_kernel(src_ref, dst_ref):
    dst_ref[...] = src_ref[...].astype(BF16)


def _cast_side(w, rows):
    n, d = w.shape
    return _Side(_cast_kernel, (w,), (0,), (jax.ShapeDtypeStruct((n, d), BF16),), rows, n // rows)


def _proj_kernel(h_ref, wt_ref, o_ref):
    wt = wt_ref[...].astype(BF16)
    for r0 in range(0, PROJ_TM, MXU_ACC_ROWS):
        rs = slice(r0, r0 + MXU_ACC_ROWS)
        o_ref[rs, :] = lax.dot_general(h_ref[rs, :], wt, NT_DIMS,
                                       preferred_element_type=F32).astype(o_ref.dtype)


def _proj(h, wt, col0, ncols, out_dtype, side=None):
    m = h.shape[0]
    j0 = col0 // PROJ_TN
    fixed = 2 * PROJ_TN * D_MODEL * 4 + 2 * PROJ_TM * PROJ_TN * jnp.dtype(out_dtype).itemsize
    fixed += side.vmem_bytes() if side else 0
    h_tile = PROJ_TM * D_MODEL * h.dtype.itemsize
    h_bufs = 2 if fixed + 2 * h_tile + VMEM_SLACK <= VMEM_BUDGET else 1
    in_specs = [
        pl.BlockSpec((PROJ_TM, D_MODEL), lambda i, j: (i, 0), pipeline_mode=pl.Buffered(h_bufs)),
        pl.BlockSpec((PROJ_TN, D_MODEL), lambda i, j: (j0 + j, 0)),
    ]
    return _host_call(
        _proj_kernel, (m // PROJ_TM, ncols // PROJ_TN), in_specs,
        pl.BlockSpec((PROJ_TM, PROJ_TN), lambda i, j: (i, j)),
        jax.ShapeDtypeStruct((m, ncols), out_dtype), (h, wt), side, VMEM_BUDGET // MIB, "in_proj")


def _sink_col(sink_ref, h, rows_per_head):
    rid = lax.broadcasted_iota(jnp.int32, (GQA_GROUP * rows_per_head, 1), 0) // rows_per_head
    col = jnp.full(rid.shape, sink_ref[h * GQA_GROUP], F32)
    for g in range(1, GQA_GROUP):
        col = jnp.where(rid == g, sink_ref[h * GQA_GROUP + g], col)
    return col


def _ctx_attn_kernel(sink_ref, q_ref, k_ref, v_ref, qg_ref, kg_ref, o_ref, nk_ref, nv_ref):
    scale = HEAD_DIM ** -0.5
    nv_ref[...] = v_ref[...]
    for h in range(N_KV_HEADS):
        hs = slice(h * HEAD_DIM, (h + 1) * HEAD_DIM)
        kn = _rms(k_ref[:, hs], kg_ref[...])
        nk_ref[:, hs] = kn
        kb = kn.astype(BF16)
        vb = v_ref[:, hs].astype(BF16)
        qs = []
        for g in range(GQA_GROUP):
            c0 = (h * GQA_GROUP + g) * HEAD_DIM
            qs.append(_rms(q_ref[:, c0:c0 + HEAD_DIM], qg_ref[...]).astype(BF16))
        q4 = jnp.concatenate(qs, axis=0)
        s = lax.dot_general(q4, kb, NT_DIMS, preferred_element_type=F32) * scale
        sk = _sink_col(sink_ref, h, SEQ)
        m = jnp.maximum(jnp.max(s, axis=1, keepdims=True), sk)
        p = jnp.exp(s - m)
        den = jnp.sum(p, axis=1, keepdims=True) + jnp.exp(sk - m)
        o = _dot(p.astype(BF16), vb) / den
        for g in range(GQA_GROUP):
            c0 = (h * GQA_GROUP + g) * HEAD_DIM
            o_ref[:, c0:c0 + HEAD_DIM] = o[g * SEQ:(g + 1) * SEQ, :].astype(BF16)


def _ctx_attn(proj, sink, qg, kg):
    m = proj.shape[0]
    kvw = N_KV_HEADS * HEAD_DIM
    return pl.pallas_call(
        _ctx_attn_kernel,
        grid=(m // SEQ,),
        in_specs=[
            pl.BlockSpec(memory_space=pltpu.SMEM),
            pl.BlockSpec((SEQ, D_MODEL), lambda b: (b, OFF_Q // D_MODEL)),
            pl.BlockSpec((SEQ, kvw), lambda b: (b, OFF_K // kvw)),
            pl.BlockSpec((SEQ, kvw), lambda b: (b, OFF_V // kvw)),
            pl.BlockSpec((1, HEAD_DIM), lambda b: (0, 0)),
            pl.BlockSpec((1, HEAD_DIM), lambda b: (0, 0)),
        ],
        out_specs=[
            pl.BlockSpec((SEQ, D_MODEL), lambda b: (b, 0)),
            pl.BlockSpec((SEQ, kvw), lambda b: (b, 0)),
            pl.BlockSpec((SEQ, kvw), lambda b: (b, 0)),
        ],
        out_shape=[
            jax.ShapeDtypeStruct((m, D_MODEL), BF16),
            jax.ShapeDtypeStruct((m, kvw), F32),
            jax.ShapeDtypeStruct((m, kvw), F32),
        ],
        compiler_params=_cp(("arbitrary",), 40),
        name="ctx_attention",
    )(sink, proj, proj, proj, qg, kg)


def _rope(x, c_ref, sa_ref, sb_ref):
    up = pltpu.roll(x, HEAD_DIM - 32, axis=1)
    dn = pltpu.roll(x, 32, axis=1)
    return x * c_ref[...] + up * sa_ref[...] + dn * sb_ref[...]


def _lat_attn_kernel(sink_ref, q_ref, k_ref, v_ref, ck_ref, cv_ref, qg_ref, kg_ref,
                     c_ref, sa_ref, sb_ref, o_ref, q_scr, k_scr, v_scr):
    t = DEC_SEQ
    scale = HEAD_DIM ** -0.5
    h = pl.program_id(1)
    nb = t // BLOCK
    kr = _rope(_rms(k_ref[...], kg_ref[...]), c_ref, sa_ref, sb_ref)
    k_scr[...] = kr.astype(BF16)
    v_scr[...] = v_ref[...].astype(BF16)
    for g in range(GQA_GROUP):
        qn = _rms(q_ref[:, g * HEAD_DIM:(g + 1) * HEAD_DIM], qg_ref[...])
        q_scr[g] = _rope(qn, c_ref, sa_ref, sb_ref).astype(BF16)
    ckb = ck_ref[...].astype(BF16)
    cvb = cv_ref[...].astype(BF16)
    sk = _sink_col(sink_ref, h, BLOCK)
    rows = GQA_GROUP * BLOCK
    qi = lax.broadcasted_iota(jnp.int32, (rows, BLOCK), 0) % BLOCK
    kj = lax.broadcasted_iota(jnp.int32, (rows, BLOCK), 1)
    prev_ok = kj >= qi
    next_ok = kj <= qi
    neg = jnp.finfo(F32).min

    def attend(n, has_prev, has_next):
        if isinstance(n, int):
            r0, k0 = n * BLOCK, (n - has_prev) * BLOCK
        else:
            r0 = pl.multiple_of(n * BLOCK, BLOCK)
            k0 = pl.multiple_of((n - has_prev) * BLOCK, BLOCK)
        nk = (1 + has_prev + has_next) * BLOCK
        q4 = jnp.concatenate([q_scr[g, pl.ds(r0, BLOCK), :] for g in range(GQA_GROUP)], axis=0)
        s_win = lax.dot_general(q4, k_scr[pl.ds(k0, nk), :], NT_DIMS,
                                preferred_element_type=F32) * scale
        tiles = [s_win[:, i * BLOCK:(i + 1) * BLOCK] for i in range(nk // BLOCK)]
        if has_prev:
            tiles[0] = jnp.where(prev_ok, tiles[0], neg)
        if has_next:
            tiles[-1] = jnp.where(next_ok, tiles[-1], neg)
        s_win = jnp.concatenate(tiles, axis=1)
        s_ctx = lax.dot_general(q4, ckb, NT_DIMS, preferred_element_type=F32) * scale
        m = jnp.maximum(jnp.max(s_win, axis=1, keepdims=True),
                        jnp.max(s_ctx, axis=1, keepdims=True))
        m = jnp.maximum(m, sk)
        p_win = jnp.exp(s_win - m)
        p_ctx = jnp.exp(s_ctx - m)
        den = (jnp.sum(p_win, axis=1, keepdims=True) + jnp.sum(p_ctx, axis=1, keepdims=True)
               + jnp.exp(sk - m))
        o = (_dot(p_win.astype(BF16), v_scr[pl.ds(k0, nk), :]) + _dot(p_ctx.astype(BF16), cvb)) / den
        for g in range(GQA_GROUP):
            o_ref[pl.ds(r0, BLOCK), g * HEAD_DIM:(g + 1) * HEAD_DIM] = (
                o[g * BLOCK:(g + 1) * BLOCK, :].astype(BF16))

    def interior(n, carry):
        attend(n, True, True)
        return carry

    attend(0, False, True)
    lax.fori_loop(1, nb - 1, interior, 0, unroll=3)
    attend(nb - 1, True, False)


def _lat_attn(proj, ck, cv, sink, qg, kg, rope_c, rope_sa, rope_sb):
    m = proj.shape[0]
    t = DEC_SEQ
    gw = GQA_GROUP * HEAD_DIM
    tab = pl.BlockSpec((t, HEAD_DIM), lambda b, h: (0, 0))
    vec = pl.BlockSpec((1, HEAD_DIM), lambda b, h: (0, 0))
    cache = pl.BlockSpec((None, PAST_LEN, HEAD_DIM), lambda b, h: (b, 0, h))
    return pl.pallas_call(
        _lat_attn_kernel,
        grid=(m // t, N_KV_HEADS),
        in_specs=[
            pl.BlockSpec(memory_space=pltpu.SMEM),
            pl.BlockSpec((t, gw), lambda b, h: (b, OFF_Q // gw + h)),
            pl.BlockSpec((t, HEAD_DIM), lambda b, h: (b, OFF_K // HEAD_DIM + h)),
            pl.BlockSpec((t, HEAD_DIM), lambda b, h: (b, OFF_V // HEAD_DIM + h)),
            cache, cache, vec, vec, tab, tab, tab,
        ],
        out_specs=pl.BlockSpec((t, gw), lambda b, h: (b, h)),
        out_shape=jax.ShapeDtypeStruct((m, D_MODEL), BF16),
        scratch_shapes=[
            pltpu.VMEM((GQA_GROUP, t, HEAD_DIM), BF16),
            pltpu.VMEM((t, HEAD_DIM), BF16),
            pltpu.VMEM((t, HEAD_DIM), BF16),
        ],
        compiler_params=_cp(("arbitrary", "arbitrary"), 40),
        name="latent_attention",
    )(sink, proj, proj, proj, ck, cv, qg, kg, rope_c, rope_sa, rope_sb)


def _log_sigmoid(x):
    return jnp.minimum(x, 0.0) - jnp.log(1.0 + jnp.exp(-jnp.abs(x)))


def _split2(x):
    hi = x.astype(BF16)
    lo = (x - hi.astype(F32)).astype(BF16)
    return hi, lo


def _gla_kernel(*refs, t, has_state):
    if has_state:
        (q_ref, k_ref, v_ref, a_ref, waf_ref, wab_ref, baf_ref, bab_ref, gn_ref,
         s0f_ref, s0b_ref, og_ref, *scr) = refs
    else:
        (q_ref, k_ref, v_ref, a_ref, waf_ref, wab_ref, baf_ref, bab_ref, gn_ref,
         og_ref, sf_out, sb_out, *scr) = refs
    stf, stb, vb, ob = scr[:4]
    qsf, kuf, decf, *tmpf = scr[4:13]
    qsb, kub, decb, *tmpb = scr[13:]
    c = GLA_CHUNK
    sc = GLA_SUPER
    nsub = sc // c
    nsc = t // sc
    vb[...] = v_ref[...].astype(BF16)

    row = lax.broadcasted_iota(jnp.int32, (sc, sc), 0)
    col = lax.broadcasted_iota(jnp.int32, (sc, sc), 1)
    rc, cc = row // c, col // c

    def intra(si, fwd):
        r0 = pl.multiple_of(si * sc, sc)
        w_ref, b_ref, qs_scr, ku_scr, dec_scr, tmp, dst = (
            (waf_ref, baf_ref, qsf, kuf, decf, tmpf, og_ref) if fwd
            else (wab_ref, bab_ref, qsb, kub, decb, tmpb, ob))
        qd, kd, ke, q2, q3, amat = tmp
        tri = jnp.where((col <= row) if fwd else (col >= row), 1.0, 0.0).astype(BF16)
        dist = (rc - cc) if fwd else (cc - rc)
        x = _dot(a_ref[pl.ds(r0, sc), :].astype(BF16), w_ref[...]) + b_ref[...]
        hi, lo = _split2(_log_sigmoid(x) / GLA_TAU)
        cum = _dot(tri, hi) + _dot(tri, lo)
        zero = jnp.zeros((1, GLA_HK), F32)

        def at_start(j):
            if fwd:
                return cum[j * c - 1:j * c, :] if j > 0 else zero
            return cum[(j + 1) * c:(j + 1) * c + 1, :] if j < nsub - 1 else zero

        def at_end(j):
            return cum[(j + 1) * c - 1:(j + 1) * c, :] if fwd else cum[j * c:j * c + 1, :]

        total = at_end(nsub - 1) if fwd else at_end(0)
        dec_scr[pl.ds(pl.multiple_of(si * 8, 8), 8), :] = jnp.broadcast_to(jnp.exp(total), (8, GLA_HK))
        for j in range(nsub):
            rs = slice(j * c, (j + 1) * c)
            rows = pl.ds(pl.multiple_of(r0 + j * c, c), c)
            cj = cum[rs, :]
            cs, ce = at_start(j), at_end(j)
            q = q_ref[rows, :] * (GLA_HK ** -0.5)
            k = k_ref[rows, :]
            qdj = q * jnp.exp(cj - cs)
            kej = k * jnp.exp(ce - cj)
            qd[rs, :] = qdj.astype(BF16)
            kd[rs, :] = (k * jnp.exp(cs - cj)).astype(BF16)
            ke[rs, :] = kej.astype(BF16)
            qs_scr[rows, :] = (qdj * jnp.exp(cs)).astype(BF16)
            ku_scr[rows, :] = (kej * jnp.exp(total - ce)).astype(BF16)
            p2 = j - 2 if fwd else j + 2
            if 0 <= p2 < nsub:
                l2 = (j - 2) if fwd else j
                q2[l2 * c:(l2 + 1) * c, :] = (qdj * jnp.exp(cs - at_end(p2))).astype(BF16)
            p3 = j - 3 if fwd else j + 3
            if 0 <= p3 < nsub:
                q3[...] = (qdj * jnp.exp(cs - at_end(p3))).astype(BF16)
        nt = lambda a, b: lax.dot_general(a, b, NT_DIMS, preferred_element_type=F32)
        tril = (col <= row) if fwd else (col >= row)
        amat[...] = (jnp.where((dist == 0) & tril, nt(qd[...], kd[...]), 0.0)
                     + jnp.where(dist == 1, nt(qd[...], ke[...]), 0.0))
        r2 = slice(2 * c, sc) if fwd else slice(0, 2 * c)
        amat[r2, :] += jnp.where(dist[r2, :] == 2, nt(q2[...], ke[...]), 0.0)
        r3 = slice(3 * c, sc) if fwd else slice(0, c)
        amat[r3, :] += jnp.where(dist[r3, :] == 3, nt(q3[...], ke[...]), 0.0)
        dst[pl.ds(r0, sc), :] = _dot(amat[...].astype(BF16), vb[pl.ds(r0, sc), :])

    def intra_body(i, carry):
        intra(i, True)
        intra(nsc - 1 - i, False)
        return carry

    lax.fori_loop(0, nsc, intra_body, 0, unroll=min(nsc, 2))

    if has_state:
        for i in range(nsc):
            for fwd in (True, False):
                si = i if fwd else nsc - 1 - i
                rows = slice(si * sc, (si + 1) * sc)
                st_ref, s0_ref, qs_scr, ku_scr, dec_scr, dst = (
                    (stf, s0f_ref, qsf, kuf, decf, og_ref) if fwd
                    else (stb, s0b_ref, qsb, kub, decb, ob))
                st = s0_ref[...] if i == 0 else st_ref[...]
                dst[rows, :] += _dot(qs_scr[rows, :], st.astype(BF16))
                if i < nsc - 1:
                    dec = jnp.broadcast_to(dec_scr[si * 8:si * 8 + 1, :], (LANES, GLA_HK)).T
                    dec = jnp.concatenate([dec] * (GLA_HV // LANES), axis=1)
                    st_ref[...] = st * dec + lax.dot_general(
                        ku_scr[rows, :], vb[rows, :], TN_DIMS, preferred_element_type=F32)
    else:
        sf_out[...] = lax.dot_general(kuf[...], vb[...], TN_DIMS, preferred_element_type=F32)
        sb_out[...] = lax.dot_general(kub[...], vb[...], TN_DIMS, preferred_element_type=F32)
    og_ref[...] = _rms(og_ref[...] + ob[...], gn_ref[...])


def _gla(proj, aproj, waf, wab, baf, bab, gn, t, s0f=None, s0b=None):
    m = proj.shape[0]
    nb = m // t
    has_state = s0f is not None
    assert t % GLA_SUPER == 0 and (has_state or t == GLA_SUPER)
    sc, c = GLA_SUPER, GLA_CHUNK
    st_shape_vmem = (GLA_HK, GLA_HV) if has_state else (8, LANES)
    per_direction = [
        pltpu.VMEM((t, GLA_HK), BF16),
        pltpu.VMEM((t, GLA_HK), BF16),
        pltpu.VMEM((8 * (t // sc), GLA_HK), F32),
        pltpu.VMEM((sc, GLA_HK), BF16),
        pltpu.VMEM((sc, GLA_HK), BF16),
        pltpu.VMEM((sc, GLA_HK), BF16),
        pltpu.VMEM((sc - 2 * c, GLA_HK), BF16),
        pltpu.VMEM((sc - 3 * c, GLA_HK), BF16),
        pltpu.VMEM((sc, sc), F32),
    ]
    st_spec = pl.BlockSpec((None, GLA_HK, GLA_HV), lambda b, h: (b * GLA_HEADS + h, 0, 0))
    in_specs = [
        pl.BlockSpec((t, GLA_HK), lambda b, h: (b, OFF_GQ // GLA_HK + h)),
        pl.BlockSpec((t, GLA_HK), lambda b, h: (b, OFF_GK // GLA_HK + h)),
        pl.BlockSpec((t, GLA_HV), lambda b, h: (b, OFF_GV // GLA_HV + h)),
        pl.BlockSpec((t, LANES), lambda b, h: (b, 0)),
        pl.BlockSpec((LANES, GLA_HK), lambda b, h: (0, h)),
        pl.BlockSpec((LANES, GLA_HK), lambda b, h: (0, h)),
        pl.BlockSpec((1, GLA_HK), lambda b, h: (0, h)),
        pl.BlockSpec((1, GLA_HK), lambda b, h: (0, h)),
        pl.BlockSpec((1, GLA_HV), lambda b, h: (0, h)),
    ]
    args = [proj, proj, proj, aproj, waf, wab, baf, bab, gn]
    og_spec = pl.BlockSpec((t, GLA_HV), lambda b, h: (b, h))
    og_shape = jax.ShapeDtypeStruct((m, GLA_DV), F32)
    if has_state:
        in_specs += [st_spec, st_spec]
        args += [s0f, s0b]
        out_specs, out_shape = og_spec, og_shape
    else:
        st_shape = jax.ShapeDtypeStruct((nb * GLA_HEADS, GLA_HK, GLA_HV), F32)
        out_specs, out_shape = [og_spec, st_spec, st_spec], [og_shape, st_shape, st_shape]
    return pl.pallas_call(
        functools.partial(_gla_kernel, t=t, has_state=has_state),
        grid=(nb, GLA_HEADS),
        in_specs=in_specs,
        out_specs=out_specs,
        out_shape=out_shape,
        scratch_shapes=[
            pltpu.VMEM(st_shape_vmem, F32),
            pltpu.VMEM(st_shape_vmem, F32),
            pltpu.VMEM((t, GLA_HV), BF16),
            pltpu.VMEM((t, GLA_HV), F32),
        ] + 2 * per_direction,
        compiler_params=_cp(("arbitrary", "arbitrary"), 56),
        name="gla_state" if has_state else "gla_zero",
    )(*args)


MERGE_SLAB = (16, 1024)


def _merge_kernel(oa_ref, og_ref, gr_ref, ga_ref, gg_ref, o_ref):
    rows, cols = o_ref.shape
    for r0 in range(0, rows, MERGE_SLAB[0]):
        for c0 in range(0, cols, MERGE_SLAB[1]):
            sl = (slice(r0, r0 + MERGE_SLAB[0]), slice(c0, c0 + MERGE_SLAB[1]))
            gr = gr_ref[sl].astype(F32)
            o_gla = og_ref[sl] * (gr * _sigmoid(gr))
            merged = (_sigmoid(ga_ref[sl].astype(F32)) * oa_ref[sl].astype(F32)
                      + _sigmoid(gg_ref[sl].astype(F32)) * o_gla)
            o_ref[sl] = merged.astype(BF16)


MERGE_TM = 256


def _merge(o_att, og, gates):
    m = o_att.shape[0]
    blk = lambda cb: pl.BlockSpec((MERGE_TM, D_MODEL), lambda i: (i, cb))
    cols = ((OFF_GR - OFF_GR) // D_MODEL, (OFF_GATT - OFF_GR) // D_MODEL, (OFF_GGLA - OFF_GR) // D_MODEL)
    return pl.pallas_call(
        _merge_kernel,
        grid=(m // MERGE_TM,),
        in_specs=[blk(0), blk(0)] + [blk(cb) for cb in cols],
        out_specs=blk(0),
        out_shape=jax.ShapeDtypeStruct((m, D_MODEL), BF16),
        compiler_params=_cp(("arbitrary",), 40),
        name="branch_merge",
    )(o_att, og, gates, gates, gates)


def _merge_side(o_att, og, gates, rows):
    m = o_att.shape[0]
    cols = ((OFF_GR - OFF_GR) // D_MODEL, (OFF_GATT - OFF_GR) // D_MODEL, (OFF_GGLA - OFF_GR) // D_MODEL)
    return _Side(_merge_kernel, (o_att, og, gates, gates, gates), (0, 0) + cols,
                 (jax.ShapeDtypeStruct((m, D_MODEL), BF16),), rows, m // rows)


OUT_TM = 512
OUT_TN = 1024


def _out_kernel(mg_ref, w_ref, x_ref, gt_ref, g2_ref, sh_ref, sc_ref, x1_ref, h2_ref, x1_scr):
    j = pl.program_id(1)
    nj = D_MODEL // OUT_TN
    slab = MXU_ACC_ROWS * 512 // OUT_TN
    for r0 in range(0, OUT_TM, slab):
        rs = slice(r0, r0 + slab)
        x1 = x_ref[rs, :] + gt_ref[...] * _dot(mg_ref[rs, :], w_ref[...])
        x1_ref[rs, :] = x1
        x1_scr[j, rs, :] = x1

    @pl.when(j == nj - 1)
    def _():
        ssq = jnp.zeros((OUT_TM, 1), F32)
        for jj in range(nj):
            xs = x1_scr[jj]
            ssq = ssq + jnp.sum(xs * xs, axis=-1, keepdims=True)
        inv = lax.rsqrt(ssq / D_MODEL + EPS)
        for jj in range(nj):
            cs = slice(jj * OUT_TN, (jj + 1) * OUT_TN)
            y = x1_scr[jj] * inv * g2_ref[:, cs]
            h2_ref[:, cs] = (y * (1.0 + sc_ref[:, cs]) + sh_ref[:, cs]).astype(BF16)


def _out_proj(merged, w_out, x, gate1, g2, shift2, scale2, mod_off, rows_per_mod):
    m = x.shape[0]
    bpb = rows_per_mod // OUT_TM
    gate_spec = pl.BlockSpec((None, 1, OUT_TN), lambda i, j: (mod_off + i // bpb, 0, j))
    return pl.pallas_call(
        _out_kernel,
        grid=(m // OUT_TM, D_MODEL // OUT_TN),
        in_specs=[
            pl.BlockSpec((OUT_TM, D_MODEL), lambda i, j: (i, 0)),
            pl.BlockSpec((D_MODEL, OUT_TN), lambda i, j: (0, j)),
            pl.BlockSpec((OUT_TM, OUT_TN), lambda i, j: (i, j)),
            gate_spec,
            pl.BlockSpec((1, D_MODEL), lambda i, j: (0, 0)),
            _mod_spec(mod_off, bpb),
            _mod_spec(mod_off, bpb),
        ],
        out_specs=[
            pl.BlockSpec((OUT_TM, OUT_TN), lambda i, j: (i, j)),
            pl.BlockSpec((OUT_TM, D_MODEL), lambda i, j: (i, 0)),
        ],
        out_shape=[
            jax.ShapeDtypeStruct((m, D_MODEL), F32),
            jax.ShapeDtypeStruct((m, D_MODEL), BF16),
        ],
        scratch_shapes=[pltpu.VMEM((D_MODEL // OUT_TN, OUT_TM, OUT_TN), F32)],
        compiler_params=_cp(("arbitrary", "arbitrary"), 56),
        name="out_proj_residual_norm",
    )(merged, w_out, x, gate1, g2, shift2, scale2)


FFI_TM = 2048
FFI_TN = 256


def _ffn_in_kernel(h_ref, wg_ref, wu_ref, o_ref):
    wg = wg_ref[...].astype(BF16)
    wu = wu_ref[...].astype(BF16)
    slab = MXU_ACC_ROWS * 512 // FFI_TN
    for r0 in range(0, FFI_TM, slab):
        rs = slice(r0, r0 + slab)
        h = h_ref[rs, :]
        g = _dot(h, wg)
        u = _dot(h, wu)
        o_ref[rs, :] = (g * _sigmoid(g) * u).astype(BF16)


def _ffn_in(h2, w):
    m = h2.shape[0]
    return pl.pallas_call(
        _ffn_in_kernel,
        grid=(m // FFI_TM, D_FF // FFI_TN),
        in_specs=[
            pl.BlockSpec((FFI_TM, D_MODEL), lambda i, j: (i, 0)),
            pl.BlockSpec((D_MODEL, FFI_TN), lambda i, j: (0, j)),
            pl.BlockSpec((D_MODEL, FFI_TN), lambda i, j: (0, D_FF // FFI_TN + j)),
        ],
        out_specs=pl.BlockSpec((FFI_TM, FFI_TN), lambda i, j: (i, j)),
        out_shape=jax.ShapeDtypeStruct((m, D_FF), BF16),
        compiler_params=_cp(("arbitrary", "arbitrary"), 58),
        name="ffn_in_swiglu",
    )(h2, w, w)


FFO_TM = 512
FFO_TN = 512


def _ffn_out_kernel(a_ref, w_ref, x_ref, gt_ref, o_ref):
    o_ref[...] = x_ref[...] + gt_ref[...] * _dot(a_ref[...], w_ref[...])


def _ffn_out(act, w, x1, gate2, mod_off, rows_per_mod, side=None):
    m = x1.shape[0]
    bpb = rows_per_mod // FFO_TM
    in_specs = [
        pl.BlockSpec((FFO_TM, D_FF), lambda i, j: (i, 0)),
        pl.BlockSpec((D_FF, FFO_TN), lambda i, j: (0, j)),
        pl.BlockSpec((FFO_TM, FFO_TN), lambda i, j: (i, j)),
        pl.BlockSpec((None, 1, FFO_TN), lambda i, j: (mod_off + i // bpb, 0, j)),
    ]
    return _host_call(
        _ffn_out_kernel, (m // FFO_TM, D_MODEL // FFO_TN), in_specs,
        pl.BlockSpec((FFO_TM, FFO_TN), lambda i, j: (i, j)),
        jax.ShapeDtypeStruct((m, D_MODEL), F32), (act, w, x1, gate2), side, 60, "ffn_out_residual")


def _rope_tables(t):
    rows = t // GRID_W
    half = HEAD_DIM // 2
    row = jnp.repeat(jnp.arange(rows, dtype=F32), GRID_W)
    col = jnp.tile(jnp.arange(GRID_W, dtype=F32), rows)
    inv = ROPE_THETA ** (-jnp.arange(0, half, 2, dtype=F32) / half)
    ar = row[:, None] * inv[None, :]
    ac = col[:, None] * inv[None, :]
    cr, sr, cc, sc = jnp.cos(ar), jnp.sin(ar), jnp.cos(ac), jnp.sin(ac)
    z = jnp.zeros_like(sr)
    tab_c = jnp.concatenate([cr, cr, cc, cc], axis=1)
    tab_sa = jnp.concatenate([-sr, z, -sc, z], axis=1)
    tab_sb = jnp.concatenate([z, sr, z, sc], axis=1)
    return tab_c, tab_sa, tab_sb


def kernel(x_prompt, x_sample, c, cache_k, cache_v, state_gla_fwd, state_gla_bwd, c_ctx, w_ada, b_ada, norm1_g, norm2_g, w_in, q_norm_g, k_norm_g, attn_sink, w_a2_fwd, b_a_fwd, w_a2_bwd, b_a_bwd, gla_norm_g, w_out, w_ffn_in, w_ffn_out):
    assert w_ada.shape[0] == 1, "single trunk layer"
    cc = jnp.zeros((MOD_ROWS, D_MODEL), F32).at[0].set(c_ctx).at[1:1 + DEC_BATCH].set(c)
    mod_all = _ada(cc, w_ada[0], b_ada[0][None, :])
    mod = tuple(mod_all[:, i * D_MODEL:(i + 1) * D_MODEL].reshape(MOD_ROWS, 1, D_MODEL)
                for i in range(N_MOD))

    r = GLA_GATE_RANK
    wa = jnp.zeros((D_MODEL, LANES), BF16).at[:, :2 * r].set(w_in[0][:, D_WIDE:].astype(BF16))
    waf = jnp.zeros((LANES, GLA_DK), BF16).at[:r].set(w_a2_fwd[0].astype(BF16))
    wab = jnp.zeros((LANES, GLA_DK), BF16).at[r:2 * r].set(w_a2_bwd[0].astype(BF16))
    g1, g2 = norm1_g[0][None, :], norm2_g[0][None, :]
    w_in_t = w_in[0].T
    qg, kg, sink = q_norm_g[0][None, :], k_norm_g[0][None, :], attn_sink[0]
    gla_w = (waf, wab, b_a_fwd[0][None, :], b_a_bwd[0][None, :], gla_norm_g[0][None, :])
    shift1, scale1, gate1, shift2, scale2, gate2 = mod
    kvw = N_KV_HEADS * HEAD_DIM
    n_wide, n_gate = OFF_GR, D_WIDE - OFF_GR
    m_ctx, m_lat = BATCH * SEQ, DEC_BATCH * DEC_SEQ
    ctx_mod, lat_mod = (0, m_ctx), (1, DEC_SEQ)

    xp = x_prompt.reshape(m_ctx, D_MODEL)
    xs = x_sample.reshape(m_lat, D_MODEL)

    h_c, a_c = _norm_mod(xp, g1, shift1, scale1, wa, *ctx_mod)
    proj_c, w_fo = _proj(h_c, w_in_t, 0, n_wide, F32, _cast_side(w_ffn_out[0], 128))
    gates_c, w_o = _proj(h_c, w_in_t, n_wide, n_gate, BF16, _cast_side(w_out[0], 64))
    oatt_c, new_k, new_v = _ctx_attn(proj_c, sink, qg, kg)
    og_c, s_f, s_b = _gla(proj_c, a_c, *gla_w, SEQ)

    h_l, a_l = _norm_mod(xs, g1, shift1, scale1, wa, *lat_mod)
    merged_c = _merge(oatt_c, og_c, gates_c)
    proj_l = _proj(h_l, w_in_t, 0, n_wide, F32)
    gates_l = _proj(h_l, w_in_t, n_wide, n_gate, BF16)
    oatt_l = _lat_attn(
        proj_l,
        cache_k[:, 0].reshape(DEC_BATCH, PAST_LEN, kvw),
        cache_v[:, 0].reshape(DEC_BATCH, PAST_LEN, kvw),
        sink, qg, kg, *_rope_tables(DEC_SEQ))
    og_l = _gla(proj_l, a_l, *gla_w, DEC_SEQ,
                state_gla_fwd[:, 0].reshape(DEC_BATCH * GLA_HEADS, GLA_HK, GLA_HV),
                state_gla_bwd[:, 0].reshape(DEC_BATCH * GLA_HEADS, GLA_HK, GLA_HV))

    x1_c, h2_c = _out_proj(merged_c, w_o, xp, gate1, g2, shift2, scale2, *ctx_mod)
    act_c = _ffn_in(h2_c, w_ffn_in[0])
    yp, merged_l = _ffn_out(act_c, w_fo, x1_c, gate2, *ctx_mod,
                            side=_merge_side(oatt_l, og_l, gates_l, 64))

    x1_l, h2_l = _out_proj(merged_l, w_o, xs, gate1, g2, shift2, scale2, *lat_mod)
    act_l = _ffn_in(h2_l, w_ffn_in[0])
    ys = _ffn_out(act_l, w_fo, x1_l, gate2, *lat_mod)
    return (
        yp.reshape(BATCH, SEQ, D_MODEL),
        ys.reshape(DEC_BATCH, DEC_SEQ, D_MODEL),
        new_k.reshape(BATCH, 1, SEQ, N_KV_HEADS, HEAD_DIM),
        new_v.reshape(BATCH, 1, SEQ, N_KV_HEADS, HEAD_DIM),
        s_f.reshape(BATCH, 1, GLA_HEADS, GLA_HK, GLA_HV),
        s_b.reshape(BATCH, 1, GLA_HEADS, GLA_HK, GLA_HV),
    )
```

```python
import functools
from typing import Callable, NamedTuple

import jax
import jax.numpy as jnp
from jax import lax
from jax.experimental import pallas as pl
from jax.experimental.pallas import tpu as pltpu

F32 = jnp.float32
BF16 = jnp.bfloat16

D_MODEL = 4096
BATCH = 32
SEQ = 256
DEC_BATCH = 8
DEC_SEQ = 1024
PAST_LEN = 256
GRID_W = 64
HEAD_DIM = 128
N_Q_HEADS = 32
N_KV_HEADS = 8
GQA_GROUP = N_Q_HEADS // N_KV_HEADS
WINDOW = 128
BLOCK = 128
ROPE_THETA = 10000.0
GLA_HEADS = 4
GLA_DK = D_MODEL // 2
GLA_DV = D_MODEL
GLA_HK = GLA_DK // GLA_HEADS
GLA_HV = GLA_DV // GLA_HEADS
GLA_GATE_RANK = 16
GLA_TAU = 16.0
GLA_CHUNK = 64
GLA_SUPER = 256
D_FF = 11008
N_MOD = 6
EPS = 1e-6

OFF_Q = 0
OFF_K = 4096
OFF_V = 5120
OFF_GQ = 6144
OFF_GK = 8192
OFF_GV = 10240
OFF_GR = 14336
OFF_GATT = 18432
OFF_GGLA = 22528
D_WIDE = 26624
LANES = 128
MXU_ACC_ROWS = 512
MOD_ROWS = 16

MIB = 1024 * 1024
VMEM_BUDGET = 60 * MIB
VMEM_SLACK = 2 * MIB
NT_DIMS = (((1,), (1,)), ((), ()))
TN_DIMS = (((0,), (0,)), ((), ()))


def _cp(sem, vmem_mib):
    return pltpu.CompilerParams(dimension_semantics=sem, vmem_limit_bytes=vmem_mib * MIB)


def _rms(x, g):
    ms = jnp.mean(x * x, axis=-1, keepdims=True)
    return x * lax.rsqrt(ms + EPS) * g


def _sigmoid(x):
    return 0.5 * jnp.tanh(0.5 * x) + 0.5


def _dot(a, b):
    return jnp.dot(a, b, preferred_element_type=F32)


ADA_TN = 512


def _ada_kernel(c_ref, w_ref, b_ref, o_ref):
    c = c_ref[...]
    s = (c * _sigmoid(c)).astype(BF16)
    o_ref[...] = _dot(s, w_ref[...].astype(BF16)) + b_ref[...]


def _ada(cc, w_ada, b_ada):
    n = w_ada.shape[1]
    return pl.pallas_call(
        _ada_kernel,
        grid=(n // ADA_TN,),
        in_specs=[
            pl.BlockSpec((MOD_ROWS, D_MODEL), lambda j: (0, 0)),
            pl.BlockSpec((D_MODEL, ADA_TN), lambda j: (0, j)),
            pl.BlockSpec((1, ADA_TN), lambda j: (0, j)),
        ],
        out_specs=pl.BlockSpec((MOD_ROWS, ADA_TN), lambda j: (0, j)),
        out_shape=jax.ShapeDtypeStruct((MOD_ROWS, n), F32),
        compiler_params=_cp(("arbitrary",), 40),
        name="ada_ln",
    )(cc, w_ada, b_ada)


NORM_TM = 512
PROJ_TM = 2048
PROJ_TN = 512


def _mod_spec(mod_off, bpb):
    return pl.BlockSpec((None, 1, D_MODEL), lambda i, *_: (mod_off + i // bpb, 0, 0))


def _norm_kernel(x_ref, g_ref, sh_ref, sc_ref, wa_ref, h_ref, a_ref):
    h = _rms(x_ref[...], g_ref[...]) * (1.0 + sc_ref[...]) + sh_ref[...]
    hb = h.astype(BF16)
    h_ref[...] = hb
    a_ref[...] = _dot(hb, wa_ref[...])


def _norm_mod(x, g1, shift, scale, wa, mod_off, rows_per_mod):
    m = x.shape[0]
    bpb = rows_per_mod // NORM_TM
    return pl.pallas_call(
        _norm_kernel,
        grid=(m // NORM_TM,),
        in_specs=[
            pl.BlockSpec((NORM_TM, D_MODEL), lambda i: (i, 0)),
            pl.BlockSpec((1, D_MODEL), lambda i: (0, 0)),
            _mod_spec(mod_off, bpb),
            _mod_spec(mod_off, bpb),
            pl.BlockSpec((D_MODEL, LANES), lambda i: (0, 0)),
        ],
        out_specs=[
            pl.BlockSpec((NORM_TM, D_MODEL), lambda i: (i, 0)),
            pl.BlockSpec((NORM_TM, LANES), lambda i: (i, 0)),
        ],
        out_shape=[
            jax.ShapeDtypeStruct((m, D_MODEL), BF16),
            jax.ShapeDtypeStruct((m, LANES), F32),
        ],
        compiler_params=_cp(("arbitrary",), 40),
        name="norm_modulate",
    )(x, g1, shift, scale, wa)


class _Side(NamedTuple):
    fn: Callable
    ins: tuple
    in_cols: tuple
    outs: tuple
    rows: int
    nblk: int

    def vmem_bytes(self):
        width = self.outs[0].shape[1]
        per_row = sum(a.dtype.itemsize for a in self.ins) + sum(o.dtype.itemsize for o in self.outs)
        return 2 * self.rows * width * per_row

    def specs(self, nj):
        def spec(shape_cols, cb):
            return pl.BlockSpec((self.rows, shape_cols),
                                lambda i, j: (jnp.minimum(i * nj + j, self.nblk - 1), cb))
        in_specs = [spec(self.outs[0].shape[1], cb) for cb in self.in_cols]
        out_specs = [spec(o.shape[1], 0) for o in self.outs]
        return in_specs, out_specs


def _host_kernel(body, n_in, n_out, side):
    def kern(*refs):
        n_sin = len(side.ins) if side else 0
        if side:
            side.fn(*refs[n_in:n_in + n_sin], *refs[n_in + n_sin + n_out:])
        body(*refs[:n_in], *refs[n_in + n_sin:n_in + n_sin + n_out])
    return kern


def _host_call(body, grid, in_specs, out_spec, out_shape, args, side, vmem_mib, name):
    if side is None:
        s_in, s_out = [], []
    else:
        assert grid[0] * grid[1] >= side.nblk
        s_in, s_out = side.specs(grid[1])
    res = pl.pallas_call(
        _host_kernel(body, len(in_specs), 1, side),
        grid=grid,
        in_specs=in_specs + s_in,
        out_specs=[out_spec] + s_out,
        out_shape=[out_shape] + (list(side.outs) if side else []),
        compiler_params=_cp(("arbitrary", "arbitrary"), vmem_mib),
        name=name,
    )(*args, *(side.ins if side else ()))
    return res[0] if side is None else res


def _cast_kernel(src_ref, dst_ref):
    dst_ref[...] = src_ref[...].astype(BF16)


def _cast_side(w, rows):
    n, d = w.shape
    return _Side(_cast_kernel, (w,), (0,), (jax.ShapeDtypeStruct((n, d), BF16),), rows, n // rows)


def _proj_kernel(h_ref, wt_ref, o_ref):
    wt = wt_ref[...].astype(BF16)
    for r0 in range(0, PROJ_TM, MXU_ACC_ROWS):
        rs = slice(r0, r0 + MXU_ACC_ROWS)
        o_ref[rs, :] = lax.dot_general(h_ref[rs, :], wt, NT_DIMS,
                                       preferred_element_type=F32).astype(o_ref.dtype)


def _proj(h, wt, col0, ncols, out_dtype, side=None):
    m = h.shape[0]
    j0 = col0 // PROJ_TN
    fixed = 2 * PROJ_TN * D_MODEL * 4 + 2 * PROJ_TM * PROJ_TN * jnp.dtype(out_dtype).itemsize
    fixed += side.vmem_bytes() if side else 0
    h_tile = PROJ_TM * D_MODEL * h.dtype.itemsize
    h_bufs = 2 if fixed + 2 * h_tile + VMEM_SLACK <= VMEM_BUDGET else 1
    in_specs = [
        pl.BlockSpec((PROJ_TM, D_MODEL), lambda i, j: (i, 0), pipeline_mode=pl.Buffered(h_bufs)),
        pl.BlockSpec((PROJ_TN, D_MODEL), lambda i, j: (j0 + j, 0)),
    ]
    return _host_call(
        _proj_kernel, (m // PROJ_TM, ncols // PROJ_TN), in_specs,
        pl.BlockSpec((PROJ_TM, PROJ_TN), lambda i, j: (i, j)),
        jax.ShapeDtypeStruct((m, ncols), out_dtype), (h, wt), side, VMEM_BUDGET // MIB, "in_proj")


def _sink_col(sink_ref, h, rows_per_head):
    rid = lax.broadcasted_iota(jnp.int32, (GQA_GROUP * rows_per_head, 1), 0) // rows_per_head
    col = jnp.full(rid.shape, sink_ref[h * GQA_GROUP], F32)
    for g in range(1, GQA_GROUP):
        col = jnp.where(rid == g, sink_ref[h * GQA_GROUP + g], col)
    return col


def _ctx_attn_kernel(sink_ref, q_ref, k_ref, v_ref, qg_ref, kg_ref, o_ref, nk_ref, nv_ref):
    scale = HEAD_DIM ** -0.5
    nv_ref[...] = v_ref[...]
    for h in range(N_KV_HEADS):
        hs = slice(h * HEAD_DIM, (h + 1) * HEAD_DIM)
        kn = _rms(k_ref[:, hs], kg_ref[...])
        nk_ref[:, hs] = kn
        kb = kn.astype(BF16)
        vb = v_ref[:, hs].astype(BF16)
        qs = []
        for g in range(GQA_GROUP):
            c0 = (h * GQA_GROUP + g) * HEAD_DIM
            qs.append(_rms(q_ref[:, c0:c0 + HEAD_DIM], qg_ref[...]).astype(BF16))
        q4 = jnp.concatenate(qs, axis=0)
        s = lax.dot_general(q4, kb, NT_DIMS, preferred_element_type=F32) * scale
        sk = _sink_col(sink_ref, h, SEQ)
        m = jnp.maximum(jnp.max(s, axis=1, keepdims=True), sk)
        p = jnp.exp(s - m)
        den = jnp.sum(p, axis=1, keepdims=True) + jnp.exp(sk - m)
        o = _dot(p.astype(BF16), vb) / den
        for g in range(GQA_GROUP):
            c0 = (h * GQA_GROUP + g) * HEAD_DIM
            o_ref[:, c0:c0 + HEAD_DIM] = o[g * SEQ:(g + 1) * SEQ, :].astype(BF16)


def _ctx_attn(proj, sink, qg, kg):
    m = proj.shape[0]
    kvw = N_KV_HEADS * HEAD_DIM
    return pl.pallas_call(
        _ctx_attn_kernel,
        grid=(m // SEQ,),
        in_specs=[
            pl.BlockSpec(memory_space=pltpu.SMEM),
            pl.BlockSpec((SEQ, D_MODEL), lambda b: (b, OFF_Q // D_MODEL)),
            pl.BlockSpec((SEQ, kvw), lambda b: (b, OFF_K // kvw)),
            pl.BlockSpec((SEQ, kvw), lambda b: (b, OFF_V // kvw)),
            pl.BlockSpec((1, HEAD_DIM), lambda b: (0, 0)),
            pl.BlockSpec((1, HEAD_DIM), lambda b: (0, 0)),
        ],
        out_specs=[
            pl.BlockSpec((SEQ, D_MODEL), lambda b: (b, 0)),
            pl.BlockSpec((SEQ, kvw), lambda b: (b, 0)),
            pl.BlockSpec((SEQ, kvw), lambda b: (b, 0)),
        ],
        out_shape=[
            jax.ShapeDtypeStruct((m, D_MODEL), BF16),
            jax.ShapeDtypeStruct((m, kvw), F32),
            jax.ShapeDtypeStruct((m, kvw), F32),
        ],
        compiler_params=_cp(("arbitrary",), 40),
        name="ctx_attention",
    )(sink, proj, proj, proj, qg, kg)


def _rope(x, c_ref, sa_ref, sb_ref):
    up = pltpu.roll(x, HEAD_DIM - 32, axis=1)
    dn = pltpu.roll(x, 32, axis=1)
    return x * c_ref[...] + up * sa_ref[...] + dn * sb_ref[...]


def _lat_attn_kernel(sink_ref, q_ref, k_ref, v_ref, ck_ref, cv_ref, qg_ref, kg_ref,
                     c_ref, sa_ref, sb_ref, o_ref, q_scr, k_scr, v_scr):
    t = DEC_SEQ
    scale = HEAD_DIM ** -0.5
    h = pl.program_id(1)
    nb = t // BLOCK
    kr = _rope(_rms(k_ref[...], kg_ref[...]), c_ref, sa_ref, sb_ref)
    k_scr[...] = kr.astype(BF16)
    v_scr[...] = v_ref[...].astype(BF16)
    for g in range(GQA_GROUP):
        qn = _rms(q_ref[:, g * HEAD_DIM:(g + 1) * HEAD_DIM], qg_ref[...])
        q_scr[g] = _rope(qn, c_ref, sa_ref, sb_ref).astype(BF16)
    ckb = ck_ref[...].astype(BF16)
    cvb = cv_ref[...].astype(BF16)
    sk = _sink_col(sink_ref, h, BLOCK)
    rows = GQA_GROUP * BLOCK
    qi = lax.broadcasted_iota(jnp.int32, (rows, BLOCK), 0) % BLOCK
    kj = lax.broadcasted_iota(jnp.int32, (rows, BLOCK), 1)
    prev_ok = kj >= qi
    next_ok = kj <= qi
    neg = jnp.finfo(F32).min

    def attend(n, has_prev, has_next):
        if isinstance(n, int):
            r0, k0 = n * BLOCK, (n - has_prev) * BLOCK
        else:
            r0 = pl.multiple_of(n * BLOCK, BLOCK)
            k0 = pl.multiple_of((n - has_prev) * BLOCK, BLOCK)
        nk = (1 + has_prev + has_next) * BLOCK
        q4 = jnp.concatenate([q_scr[g, pl.ds(r0, BLOCK), :] for g in range(GQA_GROUP)], axis=0)
        s_win = lax.dot_general(q4, k_scr[pl.ds(k0, nk), :], NT_DIMS,
                                preferred_element_type=F32) * scale
        tiles = [s_win[:, i * BLOCK:(i + 1) * BLOCK] for i in range(nk // BLOCK)]
        if has_prev:
            tiles[0] = jnp.where(prev_ok, tiles[0], neg)
        if has_next:
            tiles[-1] = jnp.where(next_ok, tiles[-1], neg)
        s_win = jnp.concatenate(tiles, axis=1)
        s_ctx = lax.dot_general(q4, ckb, NT_DIMS, preferred_element_type=F32) * scale
        m = jnp.maximum(jnp.max(s_win, axis=1, keepdims=True),
                        jnp.max(s_ctx, axis=1, keepdims=True))
        m = jnp.maximum(m, sk)
        p_win = jnp.exp(s_win - m)
        p_ctx = jnp.exp(s_ctx - m)
        den = (jnp.sum(p_win, axis=1, keepdims=True) + jnp.sum(p_ctx, axis=1, keepdims=True)
               + jnp.exp(sk - m))
        o = (_dot(p_win.astype(BF16), v_scr[pl.ds(k0, nk), :]) + _dot(p_ctx.astype(BF16), cvb)) / den
        for g in range(GQA_GROUP):
            o_ref[pl.ds(r0, BLOCK), g * HEAD_DIM:(g + 1) * HEAD_DIM] = (
                o[g * BLOCK:(g + 1) * BLOCK, :].astype(BF16))

    def interior(n, carry):
        attend(n, True, True)
        return carry

    attend(0, False, True)
    lax.fori_loop(1, nb - 1, interior, 0, unroll=3)
    attend(nb - 1, True, False)


def _lat_attn(proj, ck, cv, sink, qg, kg, rope_c, rope_sa, rope_sb):
    m = proj.shape[0]
    t = DEC_SEQ
    gw = GQA_GROUP * HEAD_DIM
    tab = pl.BlockSpec((t, HEAD_DIM), lambda b, h: (0, 0))
    vec = pl.BlockSpec((1, HEAD_DIM), lambda b, h: (0, 0))
    cache = pl.BlockSpec((None, PAST_LEN, HEAD_DIM), lambda b, h: (b, 0, h))
    return pl.pallas_call(
        _lat_attn_kernel,
        grid=(m // t, N_KV_HEADS),
        in_specs=[
            pl.BlockSpec(memory_space=pltpu.SMEM),
            pl.BlockSpec((t, gw), lambda b, h: (b, OFF_Q // gw + h)),
            pl.BlockSpec((t, HEAD_DIM), lambda b, h: (b, OFF_K // HEAD_DIM + h)),
            pl.BlockSpec((t, HEAD_DIM), lambda b, h: (b, OFF_V // HEAD_DIM + h)),
            cache, cache, vec, vec, tab, tab, tab,
        ],
        out_specs=pl.BlockSpec((t, gw), lambda b, h: (b, h)),
        out_shape=jax.ShapeDtypeStruct((m, D_MODEL), BF16),
        scratch_shapes=[
            pltpu.VMEM((GQA_GROUP, t, HEAD_DIM), BF16),
            pltpu.VMEM((t, HEAD_DIM), BF16),
            pltpu.VMEM((t, HEAD_DIM), BF16),
        ],
        compiler_params=_cp(("arbitrary", "arbitrary"), 40),
        name="latent_attention",
    )(sink, proj, proj, proj, ck, cv, qg, kg, rope_c, rope_sa, rope_sb)


def _log_sigmoid(x):
    return jnp.minimum(x, 0.0) - jnp.log(1.0 + jnp.exp(-jnp.abs(x)))


def _split2(x):
    hi = x.astype(BF16)
    lo = (x - hi.astype(F32)).astype(BF16)
    return hi, lo


def _gla_kernel(*refs, t, has_state):
    if has_state:
        (q_ref, k_ref, v_ref, a_ref, waf_ref, wab_ref, baf_ref, bab_ref, gn_ref,
         s0f_ref, s0b_ref, og_ref, *scr) = refs
    else:
        (q_ref, k_ref, v_ref, a_ref, waf_ref, wab_ref, baf_ref, bab_ref, gn_ref,
         og_ref, sf_out, sb_out, *scr) = refs
    stf, stb, vb, ob = scr[:4]
    qsf, kuf, decf, *tmpf = scr[4:13]
    qsb, kub, decb, *tmpb = scr[13:]
    c = GLA_CHUNK
    sc = GLA_SUPER
    nsub = sc // c
    nsc = t // sc
    vb[...] = v_ref[...].astype(BF16)

    row = lax.broadcasted_iota(jnp.int32, (sc, sc), 0)
    col = lax.broadcasted_iota(jnp.int32, (sc, sc), 1)
    rc, cc = row // c, col // c

    def intra(si, fwd):
        r0 = pl.multiple_of(si * sc, sc)
        w_ref, b_ref, qs_scr, ku_scr, dec_scr, tmp, dst = (
            (waf_ref, baf_ref, qsf, kuf, decf, tmpf, og_ref) if fwd
            else (wab_ref, bab_ref, qsb, kub, decb, tmpb, ob))
        qd, kd, ke, q2, q3, amat = tmp
        tri = jnp.where((col <= row) if fwd else (col >= row), 1.0, 0.0).astype(BF16)
        dist = (rc - cc) if fwd else (cc - rc)
        x = _dot(a_ref[pl.ds(r0, sc), :].astype(BF16), w_ref[...]) + b_ref[...]
        hi, lo = _split2(_log_sigmoid(x) / GLA_TAU)
        cum = _dot(tri, hi) + _dot(tri, lo)
        zero = jnp.zeros((1, GLA_HK), F32)

        def at_start(j):
            if fwd:
                return cum[j * c - 1:j * c, :] if j > 0 else zero
            return cum[(j + 1) * c:(j + 1) * c + 1, :] if j < nsub - 1 else zero

        def at_end(j):
            return cum[(j + 1) * c - 1:(j + 1) * c, :] if fwd else cum[j * c:j * c + 1, :]

        total = at_end(nsub - 1) if fwd else at_end(0)
        dec_scr[pl.ds(pl.multiple_of(si * 8, 8), 8), :] = jnp.broadcast_to(jnp.exp(total), (8, GLA_HK))
        for j in range(nsub):
            rs = slice(j * c, (j + 1) * c)
            rows = pl.ds(pl.multiple_of(r0 + j * c, c), c)
            cj = cum[rs, :]
            cs, ce = at_start(j), at_end(j)
            q = q_ref[rows, :] * (GLA_HK ** -0.5)
            k = k_ref[rows, :]
            qdj = q * jnp.exp(cj - cs)
            kej = k * jnp.exp(ce - cj)
            qd[rs, :] = qdj.astype(BF16)
            kd[rs, :] = (k * jnp.exp(cs - cj)).astype(BF16)
            ke[rs, :] = kej.astype(BF16)
            qs_scr[rows, :] = (qdj * jnp.exp(cs)).astype(BF16)
            ku_scr[rows, :] = (kej * jnp.exp(total - ce)).astype(BF16)
            p2 = j - 2 if fwd else j + 2
            if 0 <= p2 < nsub:
                l2 = (j - 2) if fwd else j
                q2[l2 * c:(l2 + 1) * c, :] = (qdj * jnp.exp(cs - at_end(p2))).astype(BF16)
            p3 = j - 3 if fwd else j + 3
            if 0 <= p3 < nsub:
                q3[...] = (qdj * jnp.exp(cs - at_end(p3))).astype(BF16)
        nt = lambda a, b: lax.dot_general(a, b, NT_DIMS, preferred_element_type=F32)
        tril = (col <= row) if fwd else (col >= row)
        amat[...] = (jnp.where((dist == 0) & tril, nt(qd[...], kd[...]), 0.0)
                     + jnp.where(dist == 1, nt(qd[...], ke[...]), 0.0))
        r2 = slice(2 * c, sc) if fwd else slice(0, 2 * c)
        amat[r2, :] += jnp.where(dist[r2, :] == 2, nt(q2[...], ke[...]), 0.0)
        r3 = slice(3 * c, sc) if fwd else slice(0, c)
        amat[r3, :] += jnp.where(dist[r3, :] == 3, nt(q3[...], ke[...]), 0.0)
        dst[pl.ds(r0, sc), :] = _dot(amat[...].astype(BF16), vb[pl.ds(r0, sc), :])

    def intra_body(i, carry):
        intra(i, True)
        intra(nsc - 1 - i, False)
        return carry

    lax.fori_loop(0, nsc, intra_body, 0, unroll=True)

    if has_state:
        for i in range(nsc):
            for fwd in (True, False):
                si = i if fwd else nsc - 1 - i
                rows = slice(si * sc, (si + 1) * sc)
                st_ref, s0_ref, qs_scr, ku_scr, dec_scr, dst = (
                    (stf, s0f_ref, qsf, kuf, decf, og_ref) if fwd
                    else (stb, s0b_ref, qsb, kub, decb, ob))
                st = s0_ref[...] if i == 0 else st_ref[...]
                dst[rows, :] += _dot(qs_scr[rows, :], st.astype(BF16))
                if i < nsc - 1:
                    dec = jnp.broadcast_to(dec_scr[si * 8:si * 8 + 1, :], (LANES, GLA_HK)).T
                    dec = jnp.concatenate([dec] * (GLA_HV // LANES), axis=1)
                    st_ref[...] = st * dec + lax.dot_general(
                        ku_scr[rows, :], vb[rows, :], TN_DIMS, preferred_element_type=F32)
    else:
        sf_out[...] = lax.dot_general(kuf[...], vb[...], TN_DIMS, preferred_element_type=F32)
        sb_out[...] = lax.dot_general(kub[...], vb[...], TN_DIMS, preferred_element_type=F32)
    og_ref[...] = _rms(og_ref[...] + ob[...], gn_ref[...])


def _gla(proj, aproj, waf, wab, baf, bab, gn, t, s0f=None, s0b=None):
    m = proj.shape[0]
    nb = m // t
    has_state = s0f is not None
    assert t % GLA_SUPER == 0 and (has_state or t == GLA_SUPER)
    sc, c = GLA_SUPER, GLA_CHUNK
    st_shape_vmem = (GLA_HK, GLA_HV) if has_state else (8, LANES)
    per_direction = [
        pltpu.VMEM((t, GLA_HK), BF16),
        pltpu.VMEM((t, GLA_HK), BF16),
        pltpu.VMEM((8 * (t // sc), GLA_HK), F32),
        pltpu.VMEM((sc, GLA_HK), BF16),
        pltpu.VMEM((sc, GLA_HK), BF16),
        pltpu.VMEM((sc, GLA_HK), BF16),
        pltpu.VMEM((sc - 2 * c, GLA_HK), BF16),
        pltpu.VMEM((sc - 3 * c, GLA_HK), BF16),
        pltpu.VMEM((sc, sc), F32),
    ]
    st_spec = pl.BlockSpec((None, GLA_HK, GLA_HV), lambda b, h: (b * GLA_HEADS + h, 0, 0))
    in_specs = [
        pl.BlockSpec((t, GLA_HK), lambda b, h: (b, OFF_GQ // GLA_HK + h)),
        pl.BlockSpec((t, GLA_HK), lambda b, h: (b, OFF_GK // GLA_HK + h)),
        pl.BlockSpec((t, GLA_HV), lambda b, h: (b, OFF_GV // GLA_HV + h)),
        pl.BlockSpec((t, LANES), lambda b, h: (b, 0)),
        pl.BlockSpec((LANES, GLA_HK), lambda b, h: (0, h)),
        pl.BlockSpec((LANES, GLA_HK), lambda b, h: (0, h)),
        pl.BlockSpec((1, GLA_HK), lambda b, h: (0, h)),
        pl.BlockSpec((1, GLA_HK), lambda b, h: (0, h)),
        pl.BlockSpec((1, GLA_HV), lambda b, h: (0, h)),
    ]
    args = [proj, proj, proj, aproj, waf, wab, baf, bab, gn]
    og_spec = pl.BlockSpec((t, GLA_HV), lambda b, h: (b, h))
    og_shape = jax.ShapeDtypeStruct((m, GLA_DV), F32)
    if has_state:
        in_specs += [st_spec, st_spec]
        args += [s0f, s0b]
        out_specs, out_shape = og_spec, og_shape
    else:
        st_shape = jax.ShapeDtypeStruct((nb * GLA_HEADS, GLA_HK, GLA_HV), F32)
        out_specs, out_shape = [og_spec, st_spec, st_spec], [og_shape, st_shape, st_shape]
    return pl.pallas_call(
        functools.partial(_gla_kernel, t=t, has_state=has_state),
        grid=(nb, GLA_HEADS),
        in_specs=in_specs,
        out_specs=out_specs,
        out_shape=out_shape,
        scratch_shapes=[
            pltpu.VMEM(st_shape_vmem, F32),
            pltpu.VMEM(st_shape_vmem, F32),
            pltpu.VMEM((t, GLA_HV), BF16),
            pltpu.VMEM((t, GLA_HV), F32),
        ] + 2 * per_direction,
        compiler_params=_cp(("arbitrary", "arbitrary"), 56),
        name="gla_state" if has_state else "gla_zero",
    )(*args)


MERGE_SLAB = (16, 1024)


def _merge_kernel(oa_ref, og_ref, gr_ref, ga_ref, gg_ref, o_ref):
    rows, cols = o_ref.shape
    for r0 in range(0, rows, MERGE_SLAB[0]):
        for c0 in range(0, cols, MERGE_SLAB[1]):
            sl = (slice(r0, r0 + MERGE_SLAB[0]), slice(c0, c0 + MERGE_SLAB[1]))
            gr = gr_ref[sl].astype(F32)
            o_gla = og_ref[sl] * (gr * _sigmoid(gr))
            merged = (_sigmoid(ga_ref[sl].astype(F32)) * oa_ref[sl].astype(F32)
                      + _sigmoid(gg_ref[sl].astype(F32)) * o_gla)
            o_ref[sl] = merged.astype(BF16)


MERGE_TM = 256


def _merge(o_att, og, gates):
    m = o_att.shape[0]
    blk = lambda cb: pl.BlockSpec((MERGE_TM, D_MODEL), lambda i: (i, cb))
    cols = ((OFF_GR - OFF_GR) // D_MODEL, (OFF_GATT - OFF_GR) // D_MODEL, (OFF_GGLA - OFF_GR) // D_MODEL)
    return pl.pallas_call(
        _merge_kernel,
        grid=(m // MERGE_TM,),
        in_specs=[blk(0), blk(0)] + [blk(cb) for cb in cols],
        out_specs=blk(0),
        out_shape=jax.ShapeDtypeStruct((m, D_MODEL), BF16),
        compiler_params=_cp(("arbitrary",), 40),
        name="branch_merge",
    )(o_att, og, gates, gates, gates)


def _merge_side(o_att, og, gates, rows):
    m = o_att.shape[0]
    cols = ((OFF_GR - OFF_GR) // D_MODEL, (OFF_GATT - OFF_GR) // D_MODEL, (OFF_GGLA - OFF_GR) // D_MODEL)
    return _Side(_merge_kernel, (o_att, og, gates, gates, gates), (0, 0) + cols,
                 (jax.ShapeDtypeStruct((m, D_MODEL), BF16),), rows, m // rows)


OUT_TM = 512
OUT_TN = 1024


def _out_kernel(mg_ref, w_ref, x_ref, gt_ref, g2_ref, sh_ref, sc_ref, x1_ref, h2_ref, x1_scr):
    j = pl.program_id(1)
    nj = D_MODEL // OUT_TN
    slab = MXU_ACC_ROWS * 512 // OUT_TN
    for r0 in range(0, OUT_TM, slab):
        rs = slice(r0, r0 + slab)
        x1 = x_ref[rs, :] + gt_ref[...] * _dot(mg_ref[rs, :], w_ref[...])
        x1_ref[rs, :] = x1
        x1_scr[j, rs, :] = x1

    @pl.when(j == nj - 1)
    def _():
        ssq = jnp.zeros((OUT_TM, 1), F32)
        for jj in range(nj):
            xs = x1_scr[jj]
            ssq = ssq + jnp.sum(xs * xs, axis=-1, keepdims=True)
        inv = lax.rsqrt(ssq / D_MODEL + EPS)
        for jj in range(nj):
            cs = slice(jj * OUT_TN, (jj + 1) * OUT_TN)
            y = x1_scr[jj] * inv * g2_ref[:, cs]
            h2_ref[:, cs] = (y * (1.0 + sc_ref[:, cs]) + sh_ref[:, cs]).astype(BF16)


def _out_proj(merged, w_out, x, gate1, g2, shift2, scale2, mod_off, rows_per_mod):
    m = x.shape[0]
    bpb = rows_per_mod // OUT_TM
    gate_spec = pl.BlockSpec((None, 1, OUT_TN), lambda i, j: (mod_off + i // bpb, 0, j))
    return pl.pallas_call(
        _out_kernel,
        grid=(m // OUT_TM, D_MODEL // OUT_TN),
        in_specs=[
            pl.BlockSpec((OUT_TM, D_MODEL), lambda i, j: (i, 0)),
            pl.BlockSpec((D_MODEL, OUT_TN), lambda i, j: (0, j)),
            pl.BlockSpec((OUT_TM, OUT_TN), lambda i, j: (i, j)),
            gate_spec,
            pl.BlockSpec((1, D_MODEL), lambda i, j: (0, 0)),
            _mod_spec(mod_off, bpb),
            _mod_spec(mod_off, bpb),
        ],
        out_specs=[
            pl.BlockSpec((OUT_TM, OUT_TN), lambda i, j: (i, j)),
            pl.BlockSpec((OUT_TM, D_MODEL), lambda i, j: (i, 0)),
        ],
        out_shape=[
            jax.ShapeDtypeStruct((m, D_MODEL), F32),
            jax.ShapeDtypeStruct((m, D_MODEL), BF16),
        ],
        scratch_shapes=[pltpu.VMEM((D_MODEL // OUT_TN, OUT_TM, OUT_TN), F32)],
        compiler_params=_cp(("arbitrary", "arbitrary"), 56),
        name="out_proj_residual_norm",
    )(merged, w_out, x, gate1, g2, shift2, scale2)


FFI_TM = 2048
FFI_TN = 256


def _ffn_in_kernel(h_ref, wg_ref, wu_ref, o_ref):
    wg = wg_ref[...].astype(BF16)
    wu = wu_ref[...].astype(BF16)
    slab = MXU_ACC_ROWS * 512 // FFI_TN
    for r0 in range(0, FFI_TM, slab):
        rs = slice(r0, r0 + slab)
        h = h_ref[rs, :]
        g = _dot(h, wg)
        u = _dot(h, wu)
        o_ref[rs, :] = (g * _sigmoid(g) * u).astype(BF16)


def _ffn_in(h2, w):
    m = h2.shape[0]
    return pl.pallas_call(
        _ffn_in_kernel,
        grid=(m // FFI_TM, D_FF // FFI_TN),
        in_specs=[
            pl.BlockSpec((FFI_TM, D_MODEL), lambda i, j: (i, 0)),
            pl.BlockSpec((D_MODEL, FFI_TN), lambda i, j: (0, j)),
            pl.BlockSpec((D_MODEL, FFI_TN), lambda i, j: (0, D_FF // FFI_TN + j)),
        ],
        out_specs=pl.BlockSpec((FFI_TM, FFI_TN), lambda i, j: (i, j)),
        out_shape=jax.ShapeDtypeStruct((m, D_FF), BF16),
        compiler_params=_cp(("arbitrary", "arbitrary"), 58),
        name="ffn_in_swiglu",
    )(h2, w, w)


FFO_TM = 512
FFO_TN = 512


def _ffn_out_kernel(a_ref, w_ref, x_ref, gt_ref, o_ref):
    o_ref[...] = x_ref[...] + gt_ref[...] * _dot(a_ref[...], w_ref[...])


def _ffn_out(act, w, x1, gate2, mod_off, rows_per_mod, side=None):
    m = x1.shape[0]
    bpb = rows_per_mod // FFO_TM
    in_specs = [
        pl.BlockSpec((FFO_TM, D_FF), lambda i, j: (i, 0)),
        pl.BlockSpec((D_FF, FFO_TN), lambda i, j: (0, j)),
        pl.BlockSpec((FFO_TM, FFO_TN), lambda i, j: (i, j)),
        pl.BlockSpec((None, 1, FFO_TN), lambda i, j: (mod_off + i // bpb, 0, j)),
    ]
    return _host_call(
        _ffn_out_kernel, (m // FFO_TM, D_MODEL // FFO_TN), in_specs,
        pl.BlockSpec((FFO_TM, FFO_TN), lambda i, j: (i, j)),
        jax.ShapeDtypeStruct((m, D_MODEL), F32), (act, w, x1, gate2), side, 60, "ffn_out_residual")


def _rope_tables(t):
    rows = t // GRID_W
    half = HEAD_DIM // 2
    row = jnp.repeat(jnp.arange(rows, dtype=F32), GRID_W)
    col = jnp.tile(jnp.arange(GRID_W, dtype=F32), rows)
    inv = ROPE_THETA ** (-jnp.arange(0, half, 2, dtype=F32) / half)
    ar = row[:, None] * inv[None, :]
    ac = col[:, None] * inv[None, :]
    cr, sr, cc, sc = jnp.cos(ar), jnp.sin(ar), jnp.cos(ac), jnp.sin(ac)
    z = jnp.zeros_like(sr)
    tab_c = jnp.concatenate([cr, cr, cc, cc], axis=1)
    tab_sa = jnp.concatenate([-sr, z, -sc, z], axis=1)
    tab_sb = jnp.concatenate([z, sr, z, sc], axis=1)
    return tab_c, tab_sa, tab_sb


def kernel(x_prompt, x_sample, c, cache_k, cache_v, state_gla_fwd, state_gla_bwd, c_ctx, w_ada, b_ada, norm1_g, norm2_g, w_in, q_norm_g, k_norm_g, attn_sink, w_a2_fwd, b_a_fwd, w_a2_bwd, b_a_bwd, gla_norm_g, w_out, w_ffn_in, w_ffn_out):
    assert w_ada.shape[0] == 1, "single trunk layer"
    cc = jnp.zeros((MOD_ROWS, D_MODEL), F32).at[0].set(c_ctx).at[1:1 + DEC_BATCH].set(c)
    mod_all = _ada(cc, w_ada[0], b_ada[0][None, :])
    mod = tuple(mod_all[:, i * D_MODEL:(i + 1) * D_MODEL].reshape(MOD_ROWS, 1, D_MODEL)
                for i in range(N_MOD))

    r = GLA_GATE_RANK
    wa = jnp.zeros((D_MODEL, LANES), BF16).at[:, :2 * r].set(w_in[0][:, D_WIDE:].astype(BF16))
    waf = jnp.zeros((LANES, GLA_DK), BF16).at[:r].set(w_a2_fwd[0].astype(BF16))
    wab = jnp.zeros((LANES, GLA_DK), BF16).at[r:2 * r].set(w_a2_bwd[0].astype(BF16))
    g1, g2 = norm1_g[0][None, :], norm2_g[0][None, :]
    w_in_t = w_in[0].T
    qg, kg, sink = q_norm_g[0][None, :], k_norm_g[0][None, :], attn_sink[0]
    gla_w = (waf, wab, b_a_fwd[0][None, :], b_a_bwd[0][None, :], gla_norm_g[0][None, :])
    shift1, scale1, gate1, shift2, scale2, gate2 = mod
    kvw = N_KV_HEADS * HEAD_DIM
    n_wide, n_gate = OFF_GR, D_WIDE - OFF_GR
    m_ctx, m_lat = BATCH * SEQ, DEC_BATCH * DEC_SEQ
    ctx_mod, lat_mod = (0, m_ctx), (1, DEC_SEQ)

    xp = x_prompt.reshape(m_ctx, D_MODEL)
    xs = x_sample.reshape(m_lat, D_MODEL)

    h_c, a_c = _norm_mod(xp, g1, shift1, scale1, wa, *ctx_mod)
    h_l, a_l = _norm_mod(xs, g1, shift1, scale1, wa, *lat_mod)
    proj_c, w_fo = _proj(h_c, w_in_t, 0, n_wide, F32, _cast_side(w_ffn_out[0], 128))
    gates_c, w_o = _proj(h_c, w_in_t, n_wide, n_gate, BF16, _cast_side(w_out[0], 64))
    oatt_c, new_k, new_v = _ctx_attn(proj_c, sink, qg, kg)
    og_c, s_f, s_b = _gla(proj_c, a_c, *gla_w, SEQ)

    merged_c = _merge(oatt_c, og_c, gates_c)
    proj_l = _proj(h_l, w_in_t, 0, n_wide, F32)
    gates_l = _proj(h_l, w_in_t, n_wide, n_gate, BF16)
    oatt_l = _lat_attn(
        proj_l,
        cache_k[:, 0].reshape(DEC_BATCH, PAST_LEN, kvw),
        cache_v[:, 0].reshape(DEC_BATCH, PAST_LEN, kvw),
        sink, qg, kg, *_rope_tables(DEC_SEQ))
    og_l = _gla(proj_l, a_l, *gla_w, DEC_SEQ,
                state_gla_fwd[:, 0].reshape(DEC_BATCH * GLA_HEADS, GLA_HK, GLA_HV),
                state_gla_bwd[:, 0].reshape(DEC_BATCH * GLA_HEADS, GLA_HK, GLA_HV))

    x1_c, h2_c = _out_proj(merged_c, w_o, xp, gate1, g2, shift2, scale2, *ctx_mod)
    act_c = _ffn_in(h2_c, w_ffn_in[0])
    yp, merged_l = _ffn_out(act_c, w_fo, x1_c, gate2, *ctx_mod,
                            side=_merge_side(oatt_l, og_l, gates_l, 64))

    x1_l, h2_l = _out_proj(merged_l, w_o, xs, gate1, g2, shift2, scale2, *lat_mod)
    act_l = _ffn_in(h2_l, w_ffn_in[0])
    ys = _ffn_out(act_l, w_fo, x1_l, gate2, *lat_mod)
    return (
        yp.reshape(BATCH, SEQ, D_MODEL),
        ys.reshape(DEC_BATCH, DEC_SEQ, D_MODEL),
        new_k.reshape(BATCH, 1, SEQ, N_KV_HEADS, HEAD_DIM),
        new_v.reshape(BATCH, 1, SEQ, N_KV_HEADS, HEAD_DIM),
        s_f.reshape(BATCH, 1, GLA_HEADS, GLA_HK, GLA_HV),
        s_b.reshape(BATCH, 1, GLA_HEADS, GLA_HK, GLA_HV),
    )
```

```python
import functools
from typing import Callable, NamedTuple

import jax
import jax.numpy as jnp
from jax import lax
from jax.experimental import pallas as pl
from jax.experimental.pallas import tpu as pltpu

F32 = jnp.float32
BF16 = jnp.bfloat16

D_MODEL = 4096
BATCH = 32
SEQ = 256
DEC_BATCH = 8
DEC_SEQ = 1024
PAST_LEN = 256
GRID_W = 64
HEAD_DIM = 128
N_Q_HEADS = 32
N_KV_HEADS = 8
GQA_GROUP = N_Q_HEADS // N_KV_HEADS
WINDOW = 128
BLOCK = 128
ROPE_THETA = 10000.0
GLA_HEADS = 4
GLA_DK = D_MODEL // 2
GLA_DV = D_MODEL
GLA_HK = GLA_DK // GLA_HEADS
GLA_HV = GLA_DV // GLA_HEADS
GLA_GATE_RANK = 16
GLA_TAU = 16.0
GLA_CHUNK = 64
GLA_SUPER = 256
D_FF = 11008
N_MOD = 6
EPS = 1e-6

OFF_Q = 0
OFF_K = 4096
OFF_V = 5120
OFF_GQ = 6144
OFF_GK = 8192
OFF_GV = 10240
OFF_GR = 14336
OFF_GATT = 18432
OFF_GGLA = 22528
D_WIDE = 26624
LANES = 128
MXU_ACC_ROWS = 512
MOD_ROWS = 16

MIB = 1024 * 1024
VMEM_BUDGET = 60 * MIB
VMEM_SLACK = 2 * MIB
NT_DIMS = (((1,), (1,)), ((), ()))
TN_DIMS = (((0,), (0,)), ((), ()))


def _cp(sem, vmem_mib):
    return pltpu.CompilerParams(dimension_semantics=sem, vmem_limit_bytes=vmem_mib * MIB)


def _rms(x, g):
    ms = jnp.mean(x * x, axis=-1, keepdims=True)
    return x * lax.rsqrt(ms + EPS) * g


def _sigmoid(x):
    return 0.5 * jnp.tanh(0.5 * x) + 0.5


def _dot(a, b):
    return jnp.dot(a, b, preferred_element_type=F32)


ADA_TN = 512


def _ada_kernel(c_ref, w_ref, b_ref, o_ref):
    c = c_ref[...]
    s = (c * _sigmoid(c)).astype(BF16)
    o_ref[...] = _dot(s, w_ref[...].astype(BF16)) + b_ref[...]


def _ada(cc, w_ada, b_ada):
    n = w_ada.shape[1]
    return pl.pallas_call(
        _ada_kernel,
        grid=(n // ADA_TN,),
        in_specs=[
            pl.BlockSpec((MOD_ROWS, D_MODEL), lambda j: (0, 0)),
            pl.BlockSpec((D_MODEL, ADA_TN), lambda j: (0, j)),
            pl.BlockSpec((1, ADA_TN), lambda j: (0, j)),
        ],
        out_specs=pl.BlockSpec((MOD_ROWS, ADA_TN), lambda j: (0, j)),
        out_shape=jax.ShapeDtypeStruct((MOD_ROWS, n), F32),
        compiler_params=_cp(("arbitrary",), 40),
        name="ada_ln",
    )(cc, w_ada, b_ada)


NORM_TM = 512
PROJ_TM = 2048
PROJ_TN = 512


def _mod_spec(mod_off, bpb):
    return pl.BlockSpec((None, 1, D_MODEL), lambda i, *_: (mod_off + i // bpb, 0, 0))


def _norm_kernel(x_ref, g_ref, sh_ref, sc_ref, wa_ref, h_ref, a_ref):
    h = _rms(x_ref[...], g_ref[...]) * (1.0 + sc_ref[...]) + sh_ref[...]
    hb = h.astype(BF16)
    h_ref[...] = hb
    a_ref[...] = _dot(hb, wa_ref[...])


def _norm_mod(x, g1, shift, scale, wa, mod_off, rows_per_mod):
    m = x.shape[0]
    bpb = rows_per_mod // NORM_TM
    return pl.pallas_call(
        _norm_kernel,
        grid=(m // NORM_TM,),
        in_specs=[
            pl.BlockSpec((NORM_TM, D_MODEL), lambda i: (i, 0)),
            pl.BlockSpec((1, D_MODEL), lambda i: (0, 0)),
            _mod_spec(mod_off, bpb),
            _mod_spec(mod_off, bpb),
            pl.BlockSpec((D_MODEL, LANES), lambda i: (0, 0)),
        ],
        out_specs=[
            pl.BlockSpec((NORM_TM, D_MODEL), lambda i: (i, 0)),
            pl.BlockSpec((NORM_TM, LANES), lambda i: (i, 0)),
        ],
        out_shape=[
            jax.ShapeDtypeStruct((m, D_MODEL), BF16),
            jax.ShapeDtypeStruct((m, LANES), F32),
        ],
        compiler_params=_cp(("arbitrary",), 40),
        name="norm_modulate",
    )(x, g1, shift, scale, wa)


class _Side(NamedTuple):
    fn: Callable
    ins: tuple
    in_cols: tuple
    outs: tuple
    rows: int
    nblk: int

    @property
    def width(self):
        o = self.outs[0]
        return o.shape[1] if len(o.shape) == 2 else o.shape[0] * o.shape[2]

    def vmem_bytes(self):
        per_row = sum(a.dtype.itemsize for a in self.ins) + sum(o.dtype.itemsize for o in self.outs)
        return 2 * self.rows * self.width * per_row

    def specs(self, nj):
        blk = lambda i, j: jnp.minimum(i * nj + j, self.nblk - 1)
        in_specs = [pl.BlockSpec((self.rows, self.width), lambda i, j, cb=cb: (blk(i, j), cb))
                    for cb in self.in_cols]
        out_specs = [
            pl.BlockSpec((self.rows, o.shape[1]), lambda i, j: (blk(i, j), 0)) if len(o.shape) == 2
            else pl.BlockSpec((o.shape[0], self.rows, o.shape[2]), lambda i, j: (0, blk(i, j), 0))
            for o in self.outs]
        return in_specs, out_specs


def _host_kernel(body, n_in, n_out, side):
    def kern(*refs):
        n_sin = len(side.ins) if side else 0
        if side:
            side.fn(*refs[n_in:n_in + n_sin], *refs[n_in + n_sin + n_out:])
        body(*refs[:n_in], *refs[n_in + n_sin:n_in + n_sin + n_out])
    return kern


def _host_call(body, grid, in_specs, out_spec, out_shape, args, side, vmem_mib, name):
    if side is None:
        s_in, s_out = [], []
    else:
        assert grid[0] * grid[1] >= side.nblk
        s_in, s_out = side.specs(grid[1])
    res = pl.pallas_call(
        _host_kernel(body, len(in_specs), 1, side),
        grid=grid,
        in_specs=in_specs + s_in,
        out_specs=[out_spec] + s_out,
        out_shape=[out_shape] + (list(side.outs) if side else []),
        compiler_params=_cp(("arbitrary", "arbitrary"), vmem_mib),
        name=name,
    )(*args, *(side.ins if side else ()))
    return res[0] if side is None else res


def _cast_kernel(src_ref, dst_ref):
    if len(dst_ref.shape) == 2:
        dst_ref[...] = src_ref[...].astype(BF16)
    else:
        tc = dst_ref.shape[2]
        for t in range(dst_ref.shape[0]):
            dst_ref[t] = src_ref[:, t * tc:(t + 1) * tc].astype(BF16)


def _cast_side(w, rows, tile_cols=None):
    n, d = w.shape
    shape = (n, d) if tile_cols is None else (d // tile_cols, n, tile_cols)
    return _Side(_cast_kernel, (w,), (0,), (jax.ShapeDtypeStruct(shape, BF16),), rows, n // rows)


def _proj_kernel(h_ref, wt_ref, o_ref):
    wt = wt_ref[...].astype(BF16)
    for r0 in range(0, PROJ_TM, MXU_ACC_ROWS):
        rs = slice(r0, r0 + MXU_ACC_ROWS)
        o_ref[rs, :] = lax.dot_general(h_ref[rs, :], wt, NT_DIMS,
                                       preferred_element_type=F32).astype(o_ref.dtype)


def _proj(h, wt, col0, ncols, out_dtype, side=None):
    m = h.shape[0]
    j0 = col0 // PROJ_TN
    fixed = 2 * PROJ_TN * D_MODEL * 4 + 2 * PROJ_TM * PROJ_TN * jnp.dtype(out_dtype).itemsize
    fixed += side.vmem_bytes() if side else 0
    h_tile = PROJ_TM * D_MODEL * h.dtype.itemsize
    h_bufs = 2 if fixed + 2 * h_tile + VMEM_SLACK <= VMEM_BUDGET else 1
    in_specs = [
        pl.BlockSpec((PROJ_TM, D_MODEL), lambda i, j: (i, 0), pipeline_mode=pl.Buffered(h_bufs)),
        pl.BlockSpec((PROJ_TN, D_MODEL), lambda i, j: (j0 + j, 0)),
    ]
    return _host_call(
        _proj_kernel, (m // PROJ_TM, ncols // PROJ_TN), in_specs,
        pl.BlockSpec((PROJ_TM, PROJ_TN), lambda i, j: (i, j)),
        jax.ShapeDtypeStruct((m, ncols), out_dtype), (h, wt), side, VMEM_BUDGET // MIB, "in_proj")


def _sink_col(sink_ref, h, rows_per_head):
    rid = lax.broadcasted_iota(jnp.int32, (GQA_GROUP * rows_per_head, 1), 0) // rows_per_head
    col = jnp.full(rid.shape, sink_ref[h * GQA_GROUP], F32)
    for g in range(1, GQA_GROUP):
        col = jnp.where(rid == g, sink_ref[h * GQA_GROUP + g], col)
    return col


def _ctx_attn_kernel(sink_ref, q_ref, k_ref, v_ref, qg_ref, kg_ref, o_ref, nk_ref, nv_ref):
    scale = HEAD_DIM ** -0.5
    nv_ref[...] = v_ref[...]
    for h in range(N_KV_HEADS):
        hs = slice(h * HEAD_DIM, (h + 1) * HEAD_DIM)
        kn = _rms(k_ref[:, hs], kg_ref[...])
        nk_ref[:, hs] = kn
        kb = kn.astype(BF16)
        vb = v_ref[:, hs].astype(BF16)
        qs = []
        for g in range(GQA_GROUP):
            c0 = (h * GQA_GROUP + g) * HEAD_DIM
            qs.append(_rms(q_ref[:, c0:c0 + HEAD_DIM], qg_ref[...]).astype(BF16))
        q4 = jnp.concatenate(qs, axis=0)
        s = lax.dot_general(q4, kb, NT_DIMS, preferred_element_type=F32) * scale
        sk = _sink_col(sink_ref, h, SEQ)
        m = jnp.maximum(jnp.max(s, axis=1, keepdims=True), sk)
        p = jnp.exp(s - m)
        den = jnp.sum(p, axis=1, keepdims=True) + jnp.exp(sk - m)
        o = _dot(p.astype(BF16), vb) / den
        for g in range(GQA_GROUP):
            c0 = (h * GQA_GROUP + g) * HEAD_DIM
            o_ref[:, c0:c0 + HEAD_DIM] = o[g * SEQ:(g + 1) * SEQ, :].astype(BF16)


def _ctx_attn(proj, sink, qg, kg):
    m = proj.shape[0]
    kvw = N_KV_HEADS * HEAD_DIM
    return pl.pallas_call(
        _ctx_attn_kernel,
        grid=(m // SEQ,),
        in_specs=[
            pl.BlockSpec(memory_space=pltpu.SMEM),
            pl.BlockSpec((SEQ, D_MODEL), lambda b: (b, OFF_Q // D_MODEL)),
            pl.BlockSpec((SEQ, kvw), lambda b: (b, OFF_K // kvw)),
            pl.BlockSpec((SEQ, kvw), lambda b: (b, OFF_V // kvw)),
            pl.BlockSpec((1, HEAD_DIM), lambda b: (0, 0)),
            pl.BlockSpec((1, HEAD_DIM), lambda b: (0, 0)),
        ],
        out_specs=[
            pl.BlockSpec((SEQ, D_MODEL), lambda b: (b, 0)),
            pl.BlockSpec((SEQ, kvw), lambda b: (b, 0)),
            pl.BlockSpec((SEQ, kvw), lambda b: (b, 0)),
        ],
        out_shape=[
            jax.ShapeDtypeStruct((m, D_MODEL), BF16),
            jax.ShapeDtypeStruct((m, kvw), F32),
            jax.ShapeDtypeStruct((m, kvw), F32),
        ],
        compiler_params=_cp(("arbitrary",), 40),
        name="ctx_attention",
    )(sink, proj, proj, proj, qg, kg)


def _rope(x, c_ref, sa_ref, sb_ref):
    up = pltpu.roll(x, HEAD_DIM - 32, axis=1)
    dn = pltpu.roll(x, 32, axis=1)
    return x * c_ref[...] + up * sa_ref[...] + dn * sb_ref[...]


def _lat_attn_kernel(sink_ref, q_ref, k_ref, v_ref, ck_ref, cv_ref, qg_ref, kg_ref,
                     c_ref, sa_ref, sb_ref, o_ref, q_scr, k_scr, v_scr):
    t = DEC_SEQ
    scale = HEAD_DIM ** -0.5
    h = pl.program_id(1)
    nb = t // BLOCK
    kr = _rope(_rms(k_ref[...], kg_ref[...]), c_ref, sa_ref, sb_ref)
    k_scr[...] = kr.astype(BF16)
    v_scr[...] = v_ref[...].astype(BF16)
    for g in range(GQA_GROUP):
        qn = _rms(q_ref[:, g * HEAD_DIM:(g + 1) * HEAD_DIM], qg_ref[...])
        q_scr[g] = _rope(qn, c_ref, sa_ref, sb_ref).astype(BF16)
    ckb = ck_ref[...].astype(BF16)
    cvb = cv_ref[...].astype(BF16)
    sk = _sink_col(sink_ref, h, BLOCK)
    rows = GQA_GROUP * BLOCK
    qi = lax.broadcasted_iota(jnp.int32, (rows, BLOCK), 0) % BLOCK
    kj = lax.broadcasted_iota(jnp.int32, (rows, BLOCK), 1)
    prev_ok = kj >= qi
    next_ok = kj <= qi
    neg = jnp.finfo(F32).min

    def attend(n, has_prev, has_next):
        if isinstance(n, int):
            r0, k0 = n * BLOCK, (n - has_prev) * BLOCK
        else:
            r0 = pl.multiple_of(n * BLOCK, BLOCK)
            k0 = pl.multiple_of((n - has_prev) * BLOCK, BLOCK)
        nk = (1 + has_prev + has_next) * BLOCK
        q4 = jnp.concatenate([q_scr[g, pl.ds(r0, BLOCK), :] for g in range(GQA_GROUP)], axis=0)
        s_win = lax.dot_general(q4, k_scr[pl.ds(k0, nk), :], NT_DIMS,
                                preferred_element_type=F32) * scale
        tiles = [s_win[:, i * BLOCK:(i + 1) * BLOCK] for i in range(nk // BLOCK)]
        if has_prev:
            tiles[0] = jnp.where(prev_ok, tiles[0], neg)
        if has_next:
            tiles[-1] = jnp.where(next_ok, tiles[-1], neg)
        s_win = jnp.concatenate(tiles, axis=1)
        s_ctx = lax.dot_general(q4, ckb, NT_DIMS, preferred_element_type=F32) * scale
        m = jnp.maximum(jnp.max(s_win, axis=1, keepdims=True),
                        jnp.max(s_ctx, axis=1, keepdims=True))
        m = jnp.maximum(m, sk)
        p_win = jnp.exp(s_win - m)
        p_ctx = jnp.exp(s_ctx - m)
        den = (jnp.sum(p_win, axis=1, keepdims=True) + jnp.sum(p_ctx, axis=1, keepdims=True)
               + jnp.exp(sk - m))
        o = (_dot(p_win.astype(BF16), v_scr[pl.ds(k0, nk), :]) + _dot(p_ctx.astype(BF16), cvb)) / den
        for g in range(GQA_GROUP):
            o_ref[pl.ds(r0, BLOCK), g * HEAD_DIM:(g + 1) * HEAD_DIM] = (
                o[g * BLOCK:(g + 1) * BLOCK, :].astype(BF16))

    def interior(n, carry):
        attend(n, True, True)
        return carry

    attend(0, False, True)
    lax.fori_loop(1, nb - 1, interior, 0, unroll=3)
    attend(nb - 1, True, False)


def _lat_attn(proj, ck, cv, sink, qg, kg, rope_c, rope_sa, rope_sb):
    m = proj.shape[0]
    t = DEC_SEQ
    gw = GQA_GROUP * HEAD_DIM
    tab = pl.BlockSpec((t, HEAD_DIM), lambda b, h: (0, 0))
    vec = pl.BlockSpec((1, HEAD_DIM), lambda b, h: (0, 0))
    cache = pl.BlockSpec((None, PAST_LEN, HEAD_DIM), lambda b, h: (b, 0, h))
    return pl.pallas_call(
        _lat_attn_kernel,
        grid=(m // t, N_KV_HEADS),
        in_specs=[
            pl.BlockSpec(memory_space=pltpu.SMEM),
            pl.BlockSpec((t, gw), lambda b, h: (b, OFF_Q // gw + h)),
            pl.BlockSpec((t, HEAD_DIM), lambda b, h: (b, OFF_K // HEAD_DIM + h)),
            pl.BlockSpec((t, HEAD_DIM), lambda b, h: (b, OFF_V // HEAD_DIM + h)),
            cache, cache, vec, vec, tab, tab, tab,
        ],
        out_specs=pl.BlockSpec((t, gw), lambda b, h: (b, h)),
        out_shape=jax.ShapeDtypeStruct((m, D_MODEL), BF16),
        scratch_shapes=[
            pltpu.VMEM((GQA_GROUP, t, HEAD_DIM), BF16),
            pltpu.VMEM((t, HEAD_DIM), BF16),
            pltpu.VMEM((t, HEAD_DIM), BF16),
        ],
        compiler_params=_cp(("arbitrary", "arbitrary"), 40),
        name="latent_attention",
    )(sink, proj, proj, proj, ck, cv, qg, kg, rope_c, rope_sa, rope_sb)


def _log_sigmoid(x):
    return jnp.minimum(x, 0.0) - jnp.log(1.0 + jnp.exp(-jnp.abs(x)))


def _split2(x):
    hi = x.astype(BF16)
    lo = (x - hi.astype(F32)).astype(BF16)
    return hi, lo


def _gla_kernel(*refs, t, has_state):
    if has_state:
        (q_ref, k_ref, v_ref, a_ref, waf_ref, wab_ref, baf_ref, bab_ref, gn_ref,
         s0f_ref, s0b_ref, og_ref, *scr) = refs
    else:
        (q_ref, k_ref, v_ref, a_ref, waf_ref, wab_ref, baf_ref, bab_ref, gn_ref,
         og_ref, sf_out, sb_out, *scr) = refs
    stf, stb, vb, ob = scr[:4]
    qsf, kuf, decf, *tmpf = scr[4:13]
    qsb, kub, decb, *tmpb = scr[13:]
    c = GLA_CHUNK
    sc = GLA_SUPER
    nsub = sc // c
    nsc = t // sc
    vb[...] = v_ref[...].astype(BF16)

    row = lax.broadcasted_iota(jnp.int32, (sc, sc), 0)
    col = lax.broadcasted_iota(jnp.int32, (sc, sc), 1)
    rc, cc = row // c, col // c

    def intra(si, fwd):
        r0 = pl.multiple_of(si * sc, sc)
        w_ref, b_ref, qs_scr, ku_scr, dec_scr, tmp, dst = (
            (waf_ref, baf_ref, qsf, kuf, decf, tmpf, og_ref) if fwd
            else (wab_ref, bab_ref, qsb, kub, decb, tmpb, ob))
        qd, kd, ke, q2, q3, amat = tmp
        tri = jnp.where((col <= row) if fwd else (col >= row), 1.0, 0.0).astype(BF16)
        dist = (rc - cc) if fwd else (cc - rc)
        x = _dot(a_ref[pl.ds(r0, sc), :].astype(BF16), w_ref[...]) + b_ref[...]
        hi, lo = _split2(_log_sigmoid(x) / GLA_TAU)
        cum = _dot(tri, hi) + _dot(tri, lo)
        zero = jnp.zeros((1, GLA_HK), F32)

        def at_start(j):
            if fwd:
                return cum[j * c - 1:j * c, :] if j > 0 else zero
            return cum[(j + 1) * c:(j + 1) * c + 1, :] if j < nsub - 1 else zero

        def at_end(j):
            return cum[(j + 1) * c - 1:(j + 1) * c, :] if fwd else cum[j * c:j * c + 1, :]

        total = at_end(nsub - 1) if fwd else at_end(0)
        dec_scr[pl.ds(pl.multiple_of(si * 8, 8), 8), :] = jnp.broadcast_to(jnp.exp(total), (8, GLA_HK))
        for j in range(nsub):
            rs = slice(j * c, (j + 1) * c)
            rows = pl.ds(pl.multiple_of(r0 + j * c, c), c)
            cj = cum[rs, :]
            cs, ce = at_start(j), at_end(j)
            q = q_ref[rows, :] * (GLA_HK ** -0.5)
            k = k_ref[rows, :]
            qdj = q * jnp.exp(cj - cs)
            kej = k * jnp.exp(ce - cj)
            qd[rs, :] = qdj.astype(BF16)
            kd[rs, :] = (k * jnp.exp(cs - cj)).astype(BF16)
            ke[rs, :] = kej.astype(BF16)
            qs_scr[rows, :] = (qdj * jnp.exp(cs)).astype(BF16)
            ku_scr[rows, :] = (kej * jnp.exp(total - ce)).astype(BF16)
            p2 = j - 2 if fwd else j + 2
            if 0 <= p2 < nsub:
                l2 = (j - 2) if fwd else j
                q2[l2 * c:(l2 + 1) * c, :] = (qdj * jnp.exp(cs - at_end(p2))).astype(BF16)
            p3 = j - 3 if fwd else j + 3
            if 0 <= p3 < nsub:
                q3[...] = (qdj * jnp.exp(cs - at_end(p3))).astype(BF16)
        nt = lambda a, b: lax.dot_general(a, b, NT_DIMS, preferred_element_type=F32)
        tril = (col <= row) if fwd else (col >= row)
        amat[...] = (jnp.where((dist == 0) & tril, nt(qd[...], kd[...]), 0.0)
                     + jnp.where(dist == 1, nt(qd[...], ke[...]), 0.0))
        r2 = slice(2 * c, sc) if fwd else slice(0, 2 * c)
        amat[r2, :] += jnp.where(dist[r2, :] == 2, nt(q2[...], ke[...]), 0.0)
        r3 = slice(3 * c, sc) if fwd else slice(0, c)
        amat[r3, :] += jnp.where(dist[r3, :] == 3, nt(q3[...], ke[...]), 0.0)
        dst[pl.ds(r0, sc), :] = _dot(amat[...].astype(BF16), vb[pl.ds(r0, sc), :])

    def intra_body(i, carry):
        intra(i, True)
        intra(nsc - 1 - i, False)
        return carry

    lax.fori_loop(0, nsc, intra_body, 0, unroll=min(nsc, 2))

    if has_state:
        for i in range(nsc):
            for fwd in (True, False):
                si = i if fwd else nsc - 1 - i
                rows = slice(si * sc, (si + 1) * sc)
                st_ref, s0_ref, qs_scr, ku_scr, dec_scr, dst = (
                    (stf, s0f_ref, qsf, kuf, decf, og_ref) if fwd
                    else (stb, s0b_ref, qsb, kub, decb, ob))
                st = s0_ref[...] if i == 0 else st_ref[...]
                dst[rows, :] += _dot(qs_scr[rows, :], st.astype(BF16))
                if i < nsc - 1:
                    dec = jnp.broadcast_to(dec_scr[si * 8:si * 8 + 1, :], (LANES, GLA_HK)).T
                    dec = jnp.concatenate([dec] * (GLA_HV // LANES), axis=1)
                    st_ref[...] = st * dec + lax.dot_general(
                        ku_scr[rows, :], vb[rows, :], TN_DIMS, preferred_element_type=F32)
    else:
        sf_out[...] = lax.dot_general(kuf[...], vb[...], TN_DIMS, preferred_element_type=F32)
        sb_out[...] = lax.dot_general(kub[...], vb[...], TN_DIMS, preferred_element_type=F32)
    og_ref[...] = _rms(og_ref[...] + ob[...], gn_ref[...])


def _gla(proj, aproj, waf, wab, baf, bab, gn, t, s0f=None, s0b=None):
    m = proj.shape[0]
    nb = m // t
    has_state = s0f is not None
    assert t % GLA_SUPER == 0 and (has_state or t == GLA_SUPER)
    sc, c = GLA_SUPER, GLA_CHUNK
    st_shape_vmem = (GLA_HK, GLA_HV) if has_state else (8, LANES)
    per_direction = [
        pltpu.VMEM((t, GLA_HK), BF16),
        pltpu.VMEM((t, GLA_HK), BF16),
        pltpu.VMEM((8 * (t // sc), GLA_HK), F32),
        pltpu.VMEM((sc, GLA_HK), BF16),
        pltpu.VMEM((sc, GLA_HK), BF16),
        pltpu.VMEM((sc, GLA_HK), BF16),
        pltpu.VMEM((sc - 2 * c, GLA_HK), BF16),
        pltpu.VMEM((sc - 3 * c, GLA_HK), BF16),
        pltpu.VMEM((sc, sc), F32),
    ]
    st_spec = pl.BlockSpec((None, GLA_HK, GLA_HV), lambda b, h: (b * GLA_HEADS + h, 0, 0))
    in_specs = [
        pl.BlockSpec((t, GLA_HK), lambda b, h: (b, OFF_GQ // GLA_HK + h)),
        pl.BlockSpec((t, GLA_HK), lambda b, h: (b, OFF_GK // GLA_HK + h)),
        pl.BlockSpec((t, GLA_HV), lambda b, h: (b, OFF_GV // GLA_HV + h)),
        pl.BlockSpec((t, LANES), lambda b, h: (b, 0)),
        pl.BlockSpec((LANES, GLA_HK), lambda b, h: (0, h)),
        pl.BlockSpec((LANES, GLA_HK), lambda b, h: (0, h)),
        pl.BlockSpec((1, GLA_HK), lambda b, h: (0, h)),
        pl.BlockSpec((1, GLA_HK), lambda b, h: (0, h)),
        pl.BlockSpec((1, GLA_HV), lambda b, h: (0, h)),
    ]
    args = [proj, proj, proj, aproj, waf, wab, baf, bab, gn]
    og_spec = pl.BlockSpec((t, GLA_HV), lambda b, h: (b, h))
    og_shape = jax.ShapeDtypeStruct((m, GLA_DV), F32)
    if has_state:
        in_specs += [st_spec, st_spec]
        args += [s0f, s0b]
        out_specs, out_shape = og_spec, og_shape
    else:
        st_shape = jax.ShapeDtypeStruct((nb * GLA_HEADS, GLA_HK, GLA_HV), F32)
        out_specs, out_shape = [og_spec, st_spec, st_spec], [og_shape, st_shape, st_shape]
    return pl.pallas_call(
        functools.partial(_gla_kernel, t=t, has_state=has_state),
        grid=(nb, GLA_HEADS),
        in_specs=in_specs,
        out_specs=out_specs,
        out_shape=out_shape,
        scratch_shapes=[
            pltpu.VMEM(st_shape_vmem, F32),
            pltpu.VMEM(st_shape_vmem, F32),
            pltpu.VMEM((t, GLA_HV), BF16),
            pltpu.VMEM((t, GLA_HV), F32),
        ] + 2 * per_direction,
        compiler_params=_cp(("arbitrary", "arbitrary"), 56),
        name="gla_state" if has_state else "gla_zero",
    )(*args)


MERGE_SLAB = (16, 1024)


def _merge_kernel(oa_ref, og_ref, gr_ref, ga_ref, gg_ref, o_ref):
    rows, cols = o_ref.shape
    for r0 in range(0, rows, MERGE_SLAB[0]):
        for c0 in range(0, cols, MERGE_SLAB[1]):
            sl = (slice(r0, r0 + MERGE_SLAB[0]), slice(c0, c0 + MERGE_SLAB[1]))
            gr = gr_ref[sl].astype(F32)
            o_gla = og_ref[sl] * (gr * _sigmoid(gr))
            merged = (_sigmoid(ga_ref[sl].astype(F32)) * oa_ref[sl].astype(F32)
                      + _sigmoid(gg_ref[sl].astype(F32)) * o_gla)
            o_ref[sl] = merged.astype(BF16)


MERGE_TM = 256


def _merge(o_att, og, gates):
    m = o_att.shape[0]
    blk = lambda cb: pl.BlockSpec((MERGE_TM, D_MODEL), lambda i: (i, cb))
    cols = ((OFF_GR - OFF_GR) // D_MODEL, (OFF_GATT - OFF_GR) // D_MODEL, (OFF_GGLA - OFF_GR) // D_MODEL)
    return pl.pallas_call(
        _merge_kernel,
        grid=(m // MERGE_TM,),
        in_specs=[blk(0), blk(0)] + [blk(cb) for cb in cols],
        out_specs=blk(0),
        out_shape=jax.ShapeDtypeStruct((m, D_MODEL), BF16),
        compiler_params=_cp(("arbitrary",), 40),
        name="branch_merge",
    )(o_att, og, gates, gates, gates)


def _merge_side(o_att, og, gates, rows):
    m = o_att.shape[0]
    cols = ((OFF_GR - OFF_GR) // D_MODEL, (OFF_GATT - OFF_GR) // D_MODEL, (OFF_GGLA - OFF_GR) // D_MODEL)
    return _Side(_merge_kernel, (o_att, og, gates, gates, gates), (0, 0) + cols,
                 (jax.ShapeDtypeStruct((m, D_MODEL), BF16),), rows, m // rows)


OUT_TM = 512
OUT_TN = 1024


def _out_kernel(mg_ref, w_ref, x_ref, gt_ref, g2_ref, sh_ref, sc_ref, x1_ref, h2_ref, x1_scr):
    j = pl.program_id(1)
    nj = D_MODEL // OUT_TN
    slab = MXU_ACC_ROWS * 512 // OUT_TN
    for r0 in range(0, OUT_TM, slab):
        rs = slice(r0, r0 + slab)
        x1 = x_ref[rs, :] + gt_ref[...] * _dot(mg_ref[rs, :], w_ref[...])
        x1_ref[rs, :] = x1
        x1_scr[j, rs, :] = x1

    @pl.when(j == nj - 1)
    def _():
        ssq = jnp.zeros((OUT_TM, 1), F32)
        for jj in range(nj):
            xs = x1_scr[jj]
            ssq = ssq + jnp.sum(xs * xs, axis=-1, keepdims=True)
        inv = lax.rsqrt(ssq / D_MODEL + EPS)
        for jj in range(nj):
            cs = slice(jj * OUT_TN, (jj + 1) * OUT_TN)
            y = x1_scr[jj] * inv * g2_ref[:, cs]
            h2_ref[:, cs] = (y * (1.0 + sc_ref[:, cs]) + sh_ref[:, cs]).astype(BF16)


def _out_proj(merged, w_out, x, gate1, g2, shift2, scale2, mod_off, rows_per_mod):
    m = x.shape[0]
    bpb = rows_per_mod // OUT_TM
    gate_spec = pl.BlockSpec((None, 1, OUT_TN), lambda i, j: (mod_off + i // bpb, 0, j))
    return pl.pallas_call(
        _out_kernel,
        grid=(m // OUT_TM, D_MODEL // OUT_TN),
        in_specs=[
            pl.BlockSpec((OUT_TM, D_MODEL), lambda i, j: (i, 0)),
            pl.BlockSpec((None, D_MODEL, OUT_TN), lambda i, j: (j, 0, 0)),
            pl.BlockSpec((OUT_TM, OUT_TN), lambda i, j: (i, j)),
            gate_spec,
            pl.BlockSpec((1, D_MODEL), lambda i, j: (0, 0)),
            _mod_spec(mod_off, bpb),
            _mod_spec(mod_off, bpb),
        ],
        out_specs=[
            pl.BlockSpec((OUT_TM, OUT_TN), lambda i, j: (i, j)),
            pl.BlockSpec((OUT_TM, D_MODEL), lambda i, j: (i, 0)),
        ],
        out_shape=[
            jax.ShapeDtypeStruct((m, D_MODEL), F32),
            jax.ShapeDtypeStruct((m, D_MODEL), BF16),
        ],
        scratch_shapes=[pltpu.VMEM((D_MODEL // OUT_TN, OUT_TM, OUT_TN), F32)],
        compiler_params=_cp(("arbitrary", "arbitrary"), 56),
        name="out_proj_residual_norm",
    )(merged, w_out, x, gate1, g2, shift2, scale2)


FFI_TM = 2048
FFI_TN = 256


def _ffn_in_kernel(h_ref, wg_ref, wu_ref, o_ref):
    wg = wg_ref[...].astype(BF16)
    wu = wu_ref[...].astype(BF16)
    slab = MXU_ACC_ROWS * 512 // FFI_TN
    for r0 in range(0, FFI_TM, slab):
        rs = slice(r0, r0 + slab)
        h = h_ref[rs, :]
        g = _dot(h, wg)
        u = _dot(h, wu)
        o_ref[rs, :] = (g * _sigmoid(g) * u).astype(BF16)


def _ffn_in(h2, w):
    m = h2.shape[0]
    return pl.pallas_call(
        _ffn_in_kernel,
        grid=(m // FFI_TM, D_FF // FFI_TN),
        in_specs=[
            pl.BlockSpec((FFI_TM, D_MODEL), lambda i, j: (i, 0)),
            pl.BlockSpec((D_MODEL, FFI_TN), lambda i, j: (0, j)),
            pl.BlockSpec((D_MODEL, FFI_TN), lambda i, j: (0, D_FF // FFI_TN + j)),
        ],
        out_specs=pl.BlockSpec((FFI_TM, FFI_TN), lambda i, j: (i, j)),
        out_shape=jax.ShapeDtypeStruct((m, D_FF), BF16),
        compiler_params=_cp(("arbitrary", "arbitrary"), 58),
        name="ffn_in_swiglu",
    )(h2, w, w)


FFO_TM = 512
FFO_TN = 512


def _ffn_out_kernel(a_ref, w_ref, x_ref, gt_ref, o_ref):
    o_ref[...] = x_ref[...] + gt_ref[...] * _dot(a_ref[...], w_ref[...])


def _ffn_out(act, w, x1, gate2, mod_off, rows_per_mod, side=None):
    m = x1.shape[0]
    bpb = rows_per_mod // FFO_TM
    in_specs = [
        pl.BlockSpec((FFO_TM, D_FF), lambda i, j: (i, 0)),
        pl.BlockSpec((None, D_FF, FFO_TN), lambda i, j: (j, 0, 0)),
        pl.BlockSpec((FFO_TM, FFO_TN), lambda i, j: (i, j)),
        pl.BlockSpec((None, 1, FFO_TN), lambda i, j: (mod_off + i // bpb, 0, j)),
    ]
    return _host_call(
        _ffn_out_kernel, (m // FFO_TM, D_MODEL // FFO_TN), in_specs,
        pl.BlockSpec((FFO_TM, FFO_TN), lambda i, j: (i, j)),
        jax.ShapeDtypeStruct((m, D_MODEL), F32), (act, w, x1, gate2), side, 60, "ffn_out_residual")


def _rope_tables(t):
    rows = t // GRID_W
    half = HEAD_DIM // 2
    row = jnp.repeat(jnp.arange(rows, dtype=F32), GRID_W)
    col = jnp.tile(jnp.arange(GRID_W, dtype=F32), rows)
    inv = ROPE_THETA ** (-jnp.arange(0, half, 2, dtype=F32) / half)
    ar = row[:, None] * inv[None, :]
    ac = col[:, None] * inv[None, :]
    cr, sr, cc, sc = jnp.cos(ar), jnp.sin(ar), jnp.cos(ac), jnp.sin(ac)
    z = jnp.zeros_like(sr)
    tab_c = jnp.concatenate([cr, cr, cc, cc], axis=1)
    tab_sa = jnp.concatenate([-sr, z, -sc, z], axis=1)
    tab_sb = jnp.concatenate([z, sr, z, sc], axis=1)
    return tab_c, tab_sa, tab_sb


def kernel(x_prompt, x_sample, c, cache_k, cache_v, state_gla_fwd, state_gla_bwd, c_ctx, w_ada, b_ada, norm1_g, norm2_g, w_in, q_norm_g, k_norm_g, attn_sink, w_a2_fwd, b_a_fwd, w_a2_bwd, b_a_bwd, gla_norm_g, w_out, w_ffn_in, w_ffn_out):
    assert w_ada.shape[0] == 1, "single trunk layer"
    cc = jnp.zeros((MOD_ROWS, D_MODEL), F32).at[0].set(c_ctx).at[1:1 + DEC_BATCH].set(c)
    mod_all = _ada(cc, w_ada[0], b_ada[0][None, :])
    mod = tuple(mod_all[:, i * D_MODEL:(i + 1) * D_MODEL].reshape(MOD_ROWS, 1, D_MODEL)
                for i in range(N_MOD))

    r = GLA_GATE_RANK
    wa = jnp.zeros((D_MODEL, LANES), BF16).at[:, :2 * r].set(w_in[0][:, D_WIDE:].astype(BF16))
    waf = jnp.zeros((LANES, GLA_DK), BF16).at[:r].set(w_a2_fwd[0].astype(BF16))
    wab = jnp.zeros((LANES, GLA_DK), BF16).at[r:2 * r].set(w_a2_bwd[0].astype(BF16))
    g1, g2 = norm1_g[0][None, :], norm2_g[0][None, :]
    w_in_t = w_in[0].T
    qg, kg, sink = q_norm_g[0][None, :], k_norm_g[0][None, :], attn_sink[0]
    gla_w = (waf, wab, b_a_fwd[0][None, :], b_a_bwd[0][None, :], gla_norm_g[0][None, :])
    shift1, scale1, gate1, shift2, scale2, gate2 = mod
    kvw = N_KV_HEADS * HEAD_DIM
    n_wide, n_gate = OFF_GR, D_WIDE - OFF_GR
    m_ctx, m_lat = BATCH * SEQ, DEC_BATCH * DEC_SEQ
    ctx_mod, lat_mod = (0, m_ctx), (1, DEC_SEQ)

    xp = x_prompt.reshape(m_ctx, D_MODEL)
    xs = x_sample.reshape(m_lat, D_MODEL)

    h_c, a_c = _norm_mod(xp, g1, shift1, scale1, wa, *ctx_mod)
    proj_c, w_fo = _proj(h_c, w_in_t, 0, n_wide, F32, _cast_side(w_ffn_out[0], 128, FFO_TN))
    gates_c, w_o = _proj(h_c, w_in_t, n_wide, n_gate, BF16, _cast_side(w_out[0], 64, OUT_TN))
    oatt_c, new_k, new_v = _ctx_attn(proj_c, sink, qg, kg)
    og_c, s_f, s_b = _gla(proj_c, a_c, *gla_w, SEQ)

    h_l, a_l = _norm_mod(xs, g1, shift1, scale1, wa, *lat_mod)
    merged_c = _merge(oatt_c, og_c, gates_c)
    proj_l = _proj(h_l, w_in_t, 0, n_wide, F32)
    gates_l = _proj(h_l, w_in_t, n_wide, n_gate, BF16)
    oatt_l = _lat_attn(
        proj_l,
        cache_k[:, 0].reshape(DEC_BATCH, PAST_LEN, kvw),
        cache_v[:, 0].reshape(DEC_BATCH, PAST_LEN, kvw),
        sink, qg, kg, *_rope_tables(DEC_SEQ))
    og_l = _gla(proj_l, a_l, *gla_w, DEC_SEQ,
                state_gla_fwd[:, 0].reshape(DEC_BATCH * GLA_HEADS, GLA_HK, GLA_HV),
                state_gla_bwd[:, 0].reshape(DEC_BATCH * GLA_HEADS, GLA_HK, GLA_HV))

    x1_c, h2_c = _out_proj(merged_c, w_o, xp, gate1, g2, shift2, scale2, *ctx_mod)
    act_c = _ffn_in(h2_c, w_ffn_in[0])
    yp, merged_l = _ffn_out(act_c, w_fo, x1_c, gate2, *ctx_mod,
                            side=_merge_side(oatt_l, og_l, gates_l, 64))

    x1_l, h2_l = _out_proj(merged_l, w_o, xs, gate1, g2, shift2, scale2, *lat_mod)
    act_l = _ffn_in(h2_l, w_ffn_in[0])
    ys = _ffn_out(act_l, w_fo, x1_l, gate2, *lat_mod)
    return (
        yp.reshape(BATCH, SEQ, D_MODEL),
        ys.reshape(DEC_BATCH, DEC_SEQ, D_MODEL),
        new_k.reshape(BATCH, 1, SEQ, N_KV_HEADS, HEAD_DIM),
        new_v.reshape(BATCH, 1, SEQ, N_KV_HEADS, HEAD_DIM),
        s_f.reshape(BATCH, 1, GLA_HEADS, GLA_HK, GLA_HV),
        s_b.reshape(BATCH, 1, GLA_HEADS, GLA_HK, GLA_HV),
    )
```

```python
import functools
from typing import Callable, NamedTuple

import jax
import jax.numpy as jnp
from jax import lax
from jax.experimental import pallas as pl
from jax.experimental.pallas import tpu as pltpu

F32 = jnp.float32
BF16 = jnp.bfloat16

D_MODEL = 4096
BATCH = 32
SEQ = 256
DEC_BATCH = 8
DEC_SEQ = 1024
PAST_LEN = 256
GRID_W = 64
HEAD_DIM = 128
N_Q_HEADS = 32
N_KV_HEADS = 8
GQA_GROUP = N_Q_HEADS // N_KV_HEADS
WINDOW = 128
BLOCK = 128
ROPE_THETA = 10000.0
GLA_HEADS = 4
GLA_DK = D_MODEL // 2
GLA_DV = D_MODEL
GLA_HK = GLA_DK // GLA_HEADS
GLA_HV = GLA_DV // GLA_HEADS
GLA_GATE_RANK = 16
GLA_TAU = 16.0
GLA_CHUNK = 64
GLA_SUPER = 256
D_FF = 11008
N_MOD = 6
EPS = 1e-6

OFF_Q = 0
OFF_K = 4096
OFF_V = 5120
OFF_GQ = 6144
OFF_GK = 8192
OFF_GV = 10240
OFF_GR = 14336
OFF_GATT = 18432
OFF_GGLA = 22528
D_WIDE = 26624
LANES = 128
MXU_ACC_ROWS = 512
MOD_ROWS = 16

MIB = 1024 * 1024
VMEM_BUDGET = 60 * MIB
VMEM_SLACK = 2 * MIB
NT_DIMS = (((1,), (1,)), ((), ()))
TN_DIMS = (((0,), (0,)), ((), ()))


def _cp(sem, vmem_mib):
    return pltpu.CompilerParams(dimension_semantics=sem, vmem_limit_bytes=vmem_mib * MIB)


def _rms(x, g):
    ms = jnp.mean(x * x, axis=-1, keepdims=True)
    return x * lax.rsqrt(ms + EPS) * g


def _sigmoid(x):
    return 0.5 * jnp.tanh(0.5 * x) + 0.5


def _dot(a, b):
    return jnp.dot(a, b, preferred_element_type=F32)


ADA_TN = 512


def _ada_kernel(c_ref, w_ref, b_ref, o_ref):
    c = c_ref[...]
    s = (c * _sigmoid(c)).astype(BF16)
    o_ref[...] = _dot(s, w_ref[...].astype(BF16)) + b_ref[...]


def _ada(cc, w_ada, b_ada):
    n = w_ada.shape[1]
    return pl.pallas_call(
        _ada_kernel,
        grid=(n // ADA_TN,),
        in_specs=[
            pl.BlockSpec((MOD_ROWS, D_MODEL), lambda j: (0, 0)),
            pl.BlockSpec((D_MODEL, ADA_TN), lambda j: (0, j)),
            pl.BlockSpec((1, ADA_TN), lambda j: (0, j)),
        ],
        out_specs=pl.BlockSpec((MOD_ROWS, ADA_TN), lambda j: (0, j)),
        out_shape=jax.ShapeDtypeStruct((MOD_ROWS, n), F32),
        compiler_params=_cp(("arbitrary",), 40),
        name="ada_ln",
    )(cc, w_ada, b_ada)


NORM_TM = 512
PROJ_TM = 2048
PROJ_TN = 512


def _mod_spec(mod_off, bpb):
    return pl.BlockSpec((None, 1, D_MODEL), lambda i, *_: (mod_off + i // bpb, 0, 0))


def _norm_kernel(x_ref, g_ref, sh_ref, sc_ref, wa_ref, h_ref, a_ref):
    h = _rms(x_ref[...], g_ref[...]) * (1.0 + sc_ref[...]) + sh_ref[...]
    hb = h.astype(BF16)
    h_ref[...] = hb
    a_ref[...] = _dot(hb, wa_ref[...])


def _norm_mod(x, g1, shift, scale, wa, mod_off, rows_per_mod):
    m = x.shape[0]
    bpb = rows_per_mod // NORM_TM
    return pl.pallas_call(
        _norm_kernel,
        grid=(m // NORM_TM,),
        in_specs=[
            pl.BlockSpec((NORM_TM, D_MODEL), lambda i: (i, 0)),
            pl.BlockSpec((1, D_MODEL), lambda i: (0, 0)),
            _mod_spec(mod_off, bpb),
            _mod_spec(mod_off, bpb),
            pl.BlockSpec((D_MODEL, LANES), lambda i: (0, 0)),
        ],
        out_specs=[
            pl.BlockSpec((NORM_TM, D_MODEL), lambda i: (i, 0)),
            pl.BlockSpec((NORM_TM, LANES), lambda i: (i, 0)),
        ],
        out_shape=[
            jax.ShapeDtypeStruct((m, D_MODEL), BF16),
            jax.ShapeDtypeStruct((m, LANES), F32),
        ],
        compiler_params=_cp(("arbitrary",), 40),
        name="norm_modulate",
    )(x, g1, shift, scale, wa)


class _Side(NamedTuple):
    fn: Callable
    ins: tuple
    in_cols: tuple
    outs: tuple
    rows: int
    nblk: int

    def vmem_bytes(self):
        width = self.outs[0].shape[1]
        per_row = sum(a.dtype.itemsize for a in self.ins) + sum(o.dtype.itemsize for o in self.outs)
        return 2 * self.rows * width * per_row

    def specs(self, nj):
        def spec(shape_cols, cb):
            return pl.BlockSpec((self.rows, shape_cols),
                                lambda i, j: (jnp.minimum(i * nj + j, self.nblk - 1), cb))
        in_specs = [spec(self.outs[0].shape[1], cb) for cb in self.in_cols]
        out_specs = [spec(o.shape[1], 0) for o in self.outs]
        return in_specs, out_specs


def _host_kernel(body, n_in, n_out, side):
    def kern(*refs):
        n_sin = len(side.ins) if side else 0
        if side:
            side.fn(*refs[n_in:n_in + n_sin], *refs[n_in + n_sin + n_out:])
        body(*refs[:n_in], *refs[n_in + n_sin:n_in + n_sin + n_out])
    return kern


def _host_call(body, grid, in_specs, out_spec, out_shape, args, side, vmem_mib, name):
    if side is None:
        s_in, s_out = [], []
    else:
        assert grid[0] * grid[1] >= side.nblk
        s_in, s_out = side.specs(grid[1])
    res = pl.pallas_call(
        _host_kernel(body, len(in_specs), 1, side),
        grid=grid,
        in_specs=in_specs + s_in,
        out_specs=[out_spec] + s_out,
        out_shape=[out_shape] + (list(side.outs) if side else []),
        compiler_params=_cp(("arbitrary", "arbitrary"), vmem_mib),
        name=name,
    )(*args, *(side.ins if side else ()))
    return res[0] if side is None else res


def _cast_kernel(src_ref, dst_ref):
    dst_ref[...] = src_ref[...].astype(BF16)


def _cast_side(w, rows):
    n, d = w.shape
    return _Side(_cast_kernel, (w,), (0,), (jax.ShapeDtypeStruct((n, d), BF16),), rows, n // rows)


def _proj_kernel(h_ref, wt_ref, o_ref):
    wt = wt_ref[...].astype(BF16)
    for r0 in range(0, PROJ_TM, MXU_ACC_ROWS):
        rs = slice(r0, r0 + MXU_ACC_ROWS)
        o_ref[rs, :] = lax.dot_general(h_ref[rs, :], wt, NT_DIMS,
                                       preferred_element_type=F32).astype(o_ref.dtype)


def _proj(h, wt, col0, ncols, out_dtype, side=None):
    m = h.shape[0]
    j0 = col0 // PROJ_TN
    fixed = 2 * PROJ_TN * D_MODEL * 4 + 2 * PROJ_TM * PROJ_TN * jnp.dtype(out_dtype).itemsize
    fixed += side.vmem_bytes() if side else 0
    h_tile = PROJ_TM * D_MODEL * h.dtype.itemsize
    h_bufs = 2 if fixed + 2 * h_tile + VMEM_SLACK <= VMEM_BUDGET else 1
    in_specs = [
        pl.BlockSpec((PROJ_TM, D_MODEL), lambda i, j: (i, 0), pipeline_mode=pl.Buffered(h_bufs)),
        pl.BlockSpec((PROJ_TN, D_MODEL), lambda i, j: (j0 + j, 0)),
    ]
    return _host_call(
        _proj_kernel, (m // PROJ_TM, ncols // PROJ_TN), in_specs,
        pl.BlockSpec((PROJ_TM, PROJ_TN), lambda i, j: (i, j)),
        jax.ShapeDtypeStruct((m, ncols), out_dtype), (h, wt), side, VMEM_BUDGET // MIB, "in_proj")


def _sink_col(sink_ref, h, rows_per_head):
    rid = lax.broadcasted_iota(jnp.int32, (GQA_GROUP * rows_per_head, 1), 0) // rows_per_head
    col = jnp.full(rid.shape, sink_ref[h * GQA_GROUP], F32)
    for g in range(1, GQA_GROUP):
        col = jnp.where(rid == g, sink_ref[h * GQA_GROUP + g], col)
    return col


def _ctx_attn_kernel(sink_ref, q_ref, k_ref, v_ref, qg_ref, kg_ref, o_ref, nk_ref, nv_ref):
    scale = HEAD_DIM ** -0.5
    nv_ref[...] = v_ref[...]
    for h in range(N_KV_HEADS):
        hs = slice(h * HEAD_DIM, (h + 1) * HEAD_DIM)
        kn = _rms(k_ref[:, hs], kg_ref[...])
        nk_ref[:, hs] = kn
        kb = kn.astype(BF16)
        vb = v_ref[:, hs].astype(BF16)
        qs = []
        for g in range(GQA_GROUP):
            c0 = (h * GQA_GROUP + g) * HEAD_DIM
            qs.append(_rms(q_ref[:, c0:c0 + HEAD_DIM], qg_ref[...]).astype(BF16))
        q4 = jnp.concatenate(qs, axis=0)
        s = lax.dot_general(q4, kb, NT_DIMS, preferred_element_type=F32) * scale
        sk = _sink_col(sink_ref, h, SEQ)
        m = jnp.maximum(jnp.max(s, axis=1, keepdims=True), sk)
        p = jnp.exp(s - m)
        den = jnp.sum(p, axis=1, keepdims=True) + jnp.exp(sk - m)
        o = _dot(p.astype(BF16), vb) / den
        for g in range(GQA_GROUP):
            c0 = (h * GQA_GROUP + g) * HEAD_DIM
            o_ref[:, c0:c0 + HEAD_DIM] = o[g * SEQ:(g + 1) * SEQ, :].astype(BF16)


def _ctx_attn(proj, sink, qg, kg):
    m = proj.shape[0]
    kvw = N_KV_HEADS * HEAD_DIM
    return pl.pallas_call(
        _ctx_attn_kernel,
        grid=(m // SEQ,),
        in_specs=[
            pl.BlockSpec(memory_space=pltpu.SMEM),
            pl.BlockSpec((SEQ, D_MODEL), lambda b: (b, OFF_Q // D_MODEL)),
            pl.BlockSpec((SEQ, kvw), lambda b: (b, OFF_K // kvw)),
            pl.BlockSpec((SEQ, kvw), lambda b: (b, OFF_V // kvw)),
            pl.BlockSpec((1, HEAD_DIM), lambda b: (0, 0)),
            pl.BlockSpec((1, HEAD_DIM), lambda b: (0, 0)),
        ],
        out_specs=[
            pl.BlockSpec((SEQ, D_MODEL), lambda b: (b, 0)),
            pl.BlockSpec((SEQ, kvw), lambda b: (b, 0)),
            pl.BlockSpec((SEQ, kvw), lambda b: (b, 0)),
        ],
        out_shape=[
            jax.ShapeDtypeStruct((m, D_MODEL), BF16),
            jax.ShapeDtypeStruct((m, kvw), F32),
            jax.ShapeDtypeStruct((m, kvw), F32),
        ],
        compiler_params=_cp(("arbitrary",), 40),
        name="ctx_attention",
    )(sink, proj, proj, proj, qg, kg)


def _rope(x, c_ref, sa_ref, sb_ref):
    up = pltpu.roll(x, HEAD_DIM - 32, axis=1)
    dn = pltpu.roll(x, 32, axis=1)
    return x * c_ref[...] + up * sa_ref[...] + dn * sb_ref[...]


def _lat_attn_kernel(sink_ref, q_ref, k_ref, v_ref, ck_ref, cv_ref, qg_ref, kg_ref,
                     c_ref, sa_ref, sb_ref, o_ref, q_scr, k_scr, v_scr):
    t = DEC_SEQ
    scale = HEAD_DIM ** -0.5
    h = pl.program_id(1)
    nb = t // BLOCK
    kr = _rope(_rms(k_ref[...], kg_ref[...]), c_ref, sa_ref, sb_ref)
    k_scr[...] = kr.astype(BF16)
    v_scr[...] = v_ref[...].astype(BF16)
    for g in range(GQA_GROUP):
        qn = _rms(q_ref[:, g * HEAD_DIM:(g + 1) * HEAD_DIM], qg_ref[...])
        q_scr[g] = _rope(qn, c_ref, sa_ref, sb_ref).astype(BF16)
    ckb = ck_ref[...].astype(BF16)
    cvb = cv_ref[...].astype(BF16)
    sk = _sink_col(sink_ref, h, BLOCK)
    rows = GQA_GROUP * BLOCK
    qi = lax.broadcasted_iota(jnp.int32, (rows, BLOCK), 0) % BLOCK
    kj = lax.broadcasted_iota(jnp.int32, (rows, BLOCK), 1)
    prev_ok = kj >= qi
    next_ok = kj <= qi
    neg = jnp.finfo(F32).min

    def attend(n, has_prev, has_next):
        if isinstance(n, int):
            r0, k0 = n * BLOCK, (n - has_prev) * BLOCK
        else:
            r0 = pl.multiple_of(n * BLOCK, BLOCK)
            k0 = pl.multiple_of((n - has_prev) * BLOCK, BLOCK)
        nk = (1 + has_prev + has_next) * BLOCK
        q4 = jnp.concatenate([q_scr[g, pl.ds(r0, BLOCK), :] for g in range(GQA_GROUP)], axis=0)
        s_win = lax.dot_general(q4, k_scr[pl.ds(k0, nk), :], NT_DIMS,
                                preferred_element_type=F32) * scale
        tiles = [s_win[:, i * BLOCK:(i + 1) * BLOCK] for i in range(nk // BLOCK)]
        if has_prev:
            tiles[0] = jnp.where(prev_ok, tiles[0], neg)
        if has_next:
            tiles[-1] = jnp.where(next_ok, tiles[-1], neg)
        s_win = jnp.concatenate(tiles, axis=1)
        s_ctx = lax.dot_general(q4, ckb, NT_DIMS, preferred_element_type=F32) * scale
        m = jnp.maximum(jnp.max(s_win, axis=1, keepdims=True),
                        jnp.max(s_ctx, axis=1, keepdims=True))
        m = jnp.maximum(m, sk)
        p_win = jnp.exp(s_win - m)
        p_ctx = jnp.exp(s_ctx - m)
        den = (jnp.sum(p_win, axis=1, keepdims=True) + jnp.sum(p_ctx, axis=1, keepdims=True)
               + jnp.exp(sk - m))
        o = (_dot(p_win.astype(BF16), v_scr[pl.ds(k0, nk), :]) + _dot(p_ctx.astype(BF16), cvb)) / den
        for g in range(GQA_GROUP):
            o_ref[pl.ds(r0, BLOCK), g * HEAD_DIM:(g + 1) * HEAD_DIM] = (
                o[g * BLOCK:(g + 1) * BLOCK, :].astype(BF16))

    def interior(n, carry):
        attend(n, True, True)
        return carry

    attend(0, False, True)
    lax.fori_loop(1, nb - 1, interior, 0, unroll=3)
    attend(nb - 1, True, False)


def _lat_attn(proj, ck, cv, sink, qg, kg, rope_c, rope_sa, rope_sb):
    m = proj.shape[0]
    t = DEC_SEQ
    gw = GQA_GROUP * HEAD_DIM
    tab = pl.BlockSpec((t, HEAD_DIM), lambda b, h: (0, 0))
    vec = pl.BlockSpec((1, HEAD_DIM), lambda b, h: (0, 0))
    cache = pl.BlockSpec((None, PAST_LEN, HEAD_DIM), lambda b, h: (b, 0, h))
    return pl.pallas_call(
        _lat_attn_kernel,
        grid=(m // t, N_KV_HEADS),
        in_specs=[
            pl.BlockSpec(memory_space=pltpu.SMEM),
            pl.BlockSpec((t, gw), lambda b, h: (b, OFF_Q // gw + h)),
            pl.BlockSpec((t, HEAD_DIM), lambda b, h: (b, OFF_K // HEAD_DIM + h)),
            pl.BlockSpec((t, HEAD_DIM), lambda b, h: (b, OFF_V // HEAD_DIM + h)),
            cache, cache, vec, vec, tab, tab, tab,
        ],
        out_specs=pl.BlockSpec((t, gw), lambda b, h: (b, h)),
        out_shape=jax.ShapeDtypeStruct((m, D_MODEL), BF16),
        scratch_shapes=[
            pltpu.VMEM((GQA_GROUP, t, HEAD_DIM), BF16),
            pltpu.VMEM((t, HEAD_DIM), BF16),
            pltpu.VMEM((t, HEAD_DIM), BF16),
        ],
        compiler_params=_cp(("arbitrary", "arbitrary"), 40),
        name="latent_attention",
    )(sink, proj, proj, proj, ck, cv, qg, kg, rope_c, rope_sa, rope_sb)


def _log_sigmoid(x):
    return jnp.minimum(x, 0.0) - jnp.log(1.0 + jnp.exp(-jnp.abs(x)))


def _split2(x):
    hi = x.astype(BF16)
    lo = (x - hi.astype(F32)).astype(BF16)
    return hi, lo


def _gla_kernel(*refs, t, has_state):
    if has_state:
        (q_ref, k_ref, v_ref, a_ref, waf_ref, wab_ref, baf_ref, bab_ref, gn_ref,
         s0f_ref, s0b_ref, og_ref, *scr) = refs
    else:
        (q_ref, k_ref, v_ref, a_ref, waf_ref, wab_ref, baf_ref, bab_ref, gn_ref,
         og_ref, sf_out, sb_out, *scr) = refs
    stf, stb, vb, ob = scr[:4]
    qsf, kuf, decf, *tmpf = scr[4:13]
    qsb, kub, decb, *tmpb = scr[13:]
    c = GLA_CHUNK
    sc = GLA_SUPER
    nsub = sc // c
    nsc = t // sc
    vb[...] = v_ref[...].astype(BF16)

    row = lax.broadcasted_iota(jnp.int32, (sc, sc), 0)
    col = lax.broadcasted_iota(jnp.int32, (sc, sc), 1)
    rc, cc = row // c, col // c

    def intra(si, fwd):
        r0 = pl.multiple_of(si * sc, sc)
        w_ref, b_ref, qs_scr, ku_scr, dec_scr, tmp, dst = (
            (waf_ref, baf_ref, qsf, kuf, decf, tmpf, og_ref) if fwd
            else (wab_ref, bab_ref, qsb, kub, decb, tmpb, ob))
        qd, kd, ke, q2, q3, amat = tmp
        tri = jnp.where((col <= row) if fwd else (col >= row), 1.0, 0.0).astype(BF16)
        dist = (rc - cc) if fwd else (cc - rc)
        x = _dot(a_ref[pl.ds(r0, sc), :].astype(BF16), w_ref[...]) + b_ref[...]
        hi, lo = _split2(_log_sigmoid(x) / GLA_TAU)
        cum = _dot(tri, hi) + _dot(tri, lo)
        zero = jnp.zeros((1, GLA_HK), F32)

        def at_start(j):
            if fwd:
                return cum[j * c - 1:j * c, :] if j > 0 else zero
            return cum[(j + 1) * c:(j + 1) * c + 1, :] if j < nsub - 1 else zero

        def at_end(j):
            return cum[(j + 1) * c - 1:(j + 1) * c, :] if fwd else cum[j * c:j * c + 1, :]

        total = at_end(nsub - 1) if fwd else at_end(0)
        dec_scr[pl.ds(pl.multiple_of(si * 8, 8), 8), :] = jnp.broadcast_to(jnp.exp(total), (8, GLA_HK))
        for j in range(nsub):
            rs = slice(j * c, (j + 1) * c)
            rows = pl.ds(pl.multiple_of(r0 + j * c, c), c)
            cj = cum[rs, :]
            cs, ce = at_start(j), at_end(j)
            q = q_ref[rows, :] * (GLA_HK ** -0.5)
            k = k_ref[rows, :]
            qdj = q * jnp.exp(cj - cs)
            kej = k * jnp.exp(ce - cj)
            qd[rs, :] = qdj.astype(BF16)
            kd[rs, :] = (k * jnp.exp(cs - cj)).astype(BF16)
            ke[rs, :] = kej.astype(BF16)
            qs_scr[rows, :] = (qdj * jnp.exp(cs)).astype(BF16)
            ku_scr[rows, :] = (kej * jnp.exp(total - ce)).astype(BF16)
            p2 = j - 2 if fwd else j + 2
            if 0 <= p2 < nsub:
                l2 = (j - 2) if fwd else j
                q2[l2 * c:(l2 + 1) * c, :] = (qdj * jnp.exp(cs - at_end(p2))).astype(BF16)
            p3 = j - 3 if fwd else j + 3
            if 0 <= p3 < nsub:
                q3[...] = (qdj * jnp.exp(cs - at_end(p3))).astype(BF16)
        nt = lambda a, b: lax.dot_general(a, b, NT_DIMS, preferred_element_type=F32)
        tril = (col <= row) if fwd else (col >= row)
        amat[...] = (jnp.where((dist == 0) & tril, nt(qd[...], kd[...]), 0.0)
                     + jnp.where(dist == 1, nt(qd[...], ke[...]), 0.0))
        r2 = slice(2 * c, sc) if fwd else slice(0, 2 * c)
        amat[r2, :] += jnp.where(dist[r2, :] == 2, nt(q2[...], ke[...]), 0.0)
        r3 = slice(3 * c, sc) if fwd else slice(0, c)
        amat[r3, :] += jnp.where(dist[r3, :] == 3, nt(q3[...], ke[...]), 0.0)
        dst[pl.ds(r0, sc), :] = _dot(amat[...].astype(BF16), vb[pl.ds(r0, sc), :])

    def intra_body(i, carry):
        intra(i, True)
        intra(nsc - 1 - i, False)
        return carry

    lax.fori_loop(0, nsc, intra_body, 0, unroll=min(nsc, 2))

    if has_state:
        for i in range(nsc):
            for fwd in (True, False):
                si = i if fwd else nsc - 1 - i
                rows = slice(si * sc, (si + 1) * sc)
                st_ref, s0_ref, qs_scr, ku_scr, dec_scr, dst = (
                    (stf, s0f_ref, qsf, kuf, decf, og_ref) if fwd
                    else (stb, s0b_ref, qsb, kub, decb, ob))
                st = s0_ref[...] if i == 0 else st_ref[...]
                dst[rows, :] += _dot(qs_scr[rows, :], st.astype(BF16))
                if i < nsc - 1:
                    dec = jnp.broadcast_to(dec_scr[si * 8:si * 8 + 1, :], (LANES, GLA_HK)).T
                    dec = jnp.concatenate([dec] * (GLA_HV // LANES), axis=1)
                    st_ref[...] = st * dec + lax.dot_general(
                        ku_scr[rows, :], vb[rows, :], TN_DIMS, preferred_element_type=F32)
    else:
        sf_out[...] = lax.dot_general(kuf[...], vb[...], TN_DIMS, preferred_element_type=F32)
        sb_out[...] = lax.dot_general(kub[...], vb[...], TN_DIMS, preferred_element_type=F32)
    og_ref[...] = _rms(og_ref[...] + ob[...], gn_ref[...])


def _gla(proj, aproj, waf, wab, baf, bab, gn, t, s0f=None, s0b=None):
    m = proj.shape[0]
    nb = m // t
    has_state = s0f is not None
    assert t % GLA_SUPER == 0 and (has_state or t == GLA_SUPER)
    sc, c = GLA_SUPER, GLA_CHUNK
    st_shape_vmem = (GLA_HK, GLA_HV) if has_state else (8, LANES)
    per_direction = [
        pltpu.VMEM((t, GLA_HK), BF16),
        pltpu.VMEM((t, GLA_HK), BF16),
        pltpu.VMEM((8 * (t // sc), GLA_HK), F32),
        pltpu.VMEM((sc, GLA_HK), BF16),
        pltpu.VMEM((sc, GLA_HK), BF16),
        pltpu.VMEM((sc, GLA_HK), BF16),
        pltpu.VMEM((sc - 2 * c, GLA_HK), BF16),
        pltpu.VMEM((sc - 3 * c, GLA_HK), BF16),
        pltpu.VMEM((sc, sc), F32),
    ]
    st_spec = pl.BlockSpec((None, GLA_HK, GLA_HV), lambda b, h: (b * GLA_HEADS + h, 0, 0))
    in_specs = [
        pl.BlockSpec((t, GLA_HK), lambda b, h: (b, OFF_GQ // GLA_HK + h)),
        pl.BlockSpec((t, GLA_HK), lambda b, h: (b, OFF_GK // GLA_HK + h)),
        pl.BlockSpec((t, GLA_HV), lambda b, h: (b, OFF_GV // GLA_HV + h)),
        pl.BlockSpec((t, LANES), lambda b, h: (b, 0)),
        pl.BlockSpec((LANES, GLA_HK), lambda b, h: (0, h)),
        pl.BlockSpec((LANES, GLA_HK), lambda b, h: (0, h)),
        pl.BlockSpec((1, GLA_HK), lambda b, h: (0, h)),
        pl.BlockSpec((1, GLA_HK), lambda b, h: (0, h)),
        pl.BlockSpec((1, GLA_HV), lambda b, h: (0, h)),
    ]
    args = [proj, proj, proj, aproj, waf, wab, baf, bab, gn]
    og_spec = pl.BlockSpec((t, GLA_HV), lambda b, h: (b, h))
    og_shape = jax.ShapeDtypeStruct((m, GLA_DV), F32)
    if has_state:
        in_specs += [st_spec, st_spec]
        args += [s0f, s0b]
        out_specs, out_shape = og_spec, og_shape
    else:
        st_shape = jax.ShapeDtypeStruct((nb * GLA_HEADS, GLA_HK, GLA_HV), F32)
        out_specs, out_shape = [og_spec, st_spec, st_spec], [og_shape, st_shape, st_shape]
    return pl.pallas_call(
        functools.partial(_gla_kernel, t=t, has_state=has_state),
        grid=(nb, GLA_HEADS),
        in_specs=in_specs,
        out_specs=out_specs,
        out_shape=out_shape,
        scratch_shapes=[
            pltpu.VMEM(st_shape_vmem, F32),
            pltpu.VMEM(st_shape_vmem, F32),
            pltpu.VMEM((t, GLA_HV), BF16),
            pltpu.VMEM((t, GLA_HV), F32),
        ] + 2 * per_direction,
        compiler_params=_cp(("arbitrary", "arbitrary"), 56),
        name="gla_state" if has_state else "gla_zero",
    )(*args)


MERGE_SLAB = (16, 1024)


def _merge_kernel(oa_ref, og_ref, gr_ref, ga_ref, gg_ref, o_ref):
    rows, cols = o_ref.shape
    for r0 in range(0, rows, MERGE_SLAB[0]):
        for c0 in range(0, cols, MERGE_SLAB[1]):
            sl = (slice(r0, r0 + MERGE_SLAB[0]), slice(c0, c0 + MERGE_SLAB[1]))
            gr = gr_ref[sl].astype(F32)
            o_gla = og_ref[sl] * (gr * _sigmoid(gr))
            merged = (_sigmoid(ga_ref[sl].astype(F32)) * oa_ref[sl].astype(F32)
                      + _sigmoid(gg_ref[sl].astype(F32)) * o_gla)
            o_ref[sl] = merged.astype(BF16)


MERGE_TM = 256


def _merge(o_att, og, gates):
    m = o_att.shape[0]
    blk = lambda cb: pl.BlockSpec((MERGE_TM, D_MODEL), lambda i: (i, cb))
    cols = ((OFF_GR - OFF_GR) // D_MODEL, (OFF_GATT - OFF_GR) // D_MODEL, (OFF_GGLA - OFF_GR) // D_MODEL)
    return pl.pallas_call(
        _merge_kernel,
        grid=(m // MERGE_TM,),
        in_specs=[blk(0), blk(0)] + [blk(cb) for cb in cols],
        out_specs=blk(0),
        out_shape=jax.ShapeDtypeStruct((m, D_MODEL), BF16),
        compiler_params=_cp(("arbitrary",), 40),
        name="branch_merge",
    )(o_att, og, gates, gates, gates)


def _merge_side(o_att, og, gates, rows):
    m = o_att.shape[0]
    cols = ((OFF_GR - OFF_GR) // D_MODEL, (OFF_GATT - OFF_GR) // D_MODEL, (OFF_GGLA - OFF_GR) // D_MODEL)
    return _Side(_merge_kernel, (o_att, og, gates, gates, gates), (0, 0) + cols,
                 (jax.ShapeDtypeStruct((m, D_MODEL), BF16),), rows, m // rows)


OUT_TM = 512
OUT_TN = 1024


def _out_kernel(mg_ref, w_ref, x_ref, gt_ref, g2_ref, sh_ref, sc_ref, x1_ref, h2_ref, x1_scr):
    j = pl.program_id(1)
    nj = D_MODEL // OUT_TN
    slab = MXU_ACC_ROWS * 512 // OUT_TN
    for r0 in range(0, OUT_TM, slab):
        rs = slice(r0, r0 + slab)
        x1 = x_ref[rs, :] + gt_ref[...] * _dot(mg_ref[rs, :], w_ref[...])
        x1_ref[rs, :] = x1
        x1_scr[j, rs, :] = x1

    @pl.when(j == nj - 1)
    def _():
        ssq = jnp.zeros((OUT_TM, 1), F32)
        for jj in range(nj):
            xs = x1_scr[jj]
            ssq = ssq + jnp.sum(xs * xs, axis=-1, keepdims=True)
        inv = lax.rsqrt(ssq / D_MODEL + EPS)
        for jj in range(nj):
            cs = slice(jj * OUT_TN, (jj + 1) * OUT_TN)
            y = x1_scr[jj] * inv * g2_ref[:, cs]
            h2_ref[:, cs] = (y * (1.0 + sc_ref[:, cs]) + sh_ref[:, cs]).astype(BF16)


def _out_proj(merged, w_out, x, gate1, g2, shift2, scale2, mod_off, rows_per_mod):
    m = x.shape[0]
    bpb = rows_per_mod // OUT_TM
    gate_spec = pl.BlockSpec((None, 1, OUT_TN), lambda i, j: (mod_off + i // bpb, 0, j))
    return pl.pallas_call(
        _out_kernel,
        grid=(m // OUT_TM, D_MODEL // OUT_TN),
        in_specs=[
            pl.BlockSpec((OUT_TM, D_MODEL), lambda i, j: (i, 0)),
            pl.BlockSpec((D_MODEL, OUT_TN), lambda i, j: (0, j)),
            pl.BlockSpec((OUT_TM, OUT_TN), lambda i, j: (i, j)),
            gate_spec,
            pl.BlockSpec((1, D_MODEL), lambda i, j: (0, 0)),
            _mod_spec(mod_off, bpb),
            _mod_spec(mod_off, bpb),
        ],
        out_specs=[
            pl.BlockSpec((OUT_TM, OUT_TN), lambda i, j: (i, j)),
            pl.BlockSpec((OUT_TM, D_MODEL), lambda i, j: (i, 0)),
        ],
        out_shape=[
            jax.ShapeDtypeStruct((m, D_MODEL), F32),
            jax.ShapeDtypeStruct((m, D_MODEL), BF16),
        ],
        scratch_shapes=[pltpu.VMEM((D_MODEL // OUT_TN, OUT_TM, OUT_TN), F32)],
        compiler_params=_cp(("arbitrary", "arbitrary"), 56),
        name="out_proj_residual_norm",
    )(merged, w_out, x, gate1, g2, shift2, scale2)


FFI_TM = 2048
FFI_TN = 256


def _ffn_in_kernel(h_ref, wg_ref, wu_ref, o_ref):
    wg = wg_ref[...].astype(BF16)
    wu = wu_ref[...].astype(BF16)
    slab = MXU_ACC_ROWS * 512 // FFI_TN
    for r0 in range(0, FFI_TM, slab):
        rs = slice(r0, r0 + slab)
        h = h_ref[rs, :]
        g = _dot(h, wg)
        u = _dot(h, wu)
        o_ref[rs, :] = (g * _sigmoid(g) * u).astype(BF16)


def _ffn_in(h2, w):
    m = h2.shape[0]
    return pl.pallas_call(
        _ffn_in_kernel,
        grid=(m // FFI_TM, D_FF // FFI_TN),
        in_specs=[
            pl.BlockSpec((FFI_TM, D_MODEL), lambda i, j: (i, 0)),
            pl.BlockSpec((D_MODEL, FFI_TN), lambda i, j: (0, j)),
            pl.BlockSpec((D_MODEL, FFI_TN), lambda i, j: (0, D_FF // FFI_TN + j)),
        ],
        out_specs=pl.BlockSpec((FFI_TM, FFI_TN), lambda i, j: (i, j)),
        out_shape=jax.ShapeDtypeStruct((m, D_FF), BF16),
        compiler_params=_cp(("arbitrary", "arbitrary"), 58),
        name="ffn_in_swiglu",
    )(h2, w, w)


FFO_TM = 512
FFO_TN = 512


def _ffn_out_kernel(a_ref, w_ref, x_ref, gt_ref, o_ref):
    o_ref[...] = x_ref[...] + gt_ref[...] * _dot(a_ref[...], w_ref[...])


def _ffn_out(act, w, x1, gate2, mod_off, rows_per_mod, side=None):
    m = x1.shape[0]
    bpb = rows_per_mod // FFO_TM
    in_specs = [
        pl.BlockSpec((FFO_TM, D_FF), lambda j, i: (i, 0)),
        pl.BlockSpec((D_FF, FFO_TN), lambda j, i: (0, j)),
        pl.BlockSpec((FFO_TM, FFO_TN), lambda j, i: (i, j)),
        pl.BlockSpec((None, 1, FFO_TN), lambda j, i: (mod_off + i // bpb, 0, j)),
    ]
    return _host_call(
        _ffn_out_kernel, (D_MODEL // FFO_TN, m // FFO_TM), in_specs,
        pl.BlockSpec((FFO_TM, FFO_TN), lambda j, i: (i, j)),
        jax.ShapeDtypeStruct((m, D_MODEL), F32), (act, w, x1, gate2), side, 60, "ffn_out_residual")


def _rope_tables(t):
    rows = t // GRID_W
    half = HEAD_DIM // 2
    row = jnp.repeat(jnp.arange(rows, dtype=F32), GRID_W)
    col = jnp.tile(jnp.arange(GRID_W, dtype=F32), rows)
    inv = ROPE_THETA ** (-jnp.arange(0, half, 2, dtype=F32) / half)
    ar = row[:, None] * inv[None, :]
    ac = col[:, None] * inv[None, :]
    cr, sr, cc, sc = jnp.cos(ar), jnp.sin(ar), jnp.cos(ac), jnp.sin(ac)
    z = jnp.zeros_like(sr)
    tab_c = jnp.concatenate([cr, cr, cc, cc], axis=1)
    tab_sa = jnp.concatenate([-sr, z, -sc, z], axis=1)
    tab_sb = jnp.concatenate([z, sr, z, sc], axis=1)
    return tab_c, tab_sa, tab_sb


def kernel(x_prompt, x_sample, c, cache_k, cache_v, state_gla_fwd, state_gla_bwd, c_ctx, w_ada, b_ada, norm1_g, norm2_g, w_in, q_norm_g, k_norm_g, attn_sink, w_a2_fwd, b_a_fwd, w_a2_bwd, b_a_bwd, gla_norm_g, w_out, w_ffn_in, w_ffn_out):
    assert w_ada.shape[0] == 1, "single trunk layer"
    cc = jnp.zeros((MOD_ROWS, D_MODEL), F32).at[0].set(c_ctx).at[1:1 + DEC_BATCH].set(c)
    mod_all = _ada(cc, w_ada[0], b_ada[0][None, :])
    mod = tuple(mod_all[:, i * D_MODEL:(i + 1) * D_MODEL].reshape(MOD_ROWS, 1, D_MODEL)
                for i in range(N_MOD))

    r = GLA_GATE_RANK
    wa = jnp.zeros((D_MODEL, LANES), BF16).at[:, :2 * r].set(w_in[0][:, D_WIDE:].astype(BF16))
    waf = jnp.zeros((LANES, GLA_DK), BF16).at[:r].set(w_a2_fwd[0].astype(BF16))
    wab = jnp.zeros((LANES, GLA_DK), BF16).at[r:2 * r].set(w_a2_bwd[0].astype(BF16))
    g1, g2 = norm1_g[0][None, :], norm2_g[0][None, :]
    w_in_t = w_in[0].T
    qg, kg, sink = q_norm_g[0][None, :], k_norm_g[0][None, :], attn_sink[0]
    gla_w = (waf, wab, b_a_fwd[0][None, :], b_a_bwd[0][None, :], gla_norm_g[0][None, :])
    shift1, scale1, gate1, shift2, scale2, gate2 = mod
    kvw = N_KV_HEADS * HEAD_DIM
    n_wide, n_gate = OFF_GR, D_WIDE - OFF_GR
    m_ctx, m_lat = BATCH * SEQ, DEC_BATCH * DEC_SEQ
    ctx_mod, lat_mod = (0, m_ctx), (1, DEC_SEQ)

    xp = x_prompt.reshape(m_ctx, D_MODEL)
    xs = x_sample.reshape(m_lat, D_MODEL)

    h_c, a_c = _norm_mod(xp, g1, shift1, scale1, wa, *ctx_mod)
    proj_c, w_fo = _proj(h_c, w_in_t, 0, n_wide, F32, _cast_side(w_ffn_out[0], 128))
    gates_c, w_o = _proj(h_c, w_in_t, n_wide, n_gate, BF16, _cast_side(w_out[0], 64))
    oatt_c, new_k, new_v = _ctx_attn(proj_c, sink, qg, kg)
    og_c, s_f, s_b = _gla(proj_c, a_c, *gla_w, SEQ)

    h_l, a_l = _norm_mod(xs, g1, shift1, scale1, wa, *lat_mod)
    merged_c = _merge(oatt_c, og_c, gates_c)
    proj_l = _proj(h_l, w_in_t, 0, n_wide, F32)
    gates_l = _proj(h_l, w_in_t, n_wide, n_gate, BF16)
    oatt_l = _lat_attn(
        proj_l,
        cache_k[:, 0].reshape(DEC_BATCH, PAST_LEN, kvw),
        cache_v[:, 0].reshape(DEC_BATCH, PAST_LEN, kvw),
        sink, qg, kg, *_rope_tables(DEC_SEQ))
    og_l = _gla(proj_l, a_l, *gla_w, DEC_SEQ,
                state_gla_fwd[:, 0].reshape(DEC_BATCH * GLA_HEADS, GLA_HK, GLA_HV),
                state_gla_bwd[:, 0].reshape(DEC_BATCH * GLA_HEADS, GLA_HK, GLA_HV))

    x1_c, h2_c = _out_proj(merged_c, w_o, xp, gate1, g2, shift2, scale2, *ctx_mod)
    act_c = _ffn_in(h2_c, w_ffn_in[0])
    yp, merged_l = _ffn_out(act_c, w_fo, x1_c, gate2, *ctx_mod,
                            side=_merge_side(oatt_l, og_l, gates_l, 64))

    x1_l, h2_l = _out_proj(merged_l, w_o, xs, gate1, g2, shift2, scale2, *lat_mod)
    act_l = _ffn_in(h2_l, w_ffn_in[0])
    ys = _ffn_out(act_l, w_fo, x1_l, gate2, *lat_mod)
    return (
        yp.reshape(BATCH, SEQ, D_MODEL),
        ys.reshape(DEC_BATCH, DEC_SEQ, D_MODEL),
        new_k.reshape(BATCH, 1, SEQ, N_KV_HEADS, HEAD_DIM),
        new_v.reshape(BATCH, 1, SEQ, N_KV_HEADS, HEAD_DIM),
        s_f.reshape(BATCH, 1, GLA_HEADS, GLA_HK, GLA_HV),
        s_b.reshape(BATCH, 1, GLA_HEADS, GLA_HK, GLA_HV),
    )
```

```python
import functools
from typing import Callable, NamedTuple

import jax
import jax.numpy as jnp
from jax import lax
from jax.experimental import pallas as pl
from jax.experimental.pallas import tpu as pltpu

F32 = jnp.float32
BF16 = jnp.bfloat16

D_MODEL = 4096
BATCH = 32
SEQ = 256
DEC_BATCH = 8
DEC_SEQ = 1024
PAST_LEN = 256
GRID_W = 64
HEAD_DIM = 128
N_Q_HEADS = 32
N_KV_HEADS = 8
GQA_GROUP = N_Q_HEADS // N_KV_HEADS
WINDOW = 128
BLOCK = 128
ROPE_THETA = 10000.0
GLA_HEADS = 4
GLA_DK = D_MODEL // 2
GLA_DV = D_MODEL
GLA_HK = GLA_DK // GLA_HEADS
GLA_HV = GLA_DV // GLA_HEADS
GLA_GATE_RANK = 16
GLA_TAU = 16.0
GLA_CHUNK = 64
GLA_SUPER = 256
D_FF = 11008
N_MOD = 6
EPS = 1e-6

OFF_Q = 0
OFF_K = 4096
OFF_V = 5120
OFF_GQ = 6144
OFF_GK = 8192
OFF_GV = 10240
OFF_GR = 14336
OFF_GATT = 18432
OFF_GGLA = 22528
D_WIDE = 26624
LANES = 128
MXU_ACC_ROWS = 512
MOD_ROWS = 16

MIB = 1024 * 1024
VMEM_BUDGET = 60 * MIB
VMEM_SLACK = 2 * MIB
NT_DIMS = (((1,), (1,)), ((), ()))
TN_DIMS = (((0,), (0,)), ((), ()))


def _cp(sem, vmem_mib):
    return pltpu.CompilerParams(dimension_semantics=sem, vmem_limit_bytes=vmem_mib * MIB)


def _rms(x, g):
    ms = jnp.mean(x * x, axis=-1, keepdims=True)
    return x * lax.rsqrt(ms + EPS) * g


def _sigmoid(x):
    return 0.5 * jnp.tanh(0.5 * x) + 0.5


def _dot(a, b):
    return jnp.dot(a, b, preferred_element_type=F32)


ADA_TN = 512


def _ada_kernel(c_ref, w_ref, b_ref, o_ref):
    c = c_ref[...]
    s = (c * _sigmoid(c)).astype(BF16)
    o_ref[...] = _dot(s, w_ref[...].astype(BF16)) + b_ref[...]


def _ada(cc, w_ada, b_ada):
    n = w_ada.shape[1]
    return pl.pallas_call(
        _ada_kernel,
        grid=(n // ADA_TN,),
        in_specs=[
            pl.BlockSpec((MOD_ROWS, D_MODEL), lambda j: (0, 0)),
            pl.BlockSpec((D_MODEL, ADA_TN), lambda j: (0, j)),
            pl.BlockSpec((1, ADA_TN), lambda j: (0, j)),
        ],
        out_specs=pl.BlockSpec((MOD_ROWS, ADA_TN), lambda j: (0, j)),
        out_shape=jax.ShapeDtypeStruct((MOD_ROWS, n), F32),
        compiler_params=_cp(("arbitrary",), 40),
        name="ada_ln",
    )(cc, w_ada, b_ada)


NORM_TM = 512
PROJ_TM = 2048
PROJ_TN = 512


def _mod_spec(mod_off, bpb):
    return pl.BlockSpec((None, 1, D_MODEL), lambda i, *_: (mod_off + i // bpb, 0, 0))


def _norm_kernel(x_ref, g_ref, sh_ref, sc_ref, wa_ref, h_ref, a_ref):
    h = _rms(x_ref[...], g_ref[...]) * (1.0 + sc_ref[...]) + sh_ref[...]
    hb = h.astype(BF16)
    h_ref[...] = hb
    a_ref[...] = _dot(hb, wa_ref[...])


def _norm_mod(x, g1, shift, scale, wa, mod_off, rows_per_mod):
    m = x.shape[0]
    bpb = rows_per_mod // NORM_TM
    return pl.pallas_call(
        _norm_kernel,
        grid=(m // NORM_TM,),
        in_specs=[
            pl.BlockSpec((NORM_TM, D_MODEL), lambda i: (i, 0)),
            pl.BlockSpec((1, D_MODEL), lambda i: (0, 0)),
            _mod_spec(mod_off, bpb),
            _mod_spec(mod_off, bpb),
            pl.BlockSpec((D_MODEL, LANES), lambda i: (0, 0)),
        ],
        out_specs=[
            pl.BlockSpec((NORM_TM, D_MODEL), lambda i: (i, 0)),
            pl.BlockSpec((NORM_TM, LANES), lambda i: (i, 0)),
        ],
        out_shape=[
            jax.ShapeDtypeStruct((m, D_MODEL), BF16),
            jax.ShapeDtypeStruct((m, LANES), F32),
        ],
        compiler_params=_cp(("arbitrary",), 40),
        name="norm_modulate",
    )(x, g1, shift, scale, wa)


class _Side(NamedTuple):
    fn: Callable
    ins: tuple
    in_cols: tuple
    outs: tuple
    rows: int
    nblk: int

    def vmem_bytes(self):
        width = self.outs[0].shape[1]
        per_row = sum(a.dtype.itemsize for a in self.ins) + sum(o.dtype.itemsize for o in self.outs)
        return 2 * self.rows * width * per_row

    def specs(self, nj):
        def spec(shape_cols, cb):
            return pl.BlockSpec((self.rows, shape_cols),
                                lambda i, j: (jnp.minimum(i * nj + j, self.nblk - 1), cb))
        in_specs = [spec(self.outs[0].shape[1], cb) for cb in self.in_cols]
        out_specs = [spec(o.shape[1], 0) for o in self.outs]
        return in_specs, out_specs


def _host_kernel(body, n_in, n_out, side):
    def kern(*refs):
        n_sin = len(side.ins) if side else 0
        if side:
            side.fn(*refs[n_in:n_in + n_sin], *refs[n_in + n_sin + n_out:])
        body(*refs[:n_in], *refs[n_in + n_sin:n_in + n_sin + n_out])
    return kern


def _host_call(body, grid, in_specs, out_spec, out_shape, args, side, vmem_mib, name):
    if side is None:
        s_in, s_out = [], []
    else:
        assert grid[0] * grid[1] >= side.nblk
        s_in, s_out = side.specs(grid[1])
    res = pl.pallas_call(
        _host_kernel(body, len(in_specs), 1, side),
        grid=grid,
        in_specs=in_specs + s_in,
        out_specs=[out_spec] + s_out,
        out_shape=[out_shape] + (list(side.outs) if side else []),
        compiler_params=_cp(("arbitrary", "arbitrary"), vmem_mib),
        name=name,
    )(*args, *(side.ins if side else ()))
    return res[0] if side is None else res


def _cast_kernel(src_ref, dst_ref):
    dst_ref[...] = src_ref[...].astype(BF16)


def _cast_side(w, rows):
    n, d = w.shape
    return _Side(_cast_kernel, (w,), (0,), (jax.ShapeDtypeStruct((n, d), BF16),), rows, n // rows)


def _proj_kernel(h_ref, wt_ref, o_ref):
    wt = wt_ref[...].astype(BF16)
    for r0 in range(0, PROJ_TM, MXU_ACC_ROWS):
        rs = slice(r0, r0 + MXU_ACC_ROWS)
        o_ref[rs, :] = lax.dot_general(h_ref[rs, :], wt, NT_DIMS,
                                       preferred_element_type=F32).astype(o_ref.dtype)


def _proj(h, wt, col0, ncols, out_dtype, side=None):
    m = h.shape[0]
    j0 = col0 // PROJ_TN
    fixed = 2 * PROJ_TN * D_MODEL * 4 + 2 * PROJ_TM * PROJ_TN * jnp.dtype(out_dtype).itemsize
    fixed += side.vmem_bytes() if side else 0
    h_tile = PROJ_TM * D_MODEL * h.dtype.itemsize
    h_bufs = 2 if fixed + 2 * h_tile + VMEM_SLACK <= VMEM_BUDGET else 1
    in_specs = [
        pl.BlockSpec((PROJ_TM, D_MODEL), lambda i, j: (i, 0), pipeline_mode=pl.Buffered(h_bufs)),
        pl.BlockSpec((PROJ_TN, D_MODEL), lambda i, j: (j0 + j, 0)),
    ]
    return _host_call(
        _proj_kernel, (m // PROJ_TM, ncols // PROJ_TN), in_specs,
        pl.BlockSpec((PROJ_TM, PROJ_TN), lambda i, j: (i, j)),
        jax.ShapeDtypeStruct((m, ncols), out_dtype), (h, wt), side, VMEM_BUDGET // MIB, "in_proj")


def _sink_col(sink_ref, h, rows_per_head):
    rid = lax.broadcasted_iota(jnp.int32, (GQA_GROUP * rows_per_head, 1), 0) // rows_per_head
    col = jnp.full(rid.shape, sink_ref[h * GQA_GROUP], F32)
    for g in range(1, GQA_GROUP):
        col = jnp.where(rid == g, sink_ref[h * GQA_GROUP + g], col)
    return col


def _ctx_attn_kernel(sink_ref, q_ref, k_ref, v_ref, qg_ref, kg_ref, after_ref, o_ref, nk_ref, nv_ref):
    del after_ref
    scale = HEAD_DIM ** -0.5
    nv_ref[...] = v_ref[...]
    for h in range(N_KV_HEADS):
        hs = slice(h * HEAD_DIM, (h + 1) * HEAD_DIM)
        kn = _rms(k_ref[:, hs], kg_ref[...])
        nk_ref[:, hs] = kn
        kb = kn.astype(BF16)
        vb = v_ref[:, hs].astype(BF16)
        qs = []
        for g in range(GQA_GROUP):
            c0 = (h * GQA_GROUP + g) * HEAD_DIM
            qs.append(_rms(q_ref[:, c0:c0 + HEAD_DIM], qg_ref[...]).astype(BF16))
        q4 = jnp.concatenate(qs, axis=0)
        s = lax.dot_general(q4, kb, NT_DIMS, preferred_element_type=F32) * scale
        sk = _sink_col(sink_ref, h, SEQ)
        m = jnp.maximum(jnp.max(s, axis=1, keepdims=True), sk)
        p = jnp.exp(s - m)
        den = jnp.sum(p, axis=1, keepdims=True) + jnp.exp(sk - m)
        o = _dot(p.astype(BF16), vb) / den
        for g in range(GQA_GROUP):
            c0 = (h * GQA_GROUP + g) * HEAD_DIM
            o_ref[:, c0:c0 + HEAD_DIM] = o[g * SEQ:(g + 1) * SEQ, :].astype(BF16)


def _ctx_attn(proj, sink, qg, kg, after):
    m = proj.shape[0]
    kvw = N_KV_HEADS * HEAD_DIM
    return pl.pallas_call(
        _ctx_attn_kernel,
        grid=(m // SEQ,),
        in_specs=[
            pl.BlockSpec(memory_space=pltpu.SMEM),
            pl.BlockSpec((SEQ, D_MODEL), lambda b: (b, OFF_Q // D_MODEL)),
            pl.BlockSpec((SEQ, kvw), lambda b: (b, OFF_K // kvw)),
            pl.BlockSpec((SEQ, kvw), lambda b: (b, OFF_V // kvw)),
            pl.BlockSpec((1, HEAD_DIM), lambda b: (0, 0)),
            pl.BlockSpec((1, HEAD_DIM), lambda b: (0, 0)),
            pl.BlockSpec(memory_space=pl.ANY),
        ],
        out_specs=[
            pl.BlockSpec((SEQ, D_MODEL), lambda b: (b, 0)),
            pl.BlockSpec((SEQ, kvw), lambda b: (b, 0)),
            pl.BlockSpec((SEQ, kvw), lambda b: (b, 0)),
        ],
        out_shape=[
            jax.ShapeDtypeStruct((m, D_MODEL), BF16),
            jax.ShapeDtypeStruct((m, kvw), F32),
            jax.ShapeDtypeStruct((m, kvw), F32),
        ],
        compiler_params=_cp(("arbitrary",), 40),
        name="ctx_attention",
    )(sink, proj, proj, proj, qg, kg, after)


def _rope(x, c_ref, sa_ref, sb_ref):
    up = pltpu.roll(x, HEAD_DIM - 32, axis=1)
    dn = pltpu.roll(x, 32, axis=1)
    return x * c_ref[...] + up * sa_ref[...] + dn * sb_ref[...]


def _lat_attn_kernel(sink_ref, q_ref, k_ref, v_ref, ck_ref, cv_ref, qg_ref, kg_ref,
                     c_ref, sa_ref, sb_ref, o_ref, q_scr, k_scr, v_scr):
    t = DEC_SEQ
    scale = HEAD_DIM ** -0.5
    h = pl.program_id(1)
    nb = t // BLOCK
    kr = _rope(_rms(k_ref[...], kg_ref[...]), c_ref, sa_ref, sb_ref)
    k_scr[...] = kr.astype(BF16)
    v_scr[...] = v_ref[...].astype(BF16)
    for g in range(GQA_GROUP):
        qn = _rms(q_ref[:, g * HEAD_DIM:(g + 1) * HEAD_DIM], qg_ref[...])
        q_scr[g] = _rope(qn, c_ref, sa_ref, sb_ref).astype(BF16)
    ckb = ck_ref[...].astype(BF16)
    cvb = cv_ref[...].astype(BF16)
    sk = _sink_col(sink_ref, h, BLOCK)
    rows = GQA_GROUP * BLOCK
    qi = lax.broadcasted_iota(jnp.int32, (rows, BLOCK), 0) % BLOCK
    kj = lax.broadcasted_iota(jnp.int32, (rows, BLOCK), 1)
    prev_ok = kj >= qi
    next_ok = kj <= qi
    neg = jnp.finfo(F32).min

    def attend(n, has_prev, has_next):
        if isinstance(n, int):
            r0, k0 = n * BLOCK, (n - has_prev) * BLOCK
        else:
            r0 = pl.multiple_of(n * BLOCK, BLOCK)
            k0 = pl.multiple_of((n - has_prev) * BLOCK, BLOCK)
        nk = (1 + has_prev + has_next) * BLOCK
        q4 = jnp.concatenate([q_scr[g, pl.ds(r0, BLOCK), :] for g in range(GQA_GROUP)], axis=0)
        s_win = lax.dot_general(q4, k_scr[pl.ds(k0, nk), :], NT_DIMS,
                                preferred_element_type=F32) * scale
        tiles = [s_win[:, i * BLOCK:(i + 1) * BLOCK] for i in range(nk // BLOCK)]
        if has_prev:
            tiles[0] = jnp.where(prev_ok, tiles[0], neg)
        if has_next:
            tiles[-1] = jnp.where(next_ok, tiles[-1], neg)
        s_win = jnp.concatenate(tiles, axis=1)
        s_ctx = lax.dot_general(q4, ckb, NT_DIMS, preferred_element_type=F32) * scale
        m = jnp.maximum(jnp.max(s_win, axis=1, keepdims=True),
                        jnp.max(s_ctx, axis=1, keepdims=True))
        m = jnp.maximum(m, sk)
        p_win = jnp.exp(s_win - m)
        p_ctx = jnp.exp(s_ctx - m)
        den = (jnp.sum(p_win, axis=1, keepdims=True) + jnp.sum(p_ctx, axis=1, keepdims=True)
               + jnp.exp(sk - m))
        o = (_dot(p_win.astype(BF16), v_scr[pl.ds(k0, nk), :]) + _dot(p_ctx.astype(BF16), cvb)) / den
        for g in range(GQA_GROUP):
            o_ref[pl.ds(r0, BLOCK), g * HEAD_DIM:(g + 1) * HEAD_DIM] = (
                o[g * BLOCK:(g + 1) * BLOCK, :].astype(BF16))

    def interior(n, carry):
        attend(n, True, True)
        return carry

    attend(0, False, True)
    lax.fori_loop(1, nb - 1, interior, 0, unroll=3)
    attend(nb - 1, True, False)


def _lat_attn(proj, ck, cv, sink, qg, kg, rope_c, rope_sa, rope_sb):
    m = proj.shape[0]
    t = DEC_SEQ
    gw = GQA_GROUP * HEAD_DIM
    tab = pl.BlockSpec((t, HEAD_DIM), lambda b, h: (0, 0))
    vec = pl.BlockSpec((1, HEAD_DIM), lambda b, h: (0, 0))
    cache = pl.BlockSpec((None, PAST_LEN, HEAD_DIM), lambda b, h: (b, 0, h))
    return pl.pallas_call(
        _lat_attn_kernel,
        grid=(m // t, N_KV_HEADS),
        in_specs=[
            pl.BlockSpec(memory_space=pltpu.SMEM),
            pl.BlockSpec((t, gw), lambda b, h: (b, OFF_Q // gw + h)),
            pl.BlockSpec((t, HEAD_DIM), lambda b, h: (b, OFF_K // HEAD_DIM + h)),
            pl.BlockSpec((t, HEAD_DIM), lambda b, h: (b, OFF_V // HEAD_DIM + h)),
            cache, cache, vec, vec, tab, tab, tab,
        ],
        out_specs=pl.BlockSpec((t, gw), lambda b, h: (b, h)),
        out_shape=jax.ShapeDtypeStruct((m, D_MODEL), BF16),
        scratch_shapes=[
            pltpu.VMEM((GQA_GROUP, t, HEAD_DIM), BF16),
            pltpu.VMEM((t, HEAD_DIM), BF16),
            pltpu.VMEM((t, HEAD_DIM), BF16),
        ],
        compiler_params=_cp(("arbitrary", "arbitrary"), 40),
        name="latent_attention",
    )(sink, proj, proj, proj, ck, cv, qg, kg, rope_c, rope_sa, rope_sb)


def _log_sigmoid(x):
    return jnp.minimum(x, 0.0) - jnp.log(1.0 + jnp.exp(-jnp.abs(x)))


def _split2(x):
    hi = x.astype(BF16)
    lo = (x - hi.astype(F32)).astype(BF16)
    return hi, lo


def _gla_kernel(*refs, t, has_state):
    if has_state:
        (q_ref, k_ref, v_ref, a_ref, waf_ref, wab_ref, baf_ref, bab_ref, gn_ref,
         s0f_ref, s0b_ref, og_ref, *scr) = refs
    else:
        (q_ref, k_ref, v_ref, a_ref, waf_ref, wab_ref, baf_ref, bab_ref, gn_ref,
         og_ref, sf_out, sb_out, *scr) = refs
    stf, stb, vb, ob = scr[:4]
    qsf, kuf, decf, *tmpf = scr[4:13]
    qsb, kub, decb, *tmpb = scr[13:]
    c = GLA_CHUNK
    sc = GLA_SUPER
    nsub = sc // c
    nsc = t // sc
    vb[...] = v_ref[...].astype(BF16)

    row = lax.broadcasted_iota(jnp.int32, (sc, sc), 0)
    col = lax.broadcasted_iota(jnp.int32, (sc, sc), 1)
    rc, cc = row // c, col // c

    def intra(si, fwd):
        r0 = pl.multiple_of(si * sc, sc)
        w_ref, b_ref, qs_scr, ku_scr, dec_scr, tmp, dst = (
            (waf_ref, baf_ref, qsf, kuf, decf, tmpf, og_ref) if fwd
            else (wab_ref, bab_ref, qsb, kub, decb, tmpb, ob))
        qd, kd, ke, q2, q3, amat = tmp
        tri = jnp.where((col <= row) if fwd else (col >= row), 1.0, 0.0).astype(BF16)
        dist = (rc - cc) if fwd else (cc - rc)
        x = _dot(a_ref[pl.ds(r0, sc), :].astype(BF16), w_ref[...]) + b_ref[...]
        hi, lo = _split2(_log_sigmoid(x) / GLA_TAU)
        cum = _dot(tri, hi) + _dot(tri, lo)
        zero = jnp.zeros((1, GLA_HK), F32)

        def at_start(j):
            if fwd:
                return cum[j * c - 1:j * c, :] if j > 0 else zero
            return cum[(j + 1) * c:(j + 1) * c + 1, :] if j < nsub - 1 else zero

        def at_end(j):
            return cum[(j + 1) * c - 1:(j + 1) * c, :] if fwd else cum[j * c:j * c + 1, :]

        total = at_end(nsub - 1) if fwd else at_end(0)
        dec_scr[pl.ds(pl.multiple_of(si * 8, 8), 8), :] = jnp.broadcast_to(jnp.exp(total), (8, GLA_HK))
        for j in range(nsub):
            rs = slice(j * c, (j + 1) * c)
            rows = pl.ds(pl.multiple_of(r0 + j * c, c), c)
            cj = cum[rs, :]
            cs, ce = at_start(j), at_end(j)
            q = q_ref[rows, :] * (GLA_HK ** -0.5)
            k = k_ref[rows, :]
            qdj = q * jnp.exp(cj - cs)
            kej = k * jnp.exp(ce - cj)
            qd[rs, :] = qdj.astype(BF16)
            kd[rs, :] = (k * jnp.exp(cs - cj)).astype(BF16)
            ke[rs, :] = kej.astype(BF16)
            qs_scr[rows, :] = (qdj * jnp.exp(cs)).astype(BF16)
            ku_scr[rows, :] = (kej * jnp.exp(total - ce)).astype(BF16)
            p2 = j - 2 if fwd else j + 2
            if 0 <= p2 < nsub:
                l2 = (j - 2) if fwd else j
                q2[l2 * c:(l2 + 1) * c, :] = (qdj * jnp.exp(cs - at_end(p2))).astype(BF16)
            p3 = j - 3 if fwd else j + 3
            if 0 <= p3 < nsub:
                q3[...] = (qdj * jnp.exp(cs - at_end(p3))).astype(BF16)
        nt = lambda a, b: lax.dot_general(a, b, NT_DIMS, preferred_element_type=F32)
        tril = (col <= row) if fwd else (col >= row)
        amat[...] = (jnp.where((dist == 0) & tril, nt(qd[...], kd[...]), 0.0)
                     + jnp.where(dist == 1, nt(qd[...], ke[...]), 0.0))
        r2 = slice(2 * c, sc) if fwd else slice(0, 2 * c)
        amat[r2, :] += jnp.where(dist[r2, :] == 2, nt(q2[...], ke[...]), 0.0)
        r3 = slice(3 * c, sc) if fwd else slice(0, c)
        amat[r3, :] += jnp.where(dist[r3, :] == 3, nt(q3[...], ke[...]), 0.0)
        dst[pl.ds(r0, sc), :] = _dot(amat[...].astype(BF16), vb[pl.ds(r0, sc), :])

    def intra_body(i, carry):
        intra(i, True)
        intra(nsc - 1 - i, False)
        return carry

    lax.fori_loop(0, nsc, intra_body, 0, unroll=min(nsc, 2))

    if has_state:
        for i in range(nsc):
            for fwd in (True, False):
                si = i if fwd else nsc - 1 - i
                rows = slice(si * sc, (si + 1) * sc)
                st_ref, s0_ref, qs_scr, ku_scr, dec_scr, dst = (
                    (stf, s0f_ref, qsf, kuf, decf, og_ref) if fwd
                    else (stb, s0b_ref, qsb, kub, decb, ob))
                st = s0_ref[...] if i == 0 else st_ref[...]
                dst[rows, :] += _dot(qs_scr[rows, :], st.astype(BF16))
                if i < nsc - 1:
                    dec = jnp.broadcast_to(dec_scr[si * 8:si * 8 + 1, :], (LANES, GLA_HK)).T
                    dec = jnp.concatenate([dec] * (GLA_HV // LANES), axis=1)
                    st_ref[...] = st * dec + lax.dot_general(
                        ku_scr[rows, :], vb[rows, :], TN_DIMS, preferred_element_type=F32)
    else:
        sf_out[...] = lax.dot_general(kuf[...], vb[...], TN_DIMS, preferred_element_type=F32)
        sb_out[...] = lax.dot_general(kub[...], vb[...], TN_DIMS, preferred_element_type=F32)
    og_ref[...] = _rms(og_ref[...] + ob[...], gn_ref[...])


def _gla(proj, aproj, waf, wab, baf, bab, gn, t, s0f=None, s0b=None):
    m = proj.shape[0]
    nb = m // t
    has_state = s0f is not None
    assert t % GLA_SUPER == 0 and (has_state or t == GLA_SUPER)
    sc, c = GLA_SUPER, GLA_CHUNK
    st_shape_vmem = (GLA_HK, GLA_HV) if has_state else (8, LANES)
    per_direction = [
        pltpu.VMEM((t, GLA_HK), BF16),
        pltpu.VMEM((t, GLA_HK), BF16),
        pltpu.VMEM((8 * (t // sc), GLA_HK), F32),
        pltpu.VMEM((sc, GLA_HK), BF16),
        pltpu.VMEM((sc, GLA_HK), BF16),
        pltpu.VMEM((sc, GLA_HK), BF16),
        pltpu.VMEM((sc - 2 * c, GLA_HK), BF16),
        pltpu.VMEM((sc - 3 * c, GLA_HK), BF16),
        pltpu.VMEM((sc, sc), F32),
    ]
    st_spec = pl.BlockSpec((None, GLA_HK, GLA_HV), lambda b, h: (b * GLA_HEADS + h, 0, 0))
    in_specs = [
        pl.BlockSpec((t, GLA_HK), lambda b, h: (b, OFF_GQ // GLA_HK + h)),
        pl.BlockSpec((t, GLA_HK), lambda b, h: (b, OFF_GK // GLA_HK + h)),
        pl.BlockSpec((t, GLA_HV), lambda b, h: (b, OFF_GV // GLA_HV + h)),
        pl.BlockSpec((t, LANES), lambda b, h: (b, 0)),
        pl.BlockSpec((LANES, GLA_HK), lambda b, h: (0, h)),
        pl.BlockSpec((LANES, GLA_HK), lambda b, h: (0, h)),
        pl.BlockSpec((1, GLA_HK), lambda b, h: (0, h)),
        pl.BlockSpec((1, GLA_HK), lambda b, h: (0, h)),
        pl.BlockSpec((1, GLA_HV), lambda b, h: (0, h)),
    ]
    args = [proj, proj, proj, aproj, waf, wab, baf, bab, gn]
    og_spec = pl.BlockSpec((t, GLA_HV), lambda b, h: (b, h))
    og_shape = jax.ShapeDtypeStruct((m, GLA_DV), F32)
    if has_state:
        in_specs += [st_spec, st_spec]
        args += [s0f, s0b]
        out_specs, out_shape = og_spec, og_shape
    else:
        st_shape = jax.ShapeDtypeStruct((nb * GLA_HEADS, GLA_HK, GLA_HV), F32)
        out_specs, out_shape = [og_spec, st_spec, st_spec], [og_shape, st_shape, st_shape]
    return pl.pallas_call(
        functools.partial(_gla_kernel, t=t, has_state=has_state),
        grid=(nb, GLA_HEADS),
        in_specs=in_specs,
        out_specs=out_specs,
        out_shape=out_shape,
        scratch_shapes=[
            pltpu.VMEM(st_shape_vmem, F32),
            pltpu.VMEM(st_shape_vmem, F32),
            pltpu.VMEM((t, GLA_HV), BF16),
            pltpu.VMEM((t, GLA_HV), F32),
        ] + 2 * per_direction,
        compiler_params=_cp(("arbitrary", "arbitrary"), 56),
        name="gla_state" if has_state else "gla_zero",
    )(*args)


MERGE_SLAB = (16, 1024)


def _merge_kernel(oa_ref, og_ref, gr_ref, ga_ref, gg_ref, o_ref):
    rows, cols = o_ref.shape
    for r0 in range(0, rows, MERGE_SLAB[0]):
        for c0 in range(0, cols, MERGE_SLAB[1]):
            sl = (slice(r0, r0 + MERGE_SLAB[0]), slice(c0, c0 + MERGE_SLAB[1]))
            gr = gr_ref[sl].astype(F32)
            o_gla = og_ref[sl] * (gr * _sigmoid(gr))
            merged = (_sigmoid(ga_ref[sl].astype(F32)) * oa_ref[sl].astype(F32)
                      + _sigmoid(gg_ref[sl].astype(F32)) * o_gla)
            o_ref[sl] = merged.astype(BF16)


MERGE_TM = 256


def _merge(o_att, og, gates):
    m = o_att.shape[0]
    blk = lambda cb: pl.BlockSpec((MERGE_TM, D_MODEL), lambda i: (i, cb))
    cols = ((OFF_GR - OFF_GR) // D_MODEL, (OFF_GATT - OFF_GR) // D_MODEL, (OFF_GGLA - OFF_GR) // D_MODEL)
    return pl.pallas_call(
        _merge_kernel,
        grid=(m // MERGE_TM,),
        in_specs=[blk(0), blk(0)] + [blk(cb) for cb in cols],
        out_specs=blk(0),
        out_shape=jax.ShapeDtypeStruct((m, D_MODEL), BF16),
        compiler_params=_cp(("arbitrary",), 40),
        name="branch_merge",
    )(o_att, og, gates, gates, gates)


def _merge_side(o_att, og, gates, rows):
    m = o_att.shape[0]
    cols = ((OFF_GR - OFF_GR) // D_MODEL, (OFF_GATT - OFF_GR) // D_MODEL, (OFF_GGLA - OFF_GR) // D_MODEL)
    return _Side(_merge_kernel, (o_att, og, gates, gates, gates), (0, 0) + cols,
                 (jax.ShapeDtypeStruct((m, D_MODEL), BF16),), rows, m // rows)


OUT_TM = 512
OUT_TN = 1024


def _out_kernel(mg_ref, w_ref, x_ref, gt_ref, g2_ref, sh_ref, sc_ref, x1_ref, h2_ref, x1_scr):
    j = pl.program_id(1)
    nj = D_MODEL // OUT_TN
    slab = MXU_ACC_ROWS * 512 // OUT_TN
    for r0 in range(0, OUT_TM, slab):
        rs = slice(r0, r0 + slab)
        x1 = x_ref[rs, :] + gt_ref[...] * _dot(mg_ref[rs, :], w_ref[...])
        x1_ref[rs, :] = x1
        x1_scr[j, rs, :] = x1

    @pl.when(j == nj - 1)
    def _():
        ssq = jnp.zeros((OUT_TM, 1), F32)
        for jj in range(nj):
            xs = x1_scr[jj]
            ssq = ssq + jnp.sum(xs * xs, axis=-1, keepdims=True)
        inv = lax.rsqrt(ssq / D_MODEL + EPS)
        for jj in range(nj):
            cs = slice(jj * OUT_TN, (jj + 1) * OUT_TN)
            y = x1_scr[jj] * inv * g2_ref[:, cs]
            h2_ref[:, cs] = (y * (1.0 + sc_ref[:, cs]) + sh_ref[:, cs]).astype(BF16)


def _out_proj(merged, w_out, x, gate1, g2, shift2, scale2, mod_off, rows_per_mod):
    m = x.shape[0]
    bpb = rows_per_mod // OUT_TM
    gate_spec = pl.BlockSpec((None, 1, OUT_TN), lambda i, j: (mod_off + i // bpb, 0, j))
    return pl.pallas_call(
        _out_kernel,
        grid=(m // OUT_TM, D_MODEL // OUT_TN),
        in_specs=[
            pl.BlockSpec((OUT_TM, D_MODEL), lambda i, j: (i, 0)),
            pl.BlockSpec((D_MODEL, OUT_TN), lambda i, j: (0, j)),
            pl.BlockSpec((OUT_TM, OUT_TN), lambda i, j: (i, j)),
            gate_spec,
            pl.BlockSpec((1, D_MODEL), lambda i, j: (0, 0)),
            _mod_spec(mod_off, bpb),
            _mod_spec(mod_off, bpb),
        ],
        out_specs=[
            pl.BlockSpec((OUT_TM, OUT_TN), lambda i, j: (i, j)),
            pl.BlockSpec((OUT_TM, D_MODEL), lambda i, j: (i, 0)),
        ],
        out_shape=[
            jax.ShapeDtypeStruct((m, D_MODEL), F32),
            jax.ShapeDtypeStruct((m, D_MODEL), BF16),
        ],
        scratch_shapes=[pltpu.VMEM((D_MODEL // OUT_TN, OUT_TM, OUT_TN), F32)],
        compiler_params=_cp(("arbitrary", "arbitrary"), 56),
        name="out_proj_residual_norm",
    )(merged, w_out, x, gate1, g2, shift2, scale2)


FFI_TM = 2048
FFI_TN = 256


def _ffn_in_kernel(h_ref, wg_ref, wu_ref, o_ref):
    wg = wg_ref[...].astype(BF16)
    wu = wu_ref[...].astype(BF16)
    slab = MXU_ACC_ROWS * 512 // FFI_TN
    for r0 in range(0, FFI_TM, slab):
        rs = slice(r0, r0 + slab)
        h = h_ref[rs, :]
        g = _dot(h, wg)
        u = _dot(h, wu)
        o_ref[rs, :] = (g * _sigmoid(g) * u).astype(BF16)


def _ffn_in(h2, w):
    m = h2.shape[0]
    return pl.pallas_call(
        _ffn_in_kernel,
        grid=(m // FFI_TM, D_FF // FFI_TN),
        in_specs=[
            pl.BlockSpec((FFI_TM, D_MODEL), lambda i, j: (i, 0)),
            pl.BlockSpec((D_MODEL, FFI_TN), lambda i, j: (0, j)),
            pl.BlockSpec((D_MODEL, FFI_TN), lambda i, j: (0, D_FF // FFI_TN + j)),
        ],
        out_specs=pl.BlockSpec((FFI_TM, FFI_TN), lambda i, j: (i, j)),
        out_shape=jax.ShapeDtypeStruct((m, D_FF), BF16),
        compiler_params=_cp(("arbitrary", "arbitrary"), 58),
        name="ffn_in_swiglu",
    )(h2, w, w)


FFO_TM = 512
FFO_TN = 512


def _ffn_out_kernel(a_ref, w_ref, x_ref, gt_ref, o_ref):
    o_ref[...] = x_ref[...] + gt_ref[...] * _dot(a_ref[...], w_ref[...])


def _ffn_out(act, w, x1, gate2, mod_off, rows_per_mod, side=None):
    m = x1.shape[0]
    bpb = rows_per_mod // FFO_TM
    in_specs = [
        pl.BlockSpec((FFO_TM, D_FF), lambda j, i: (i, 0)),
        pl.BlockSpec((D_FF, FFO_TN), lambda j, i: (0, j)),
        pl.BlockSpec((FFO_TM, FFO_TN), lambda j, i: (i, j)),
        pl.BlockSpec((None, 1, FFO_TN), lambda j, i: (mod_off + i // bpb, 0, j)),
    ]
    return _host_call(
        _ffn_out_kernel, (D_MODEL // FFO_TN, m // FFO_TM), in_specs,
        pl.BlockSpec((FFO_TM, FFO_TN), lambda j, i: (i, j)),
        jax.ShapeDtypeStruct((m, D_MODEL), F32), (act, w, x1, gate2), side, 60, "ffn_out_residual")


def _rope_tables(t):
    rows = t // GRID_W
    half = HEAD_DIM // 2
    row = jnp.repeat(jnp.arange(rows, dtype=F32), GRID_W)
    col = jnp.tile(jnp.arange(GRID_W, dtype=F32), rows)
    inv = ROPE_THETA ** (-jnp.arange(0, half, 2, dtype=F32) / half)
    ar = row[:, None] * inv[None, :]
    ac = col[:, None] * inv[None, :]
    cr, sr, cc, sc = jnp.cos(ar), jnp.sin(ar), jnp.cos(ac), jnp.sin(ac)
    z = jnp.zeros_like(sr)
    tab_c = jnp.concatenate([cr, cr, cc, cc], axis=1)
    tab_sa = jnp.concatenate([-sr, z, -sc, z], axis=1)
    tab_sb = jnp.concatenate([z, sr, z, sc], axis=1)
    return tab_c, tab_sa, tab_sb


def kernel(x_prompt, x_sample, c, cache_k, cache_v, state_gla_fwd, state_gla_bwd, c_ctx, w_ada, b_ada, norm1_g, norm2_g, w_in, q_norm_g, k_norm_g, attn_sink, w_a2_fwd, b_a_fwd, w_a2_bwd, b_a_bwd, gla_norm_g, w_out, w_ffn_in, w_ffn_out):
    assert w_ada.shape[0] == 1, "single trunk layer"
    cc = jnp.zeros((MOD_ROWS, D_MODEL), F32).at[0].set(c_ctx).at[1:1 + DEC_BATCH].set(c)
    mod_all = _ada(cc, w_ada[0], b_ada[0][None, :])
    mod = tuple(mod_all[:, i * D_MODEL:(i + 1) * D_MODEL].reshape(MOD_ROWS, 1, D_MODEL)
                for i in range(N_MOD))

    r = GLA_GATE_RANK
    wa = jnp.zeros((D_MODEL, LANES), BF16).at[:, :2 * r].set(w_in[0][:, D_WIDE:].astype(BF16))
    waf = jnp.zeros((LANES, GLA_DK), BF16).at[:r].set(w_a2_fwd[0].astype(BF16))
    wab = jnp.zeros((LANES, GLA_DK), BF16).at[r:2 * r].set(w_a2_bwd[0].astype(BF16))
    g1, g2 = norm1_g[0][None, :], norm2_g[0][None, :]
    w_in_t = w_in[0].T
    qg, kg, sink = q_norm_g[0][None, :], k_norm_g[0][None, :], attn_sink[0]
    gla_w = (waf, wab, b_a_fwd[0][None, :], b_a_bwd[0][None, :], gla_norm_g[0][None, :])
    shift1, scale1, gate1, shift2, scale2, gate2 = mod
    kvw = N_KV_HEADS * HEAD_DIM
    n_wide, n_gate = OFF_GR, D_WIDE - OFF_GR
    m_ctx, m_lat = BATCH * SEQ, DEC_BATCH * DEC_SEQ
    ctx_mod, lat_mod = (0, m_ctx), (1, DEC_SEQ)

    xp = x_prompt.reshape(m_ctx, D_MODEL)
    xs = x_sample.reshape(m_lat, D_MODEL)

    h_c, a_c = _norm_mod(xp, g1, shift1, scale1, wa, *ctx_mod)
    proj_c, w_fo = _proj(h_c, w_in_t, 0, n_wide, F32, _cast_side(w_ffn_out[0], 128))
    gates_c, w_o = _proj(h_c, w_in_t, n_wide, n_gate, BF16, _cast_side(w_out[0], 64))
    h_l, a_l = _norm_mod(xs, g1, shift1, scale1, wa, *lat_mod)
    oatt_c, new_k, new_v = _ctx_attn(proj_c, sink, qg, kg, after=h_l)
    og_c, s_f, s_b = _gla(proj_c, a_c, *gla_w, SEQ)

    merged_c = _merge(oatt_c, og_c, gates_c)
    proj_l = _proj(h_l, w_in_t, 0, n_wide, F32)
    gates_l = _proj(h_l, w_in_t, n_wide, n_gate, BF16)
    oatt_l = _lat_attn(
        proj_l,
        cache_k[:, 0].reshape(DEC_BATCH, PAST_LEN, kvw),
        cache_v[:, 0].reshape(DEC_BATCH, PAST_LEN, kvw),
        sink, qg, kg, *_rope_tables(DEC_SEQ))
    og_l = _gla(proj_l, a_l, *gla_w, DEC_SEQ,
                state_gla_fwd[:, 0].reshape(DEC_BATCH * GLA_HEADS, GLA_HK, GLA_HV),
                state_gla_bwd[:, 0].reshape(DEC_BATCH * GLA_HEADS, GLA_HK, GLA_HV))

    x1_c, h2_c = _out_proj(merged_c, w_o, xp, gate1, g2, shift2, scale2, *ctx_mod)
    act_c = _ffn_in(h2_c, w_ffn_in[0])
    yp, merged_l = _ffn_out(act_c, w_fo, x1_c, gate2, *ctx_mod,
                            side=_merge_side(oatt_l, og_l, gates_l, 64))

    x1_l, h2_l = _out_proj(merged_l, w_o, xs, gate1, g2, shift2, scale2, *lat_mod)
    act_l = _ffn_in(h2_l, w_ffn_in[0])
    ys = _ffn_out(act_l, w_fo, x1_l, gate2, *lat_mod)
    return (
        yp.reshape(BATCH, SEQ, D_MODEL),
        ys.reshape(DEC_BATCH, DEC_SEQ, D_MODEL),
        new_k.reshape(BATCH, 1, SEQ, N_KV_HEADS, HEAD_DIM),
        new_v.reshape(BATCH, 1, SEQ, N_KV_HEADS, HEAD_DIM),
        s_f.reshape(BATCH, 1, GLA_HEADS, GLA_HK, GLA_HV),
        s_b.reshape(BATCH, 1, GLA_HEADS, GLA_HK, GLA_HV),
    )
```

```python
import functools
from typing import Callable, NamedTuple

import jax
import jax.numpy as jnp
from jax import lax
from jax.experimental import pallas as pl
from jax.experimental.pallas import tpu as pltpu

F32 = jnp.float32
BF16 = jnp.bfloat16

D_MODEL = 4096
BATCH = 32
SEQ = 256
DEC_BATCH = 8
DEC_SEQ = 1024
PAST_LEN = 256
GRID_W = 64
HEAD_DIM = 128
N_Q_HEADS = 32
N_KV_HEADS = 8
GQA_GROUP = N_Q_HEADS // N_KV_HEADS
WINDOW = 128
BLOCK = 128
ROPE_THETA = 10000.0
GLA_HEADS = 4
GLA_DK = D_MODEL // 2
GLA_DV = D_MODEL
GLA_HK = GLA_DK // GLA_HEADS
GLA_HV = GLA_DV // GLA_HEADS
GLA_GATE_RANK = 16
GLA_TAU = 16.0
GLA_CHUNK = 64
GLA_SUPER = 256
D_FF = 11008
N_MOD = 6
EPS = 1e-6

OFF_Q = 0
OFF_K = 4096
OFF_V = 5120
OFF_GQ = 6144
OFF_GK = 8192
OFF_GV = 10240
OFF_GR = 14336
OFF_GATT = 18432
OFF_GGLA = 22528
D_WIDE = 26624
LANES = 128
MXU_ACC_ROWS = 512
MOD_ROWS = 16

MIB = 1024 * 1024
VMEM_BUDGET = 60 * MIB
VMEM_SLACK = 2 * MIB
NT_DIMS = (((1,), (1,)), ((), ()))
TN_DIMS = (((0,), (0,)), ((), ()))


def _cp(sem, vmem_mib):
    return pltpu.CompilerParams(dimension_semantics=sem, vmem_limit_bytes=vmem_mib * MIB)


def _rms(x, g):
    ms = jnp.mean(x * x, axis=-1, keepdims=True)
    return x * lax.rsqrt(ms + EPS) * g


def _sigmoid(x):
    return 0.5 * jnp.tanh(0.5 * x) + 0.5


def _dot(a, b):
    return jnp.dot(a, b, preferred_element_type=F32)


ADA_TN = 512


def _ada_kernel(c_ref, w_ref, b_ref, o_ref):
    c = c_ref[...]
    s = (c * _sigmoid(c)).astype(BF16)
    o_ref[...] = _dot(s, w_ref[...].astype(BF16)) + b_ref[...]


def _ada(cc, w_ada, b_ada):
    n = w_ada.shape[1]
    return pl.pallas_call(
        _ada_kernel,
        grid=(n // ADA_TN,),
        in_specs=[
            pl.BlockSpec((MOD_ROWS, D_MODEL), lambda j: (0, 0)),
            pl.BlockSpec((D_MODEL, ADA_TN), lambda j: (0, j)),
            pl.BlockSpec((1, ADA_TN), lambda j: (0, j)),
        ],
        out_specs=pl.BlockSpec((MOD_ROWS, ADA_TN), lambda j: (0, j)),
        out_shape=jax.ShapeDtypeStruct((MOD_ROWS, n), F32),
        compiler_params=_cp(("arbitrary",), 40),
        name="ada_ln",
    )(cc, w_ada, b_ada)


NORM_TM = 512
PROJ_TM = 2048
PROJ_TN = 512


def _mod_spec(mod_off, bpb):
    return pl.BlockSpec((None, 1, D_MODEL), lambda i, *_: (mod_off + i // bpb, 0, 0))


def _norm_kernel(x_ref, g_ref, sh_ref, sc_ref, wa_ref, h_ref, a_ref):
    h = _rms(x_ref[...], g_ref[...]) * (1.0 + sc_ref[...]) + sh_ref[...]
    hb = h.astype(BF16)
    h_ref[...] = hb
    a_ref[...] = _dot(hb, wa_ref[...])


def _norm_mod(x, g1, shift, scale, wa, mod_off, rows_per_mod):
    m = x.shape[0]
    bpb = rows_per_mod // NORM_TM
    return pl.pallas_call(
        _norm_kernel,
        grid=(m // NORM_TM,),
        in_specs=[
            pl.BlockSpec((NORM_TM, D_MODEL), lambda i: (i, 0)),
            pl.BlockSpec((1, D_MODEL), lambda i: (0, 0)),
            _mod_spec(mod_off, bpb),
            _mod_spec(mod_off, bpb),
            pl.BlockSpec((D_MODEL, LANES), lambda i: (0, 0)),
        ],
        out_specs=[
            pl.BlockSpec((NORM_TM, D_MODEL), lambda i: (i, 0)),
            pl.BlockSpec((NORM_TM, LANES), lambda i: (i, 0)),
        ],
        out_shape=[
            jax.ShapeDtypeStruct((m, D_MODEL), BF16),
            jax.ShapeDtypeStruct((m, LANES), F32),
        ],
        compiler_params=_cp(("arbitrary",), 40),
        name="norm_modulate",
    )(x, g1, shift, scale, wa)


class _Side(NamedTuple):
    fn: Callable
    ins: tuple
    in_cols: tuple
    outs: tuple
    rows: int
    nblk: int

    def vmem_bytes(self):
        width = self.outs[0].shape[1]
        per_row = sum(a.dtype.itemsize for a in self.ins) + sum(o.dtype.itemsize for o in self.outs)
        return 2 * self.rows * width * per_row

    def specs(self, nj):
        def spec(shape_cols, cb):
            return pl.BlockSpec((self.rows, shape_cols),
                                lambda i, j: (jnp.minimum(i * nj + j, self.nblk - 1), cb))
        in_specs = [spec(self.outs[0].shape[1], cb) for cb in self.in_cols]
        out_specs = [spec(o.shape[1], 0) for o in self.outs]
        return in_specs, out_specs


def _host_kernel(body, n_in, n_out, side):
    def kern(*refs):
        n_sin = len(side.ins) if side else 0
        if side:
            side.fn(*refs[n_in:n_in + n_sin], *refs[n_in + n_sin + n_out:])
        body(*refs[:n_in], *refs[n_in + n_sin:n_in + n_sin + n_out])
    return kern


def _host_call(body, grid, in_specs, out_spec, out_shape, args, side, vmem_mib, name):
    if side is None:
        s_in, s_out = [], []
    else:
        assert grid[0] * grid[1] >= side.nblk
        s_in, s_out = side.specs(grid[1])
    res = pl.pallas_call(
        _host_kernel(body, len(in_specs), 1, side),
        grid=grid,
        in_specs=in_specs + s_in,
        out_specs=[out_spec] + s_out,
        out_shape=[out_shape] + (list(side.outs) if side else []),
        compiler_params=_cp(("arbitrary", "arbitrary"), vmem_mib),
        name=name,
    )(*args, *(side.ins if side else ()))
    return res[0] if side is None else res


def _cast_kernel(src_ref, dst_ref):
    dst_ref[...] = src_ref[...].astype(BF16)


def _cast_side(w, rows):
    n, d = w.shape
    return _Side(_cast_kernel, (w,), (0,), (jax.ShapeDtypeStruct((n, d), BF16),), rows, n // rows)


def _proj_kernel(h_ref, wt_ref, o_ref):
    wt = wt_ref[...].astype(BF16)
    for r0 in range(0, PROJ_TM, MXU_ACC_ROWS):
        rs = slice(r0, r0 + MXU_ACC_ROWS)
        o_ref[rs, :] = lax.dot_general(h_ref[rs, :], wt, NT_DIMS,
                                       preferred_element_type=F32).astype(o_ref.dtype)


def _proj(h, wt, col0, ncols, out_dtype, side=None):
    m = h.shape[0]
    j0 = col0 // PROJ_TN
    fixed = 2 * PROJ_TN * D_MODEL * 4 + 2 * PROJ_TM * PROJ_TN * jnp.dtype(out_dtype).itemsize
    fixed += side.vmem_bytes() if side else 0
    h_tile = PROJ_TM * D_MODEL * h.dtype.itemsize
    h_bufs = 2 if fixed + 2 * h_tile + VMEM_SLACK <= VMEM_BUDGET else 1
    in_specs = [
        pl.BlockSpec((PROJ_TM, D_MODEL), lambda i, j: (i, 0), pipeline_mode=pl.Buffered(h_bufs)),
        pl.BlockSpec((PROJ_TN, D_MODEL), lambda i, j: (j0 + j, 0)),
    ]
    return _host_call(
        _proj_kernel, (m // PROJ_TM, ncols // PROJ_TN), in_specs,
        pl.BlockSpec((PROJ_TM, PROJ_TN), lambda i, j: (i, j)),
        jax.ShapeDtypeStruct((m, ncols), out_dtype), (h, wt), side, VMEM_BUDGET // MIB, "in_proj")


def _sink_col(sink_ref, h, rows_per_head):
    rid = lax.broadcasted_iota(jnp.int32, (GQA_GROUP * rows_per_head, 1), 0) // rows_per_head
    col = jnp.full(rid.shape, sink_ref[h * GQA_GROUP], F32)
    for g in range(1, GQA_GROUP):
        col = jnp.where(rid == g, sink_ref[h * GQA_GROUP + g], col)
    return col


def _ctx_attn_kernel(sink_ref, q_ref, k_ref, v_ref, qg_ref, kg_ref, after_ref, o_ref, nk_ref, nv_ref):
    del after_ref
    scale = HEAD_DIM ** -0.5
    nv_ref[...] = v_ref[...]
    for h in range(N_KV_HEADS):
        hs = slice(h * HEAD_DIM, (h + 1) * HEAD_DIM)
        kn = _rms(k_ref[:, hs], kg_ref[...])
        nk_ref[:, hs] = kn
        kb = kn.astype(BF16)
        vb = jnp.concatenate([v_ref[:, hs].astype(BF16), jnp.ones((SEQ, HEAD_DIM), BF16)], axis=1)
        qs = []
        for g in range(GQA_GROUP):
            c0 = (h * GQA_GROUP + g) * HEAD_DIM
            qs.append(_rms(q_ref[:, c0:c0 + HEAD_DIM], qg_ref[...]).astype(BF16))
        q4 = jnp.concatenate(qs, axis=0)
        s = lax.dot_general(q4, kb, NT_DIMS, preferred_element_type=F32) * scale
        sk = _sink_col(sink_ref, h, SEQ)
        m = jnp.maximum(jnp.max(s, axis=1, keepdims=True), sk)
        p = jnp.exp(s - m)
        ov = _dot(p.astype(BF16), vb)
        den = ov[:, HEAD_DIM:HEAD_DIM + 1] + jnp.exp(sk - m)
        o = ov[:, :HEAD_DIM] / den
        for g in range(GQA_GROUP):
            c0 = (h * GQA_GROUP + g) * HEAD_DIM
            o_ref[:, c0:c0 + HEAD_DIM] = o[g * SEQ:(g + 1) * SEQ, :].astype(BF16)


def _ctx_attn(proj, sink, qg, kg, after):
    m = proj.shape[0]
    kvw = N_KV_HEADS * HEAD_DIM
    return pl.pallas_call(
        _ctx_attn_kernel,
        grid=(m // SEQ,),
        in_specs=[
            pl.BlockSpec(memory_space=pltpu.SMEM),
            pl.BlockSpec((SEQ, D_MODEL), lambda b: (b, OFF_Q // D_MODEL)),
            pl.BlockSpec((SEQ, kvw), lambda b: (b, OFF_K // kvw)),
            pl.BlockSpec((SEQ, kvw), lambda b: (b, OFF_V // kvw)),
            pl.BlockSpec((1, HEAD_DIM), lambda b: (0, 0)),
            pl.BlockSpec((1, HEAD_DIM), lambda b: (0, 0)),
            pl.BlockSpec(memory_space=pl.ANY),
        ],
        out_specs=[
            pl.BlockSpec((SEQ, D_MODEL), lambda b: (b, 0)),
            pl.BlockSpec((SEQ, kvw), lambda b: (b, 0)),
            pl.BlockSpec((SEQ, kvw), lambda b: (b, 0)),
        ],
        out_shape=[
            jax.ShapeDtypeStruct((m, D_MODEL), BF16),
            jax.ShapeDtypeStruct((m, kvw), F32),
            jax.ShapeDtypeStruct((m, kvw), F32),
        ],
        compiler_params=_cp(("arbitrary",), 40),
        name="ctx_attention",
    )(sink, proj, proj, proj, qg, kg, after)


def _rope(x, c_ref, sa_ref, sb_ref):
    up = pltpu.roll(x, HEAD_DIM - 32, axis=1)
    dn = pltpu.roll(x, 32, axis=1)
    return x * c_ref[...] + up * sa_ref[...] + dn * sb_ref[...]


def _lat_attn_kernel(sink_ref, q_ref, k_ref, v_ref, ck_ref, cv_ref, qg_ref, kg_ref,
                     c_ref, sa_ref, sb_ref, o_ref, q_scr, k_scr, v_scr):
    t = DEC_SEQ
    scale = HEAD_DIM ** -0.5
    h = pl.program_id(1)
    nb = t // BLOCK
    kr = _rope(_rms(k_ref[...], kg_ref[...]), c_ref, sa_ref, sb_ref)
    k_scr[...] = kr.astype(BF16)
    ones = jnp.ones((t, HEAD_DIM), BF16)
    v_scr[...] = jnp.concatenate([v_ref[...].astype(BF16), ones], axis=1)
    for g in range(GQA_GROUP):
        qn = _rms(q_ref[:, g * HEAD_DIM:(g + 1) * HEAD_DIM], qg_ref[...])
        q_scr[g] = _rope(qn, c_ref, sa_ref, sb_ref).astype(BF16)
    ckb = ck_ref[...].astype(BF16)
    cvb = jnp.concatenate([cv_ref[...].astype(BF16), ones[:PAST_LEN, :]], axis=1)
    sk = _sink_col(sink_ref, h, BLOCK)
    rows = GQA_GROUP * BLOCK
    qi = lax.broadcasted_iota(jnp.int32, (rows, BLOCK), 0) % BLOCK
    kj = lax.broadcasted_iota(jnp.int32, (rows, BLOCK), 1)
    prev_ok = kj >= qi
    next_ok = kj <= qi
    neg = jnp.finfo(F32).min

    def attend(n, has_prev, has_next):
        if isinstance(n, int):
            r0, k0 = n * BLOCK, (n - has_prev) * BLOCK
        else:
            r0 = pl.multiple_of(n * BLOCK, BLOCK)
            k0 = pl.multiple_of((n - has_prev) * BLOCK, BLOCK)
        nk = (1 + has_prev + has_next) * BLOCK
        q4 = jnp.concatenate([q_scr[g, pl.ds(r0, BLOCK), :] for g in range(GQA_GROUP)], axis=0)
        s_win = lax.dot_general(q4, k_scr[pl.ds(k0, nk), :], NT_DIMS,
                                preferred_element_type=F32) * scale
        tiles = [s_win[:, i * BLOCK:(i + 1) * BLOCK] for i in range(nk // BLOCK)]
        if has_prev:
            tiles[0] = jnp.where(prev_ok, tiles[0], neg)
        if has_next:
            tiles[-1] = jnp.where(next_ok, tiles[-1], neg)
        s_win = jnp.concatenate(tiles, axis=1)
        s_ctx = lax.dot_general(q4, ckb, NT_DIMS, preferred_element_type=F32) * scale
        m = jnp.maximum(jnp.max(s_win, axis=1, keepdims=True),
                        jnp.max(s_ctx, axis=1, keepdims=True))
        m = jnp.maximum(m, sk)
        p_win = jnp.exp(s_win - m)
        p_ctx = jnp.exp(s_ctx - m)
        ov = _dot(p_win.astype(BF16), v_scr[pl.ds(k0, nk), :]) + _dot(p_ctx.astype(BF16), cvb)
        den = ov[:, HEAD_DIM:HEAD_DIM + 1] + jnp.exp(sk - m)
        o = ov[:, :HEAD_DIM] / den
        for g in range(GQA_GROUP):
            o_ref[pl.ds(r0, BLOCK), g * HEAD_DIM:(g + 1) * HEAD_DIM] = (
                o[g * BLOCK:(g + 1) * BLOCK, :].astype(BF16))

    def interior(n, carry):
        attend(n, True, True)
        return carry

    attend(0, False, True)
    lax.fori_loop(1, nb - 1, interior, 0, unroll=3)
    attend(nb - 1, True, False)


def _lat_attn(proj, ck, cv, sink, qg, kg, rope_c, rope_sa, rope_sb):
    m = proj.shape[0]
    t = DEC_SEQ
    gw = GQA_GROUP * HEAD_DIM
    tab = pl.BlockSpec((t, HEAD_DIM), lambda b, h: (0, 0))
    vec = pl.BlockSpec((1, HEAD_DIM), lambda b, h: (0, 0))
    cache = pl.BlockSpec((None, PAST_LEN, HEAD_DIM), lambda b, h: (b, 0, h))
    return pl.pallas_call(
        _lat_attn_kernel,
        grid=(m // t, N_KV_HEADS),
        in_specs=[
            pl.BlockSpec(memory_space=pltpu.SMEM),
            pl.BlockSpec((t, gw), lambda b, h: (b, OFF_Q // gw + h)),
            pl.BlockSpec((t, HEAD_DIM), lambda b, h: (b, OFF_K // HEAD_DIM + h)),
            pl.BlockSpec((t, HEAD_DIM), lambda b, h: (b, OFF_V // HEAD_DIM + h)),
            cache, cache, vec, vec, tab, tab, tab,
        ],
        out_specs=pl.BlockSpec((t, gw), lambda b, h: (b, h)),
        out_shape=jax.ShapeDtypeStruct((m, D_MODEL), BF16),
        scratch_shapes=[
            pltpu.VMEM((GQA_GROUP, t, HEAD_DIM), BF16),
            pltpu.VMEM((t, HEAD_DIM), BF16),
            pltpu.VMEM((t, 2 * HEAD_DIM), BF16),
        ],
        compiler_params=_cp(("arbitrary", "arbitrary"), 40),
        name="latent_attention",
    )(sink, proj, proj, proj, ck, cv, qg, kg, rope_c, rope_sa, rope_sb)


def _log_sigmoid(x):
    return jnp.minimum(x, 0.0) - jnp.log(1.0 + jnp.exp(-jnp.abs(x)))


def _split2(x):
    hi = x.astype(BF16)
    lo = (x - hi.astype(F32)).astype(BF16)
    return hi, lo


def _gla_kernel(*refs, t, has_state):
    if has_state:
        (q_ref, k_ref, v_ref, a_ref, waf_ref, wab_ref, baf_ref, bab_ref, gn_ref,
         s0f_ref, s0b_ref, og_ref, *scr) = refs
    else:
        (q_ref, k_ref, v_ref, a_ref, waf_ref, wab_ref, baf_ref, bab_ref, gn_ref,
         og_ref, sf_out, sb_out, *scr) = refs
    stf, stb, vb, ob = scr[:4]
    qsf, kuf, decf, *tmpf = scr[4:13]
    qsb, kub, decb, *tmpb = scr[13:]
    c = GLA_CHUNK
    sc = GLA_SUPER
    nsub = sc // c
    nsc = t // sc
    vb[...] = v_ref[...].astype(BF16)

    row = lax.broadcasted_iota(jnp.int32, (sc, sc), 0)
    col = lax.broadcasted_iota(jnp.int32, (sc, sc), 1)
    rc, cc = row // c, col // c

    def intra(si, fwd):
        r0 = pl.multiple_of(si * sc, sc)
        w_ref, b_ref, qs_scr, ku_scr, dec_scr, tmp, dst = (
            (waf_ref, baf_ref, qsf, kuf, decf, tmpf, og_ref) if fwd
            else (wab_ref, bab_ref, qsb, kub, decb, tmpb, ob))
        qd, kd, ke, q2, q3, amat = tmp
        tri = jnp.where((col <= row) if fwd else (col >= row), 1.0, 0.0).astype(BF16)
        dist = (rc - cc) if fwd else (cc - rc)
        x = _dot(a_ref[pl.ds(r0, sc), :].astype(BF16), w_ref[...]) + b_ref[...]
        hi, lo = _split2(_log_sigmoid(x) / GLA_TAU)
        cum = _dot(tri, hi) + _dot(tri, lo)
        zero = jnp.zeros((1, GLA_HK), F32)

        def at_start(j):
            if fwd:
                return cum[j * c - 1:j * c, :] if j > 0 else zero
            return cum[(j + 1) * c:(j + 1) * c + 1, :] if j < nsub - 1 else zero

        def at_end(j):
            return cum[(j + 1) * c - 1:(j + 1) * c, :] if fwd else cum[j * c:j * c + 1, :]

        total = at_end(nsub - 1) if fwd else at_end(0)
        dec_scr[pl.ds(pl.multiple_of(si * 8, 8), 8), :] = jnp.broadcast_to(jnp.exp(total), (8, GLA_HK))
        for j in range(nsub):
            rs = slice(j * c, (j + 1) * c)
            rows = pl.ds(pl.multiple_of(r0 + j * c, c), c)
            cj = cum[rs, :]
            cs, ce = at_start(j), at_end(j)
            q = q_ref[rows, :] * (GLA_HK ** -0.5)
            k = k_ref[rows, :]
            qdj = q * jnp.exp(cj - cs)
            kej = k * jnp.exp(ce - cj)
            qd[rs, :] = qdj.astype(BF16)
            kd[rs, :] = (k * jnp.exp(cs - cj)).astype(BF16)
            ke[rs, :] = kej.astype(BF16)
            qs_scr[rows, :] = (qdj * jnp.exp(cs)).astype(BF16)
            ku_scr[rows, :] = (kej * jnp.exp(total - ce)).astype(BF16)
            p2 = j - 2 if fwd else j + 2
            if 0 <= p2 < nsub:
                l2 = (j - 2) if fwd else j
                q2[l2 * c:(l2 + 1) * c, :] = (qdj * jnp.exp(cs - at_end(p2))).astype(BF16)
            p3 = j - 3 if fwd else j + 3
            if 0 <= p3 < nsub:
                q3[...] = (qdj * jnp.exp(cs - at_end(p3))).astype(BF16)
        nt = lambda a, b: lax.dot_general(a, b, NT_DIMS, preferred_element_type=F32)
        tril = (col <= row) if fwd else (col >= row)
        amat[...] = (jnp.where((dist == 0) & tril, nt(qd[...], kd[...]), 0.0)
                     + jnp.where(dist == 1, nt(qd[...], ke[...]), 0.0))
        r2 = slice(2 * c, sc) if fwd else slice(0, 2 * c)
        amat[r2, :] += jnp.where(dist[r2, :] == 2, nt(q2[...], ke[...]), 0.0)
        r3 = slice(3 * c, sc) if fwd else slice(0, c)
        amat[r3, :] += jnp.where(dist[r3, :] == 3, nt(q3[...], ke[...]), 0.0)
        dst[pl.ds(r0, sc), :] = _dot(amat[...].astype(BF16), vb[pl.ds(r0, sc), :])

    def intra_body(i, carry):
        intra(i, True)
        intra(nsc - 1 - i, False)
        return carry

    lax.fori_loop(0, nsc, intra_body, 0, unroll=min(nsc, 2))

    if has_state:
        for i in range(nsc):
            for fwd in (True, False):
                si = i if fwd else nsc - 1 - i
                rows = slice(si * sc, (si + 1) * sc)
                st_ref, s0_ref, qs_scr, ku_scr, dec_scr, dst = (
                    (stf, s0f_ref, qsf, kuf, decf, og_ref) if fwd
                    else (stb, s0b_ref, qsb, kub, decb, ob))
                st = s0_ref[...] if i == 0 else st_ref[...]
                dst[rows, :] += _dot(qs_scr[rows, :], st.astype(BF16))
                if i < nsc - 1:
                    dec = jnp.broadcast_to(dec_scr[si * 8:si * 8 + 1, :], (LANES, GLA_HK)).T
                    dec = jnp.concatenate([dec] * (GLA_HV // LANES), axis=1)
                    st_ref[...] = st * dec + lax.dot_general(
                        ku_scr[rows, :], vb[rows, :], TN_DIMS, preferred_element_type=F32)
    else:
        sf_out[...] = lax.dot_general(kuf[...], vb[...], TN_DIMS, preferred_element_type=F32)
        sb_out[...] = lax.dot_general(kub[...], vb[...], TN_DIMS, preferred_element_type=F32)
    og_ref[...] = _rms(og_ref[...] + ob[...], gn_ref[...])


def _gla(proj, aproj, waf, wab, baf, bab, gn, t, s0f=None, s0b=None):
    m = proj.shape[0]
    nb = m // t
    has_state = s0f is not None
    assert t % GLA_SUPER == 0 and (has_state or t == GLA_SUPER)
    sc, c = GLA_SUPER, GLA_CHUNK
    st_shape_vmem = (GLA_HK, GLA_HV) if has_state else (8, LANES)
    per_direction = [
        pltpu.VMEM((t, GLA_HK), BF16),
        pltpu.VMEM((t, GLA_HK), BF16),
        pltpu.VMEM((8 * (t // sc), GLA_HK), F32),
        pltpu.VMEM((sc, GLA_HK), BF16),
        pltpu.VMEM((sc, GLA_HK), BF16),
        pltpu.VMEM((sc, GLA_HK), BF16),
        pltpu.VMEM((sc - 2 * c, GLA_HK), BF16),
        pltpu.VMEM((sc - 3 * c, GLA_HK), BF16),
        pltpu.VMEM((sc, sc), F32),
    ]
    st_spec = pl.BlockSpec((None, GLA_HK, GLA_HV), lambda b, h: (b * GLA_HEADS + h, 0, 0))
    in_specs = [
        pl.BlockSpec((t, GLA_HK), lambda b, h: (b, OFF_GQ // GLA_HK + h)),
        pl.BlockSpec((t, GLA_HK), lambda b, h: (b, OFF_GK // GLA_HK + h)),
        pl.BlockSpec((t, GLA_HV), lambda b, h: (b, OFF_GV // GLA_HV + h)),
        pl.BlockSpec((t, LANES), lambda b, h: (b, 0)),
        pl.BlockSpec((LANES, GLA_HK), lambda b, h: (0, h)),
        pl.BlockSpec((LANES, GLA_HK), lambda b, h: (0, h)),
        pl.BlockSpec((1, GLA_HK), lambda b, h: (0, h)),
        pl.BlockSpec((1, GLA_HK), lambda b, h: (0, h)),
        pl.BlockSpec((1, GLA_HV), lambda b, h: (0, h)),
    ]
    args = [proj, proj, proj, aproj, waf, wab, baf, bab, gn]
    og_spec = pl.BlockSpec((t, GLA_HV), lambda b, h: (b, h))
    og_shape = jax.ShapeDtypeStruct((m, GLA_DV), F32)
    if has_state:
        in_specs += [st_spec, st_spec]
        args += [s0f, s0b]
        out_specs, out_shape = og_spec, og_shape
    else:
        st_shape = jax.ShapeDtypeStruct((nb * GLA_HEADS, GLA_HK, GLA_HV), F32)
        out_specs, out_shape = [og_spec, st_spec, st_spec], [og_shape, st_shape, st_shape]
    return pl.pallas_call(
        functools.partial(_gla_kernel, t=t, has_state=has_state),
        grid=(nb, GLA_HEADS),
        in_specs=in_specs,
        out_specs=out_specs,
        out_shape=out_shape,
        scratch_shapes=[
            pltpu.VMEM(st_shape_vmem, F32),
            pltpu.VMEM(st_shape_vmem, F32),
            pltpu.VMEM((t, GLA_HV), BF16),
            pltpu.VMEM((t, GLA_HV), F32),
        ] + 2 * per_direction,
        compiler_params=_cp(("arbitrary", "arbitrary"), 56),
        name="gla_state" if has_state else "gla_zero",
    )(*args)


MERGE_SLAB = (16, 1024)


def _merge_kernel(oa_ref, og_ref, gr_ref, ga_ref, gg_ref, o_ref):
    rows, cols = o_ref.shape
    for r0 in range(0, rows, MERGE_SLAB[0]):
        for c0 in range(0, cols, MERGE_SLAB[1]):
            sl = (slice(r0, r0 + MERGE_SLAB[0]), slice(c0, c0 + MERGE_SLAB[1]))
            gr = gr_ref[sl].astype(F32)
            o_gla = og_ref[sl] * (gr * _sigmoid(gr))
            merged = (_sigmoid(ga_ref[sl].astype(F32)) * oa_ref[sl].astype(F32)
                      + _sigmoid(gg_ref[sl].astype(F32)) * o_gla)
            o_ref[sl] = merged.astype(BF16)


MERGE_TM = 256


def _merge(o_att, og, gates):
    m = o_att.shape[0]
    blk = lambda cb: pl.BlockSpec((MERGE_TM, D_MODEL), lambda i: (i, cb))
    cols = ((OFF_GR - OFF_GR) // D_MODEL, (OFF_GATT - OFF_GR) // D_MODEL, (OFF_GGLA - OFF_GR) // D_MODEL)
    return pl.pallas_call(
        _merge_kernel,
        grid=(m // MERGE_TM,),
        in_specs=[blk(0), blk(0)] + [blk(cb) for cb in cols],
        out_specs=blk(0),
        out_shape=jax.ShapeDtypeStruct((m, D_MODEL), BF16),
        compiler_params=_cp(("arbitrary",), 40),
        name="branch_merge",
    )(o_att, og, gates, gates, gates)


def _merge_side(o_att, og, gates, rows):
    m = o_att.shape[0]
    cols = ((OFF_GR - OFF_GR) // D_MODEL, (OFF_GATT - OFF_GR) // D_MODEL, (OFF_GGLA - OFF_GR) // D_MODEL)
    return _Side(_merge_kernel, (o_att, og, gates, gates, gates), (0, 0) + cols,
                 (jax.ShapeDtypeStruct((m, D_MODEL), BF16),), rows, m // rows)


OUT_TM = 512
OUT_TN = 1024


def _out_kernel(mg_ref, w_ref, x_ref, gt_ref, g2_ref, sh_ref, sc_ref, x1_ref, h2_ref, x1_scr):
    j = pl.program_id(1)
    nj = D_MODEL // OUT_TN
    slab = MXU_ACC_ROWS * 512 // OUT_TN
    for r0 in range(0, OUT_TM, slab):
        rs = slice(r0, r0 + slab)
        x1 = x_ref[rs, :] + gt_ref[...] * _dot(mg_ref[rs, :], w_ref[...])
        x1_ref[rs, :] = x1
        x1_scr[j, rs, :] = x1

    @pl.when(j == nj - 1)
    def _():
        ssq = jnp.zeros((OUT_TM, 1), F32)
        for jj in range(nj):
            xs = x1_scr[jj]
            ssq = ssq + jnp.sum(xs * xs, axis=-1, keepdims=True)
        inv = lax.rsqrt(ssq / D_MODEL + EPS)
        for jj in range(nj):
            cs = slice(jj * OUT_TN, (jj + 1) * OUT_TN)
            y = x1_scr[jj] * inv * g2_ref[:, cs]
            h2_ref[:, cs] = (y * (1.0 + sc_ref[:, cs]) + sh_ref[:, cs]).astype(BF16)


def _out_proj(merged, w_out, x, gate1, g2, shift2, scale2, mod_off, rows_per_mod):
    m = x.shape[0]
    bpb = rows_per_mod // OUT_TM
    gate_spec = pl.BlockSpec((None, 1, OUT_TN), lambda i, j: (mod_off + i // bpb, 0, j))
    return pl.pallas_call(
        _out_kernel,
        grid=(m // OUT_TM, D_MODEL // OUT_TN),
        in_specs=[
            pl.BlockSpec((OUT_TM, D_MODEL), lambda i, j: (i, 0)),
            pl.BlockSpec((D_MODEL, OUT_TN), lambda i, j: (0, j)),
            pl.BlockSpec((OUT_TM, OUT_TN), lambda i, j: (i, j)),
            gate_spec,
            pl.BlockSpec((1, D_MODEL), lambda i, j: (0, 0)),
            _mod_spec(mod_off, bpb),
            _mod_spec(mod_off, bpb),
        ],
        out_specs=[
            pl.BlockSpec((OUT_TM, OUT_TN), lambda i, j: (i, j)),
            pl.BlockSpec((OUT_TM, D_MODEL), lambda i, j: (i, 0)),
        ],
        out_shape=[
            jax.ShapeDtypeStruct((m, D_MODEL), F32),
            jax.ShapeDtypeStruct((m, D_MODEL), BF16),
        ],
        scratch_shapes=[pltpu.VMEM((D_MODEL // OUT_TN, OUT_TM, OUT_TN), F32)],
        compiler_params=_cp(("arbitrary", "arbitrary"), 56),
        name="out_proj_residual_norm",
    )(merged, w_out, x, gate1, g2, shift2, scale2)


FFI_TM = 2048
FFI_TN = 256


def _ffn_in_kernel(h_ref, wg_ref, wu_ref, o_ref):
    wg = wg_ref[...].astype(BF16)
    wu = wu_ref[...].astype(BF16)
    slab = MXU_ACC_ROWS * 512 // FFI_TN
    for r0 in range(0, FFI_TM, slab):
        rs = slice(r0, r0 + slab)
        h = h_ref[rs, :]
        g = _dot(h, wg)
        u = _dot(h, wu)
        o_ref[rs, :] = (g * _sigmoid(g) * u).astype(BF16)


def _ffn_in(h2, w):
    m = h2.shape[0]
    return pl.pallas_call(
        _ffn_in_kernel,
        grid=(m // FFI_TM, D_FF // FFI_TN),
        in_specs=[
            pl.BlockSpec((FFI_TM, D_MODEL), lambda i, j: (i, 0)),
            pl.BlockSpec((D_MODEL, FFI_TN), lambda i, j: (0, j)),
            pl.BlockSpec((D_MODEL, FFI_TN), lambda i, j: (0, D_FF // FFI_TN + j)),
        ],
        out_specs=pl.BlockSpec((FFI_TM, FFI_TN), lambda i, j: (i, j)),
        out_shape=jax.ShapeDtypeStruct((m, D_FF), BF16),
        compiler_params=_cp(("arbitrary", "arbitrary"), 58),
        name="ffn_in_swiglu",
    )(h2, w, w)


FFO_TM = 512
FFO_TN = 512


def _ffn_out_kernel(a_ref, w_ref, x_ref, gt_ref, o_ref):
    o_ref[...] = x_ref[...] + gt_ref[...] * _dot(a_ref[...], w_ref[...])


def _ffn_out(act, w, x1, gate2, mod_off, rows_per_mod, side=None):
    m = x1.shape[0]
    bpb = rows_per_mod // FFO_TM
    in_specs = [
        pl.BlockSpec((FFO_TM, D_FF), lambda j, i: (i, 0)),
        pl.BlockSpec((D_FF, FFO_TN), lambda j, i: (0, j)),
        pl.BlockSpec((FFO_TM, FFO_TN), lambda j, i: (i, j)),
        pl.BlockSpec((None, 1, FFO_TN), lambda j, i: (mod_off + i // bpb, 0, j)),
    ]
    return _host_call(
        _ffn_out_kernel, (D_MODEL // FFO_TN, m // FFO_TM), in_specs,
        pl.BlockSpec((FFO_TM, FFO_TN), lambda j, i: (i, j)),
        jax.ShapeDtypeStruct((m, D_MODEL), F32), (act, w, x1, gate2), side, 60, "ffn_out_residual")


def _rope_tables(t):
    rows = t // GRID_W
    half = HEAD_DIM // 2
    row = jnp.repeat(jnp.arange(rows, dtype=F32), GRID_W)
    col = jnp.tile(jnp.arange(GRID_W, dtype=F32), rows)
    inv = ROPE_THETA ** (-jnp.arange(0, half, 2, dtype=F32) / half)
    ar = row[:, None] * inv[None, :]
    ac = col[:, None] * inv[None, :]
    cr, sr, cc, sc = jnp.cos(ar), jnp.sin(ar), jnp.cos(ac), jnp.sin(ac)
    z = jnp.zeros_like(sr)
    tab_c = jnp.concatenate([cr, cr, cc, cc], axis=1)
    tab_sa = jnp.concatenate([-sr, z, -sc, z], axis=1)
    tab_sb = jnp.concatenate([z, sr, z, sc], axis=1)
    return tab_c, tab_sa, tab_sb


def kernel(x_prompt, x_sample, c, cache_k, cache_v, state_gla_fwd, state_gla_bwd, c_ctx, w_ada, b_ada, norm1_g, norm2_g, w_in, q_norm_g, k_norm_g, attn_sink, w_a2_fwd, b_a_fwd, w_a2_bwd, b_a_bwd, gla_norm_g, w_out, w_ffn_in, w_ffn_out):
    assert w_ada.shape[0] == 1, "single trunk layer"
    cc = jnp.zeros((MOD_ROWS, D_MODEL), F32).at[0].set(c_ctx).at[1:1 + DEC_BATCH].set(c)
    mod_all = _ada(cc, w_ada[0], b_ada[0][None, :])
    mod = tuple(mod_all[:, i * D_MODEL:(i + 1) * D_MODEL].reshape(MOD_ROWS, 1, D_MODEL)
                for i in range(N_MOD))

    r = GLA_GATE_RANK
    wa = jnp.zeros((D_MODEL, LANES), BF16).at[:, :2 * r].set(w_in[0][:, D_WIDE:].astype(BF16))
    waf = jnp.zeros((LANES, GLA_DK), BF16).at[:r].set(w_a2_fwd[0].astype(BF16))
    wab = jnp.zeros((LANES, GLA_DK), BF16).at[r:2 * r].set(w_a2_bwd[0].astype(BF16))
    g1, g2 = norm1_g[0][None, :], norm2_g[0][None, :]
    w_in_t = w_in[0].T
    qg, kg, sink = q_norm_g[0][None, :], k_norm_g[0][None, :], attn_sink[0]
    gla_w = (waf, wab, b_a_fwd[0][None, :], b_a_bwd[0][None, :], gla_norm_g[0][None, :])
    shift1, scale1, gate1, shift2, scale2, gate2 = mod
    kvw = N_KV_HEADS * HEAD_DIM
    n_wide, n_gate = OFF_GR, D_WIDE - OFF_GR
    m_ctx, m_lat = BATCH * SEQ, DEC_BATCH * DEC_SEQ
    ctx_mod, lat_mod = (0, m_ctx), (1, DEC_SEQ)

    xp = x_prompt.reshape(m_ctx, D_MODEL)
    xs = x_sample.reshape(m_lat, D_MODEL)

    h_c, a_c = _norm_mod(xp, g1, shift1, scale1, wa, *ctx_mod)
    proj_c, w_fo = _proj(h_c, w_in_t, 0, n_wide, F32, _cast_side(w_ffn_out[0], 128))
    gates_c, w_o = _proj(h_c, w_in_t, n_wide, n_gate, BF16, _cast_side(w_out[0], 64))
    h_l, a_l = _norm_mod(xs, g1, shift1, scale1, wa, *lat_mod)
    oatt_c, new_k, new_v = _ctx_attn(proj_c, sink, qg, kg, after=h_l)
    og_c, s_f, s_b = _gla(proj_c, a_c, *gla_w, SEQ)

    merged_c = _merge(oatt_c, og_c, gates_c)
    proj_l = _proj(h_l, w_in_t, 0, n_wide, F32)
    gates_l = _proj(h_l, w_in_t, n_wide, n_gate, BF16)
    oatt_l = _lat_attn(
        proj_l,
        cache_k[:, 0].reshape(DEC_BATCH, PAST_LEN, kvw),
        cache_v[:, 0].reshape(DEC_BATCH, PAST_LEN, kvw),
        sink, qg, kg, *_rope_tables(DEC_SEQ))
    og_l = _gla(proj_l, a_l, *gla_w, DEC_SEQ,
                state_gla_fwd[:, 0].reshape(DEC_BATCH * GLA_HEADS, GLA_HK, GLA_HV),
                state_gla_bwd[:, 0].reshape(DEC_BATCH * GLA_HEADS, GLA_HK, GLA_HV))

    x1_c, h2_c = _out_proj(merged_c, w_o, xp, gate1, g2, shift2, scale2, *ctx_mod)
    act_c = _ffn_in(h2_c, w_ffn_in[0])
    yp, merged_l = _ffn_out(act_c, w_fo, x1_c, gate2, *ctx_mod,
                            side=_merge_side(oatt_l, og_l, gates_l, 64))

    x1_l, h2_l = _out_proj(merged_l, w_o, xs, gate1, g2, shift2, scale2, *lat_mod)
    act_l = _ffn_in(h2_l, w_ffn_in[0])
    ys = _ffn_out(act_l, w_fo, x1_l, gate2, *lat_mod)
    return (
        yp.reshape(BATCH, SEQ, D_MODEL),
        ys.reshape(DEC_BATCH, DEC_SEQ, D_MODEL),
        new_k.reshape(BATCH, 1, SEQ, N_KV_HEADS, HEAD_DIM),
        new_v.reshape(BATCH, 1, SEQ, N_KV_HEADS, HEAD_DIM),
        s_f.reshape(BATCH, 1, GLA_HEADS, GLA_HK, GLA_HV),
        s_b.reshape(BATCH, 1, GLA_HEADS, GLA_HK, GLA_HV),
    )
```
